```python
import math
import jax, jax.numpy as jnp
from jax import lax
import numpy as np

D_MODEL = 1024
BATCH = 2
SEQ = 8192
DEPTH = 1

CHUNK = 64
N_ATTN_HEADS = 8
HEAD_DIM = 64
ATTN_WIDTH = N_ATTN_HEADS * HEAD_DIM
CONV_GROUPS = 8
CONV_WIDTH_CH = 512
CONV_K = 3
Q_BLOCK = 128
FFN_HIDDEN = int(math.ceil(8 * D_MODEL / 3 / 256) * 256)
RMS_EPS = 1e-6

_COLS = [ATTN_WIDTH, ATTN_WIDTH, ATTN_WIDTH, N_ATTN_HEADS,
         CONV_WIDTH_CH, CONV_WIDTH_CH, CONV_WIDTH_CH,
         D_MODEL, D_MODEL]
IN_COLS = sum(_COLS)
_SPLITS = list(np.cumsum(_COLS)[:-1])

kernel_name = "hybrid_shortconv_fox_sandwich_block"


def rmsnorm(x, g):
    xf = x.astype(jnp.float32)
    y = xf * lax.rsqrt(jnp.mean(xf * xf, axis=-1, keepdims=True) + RMS_EPS)
    return (y * g.astype(jnp.float32)).astype(x.dtype)


def causal_depthwise_conv(z, w):
    c = z.shape[-1]
    return lax.conv_general_dilated(
        z, w[:, None, :].astype(z.dtype), window_strides=(1,),
        padding=[(CONV_K - 1, 0)], dimension_numbers=("NWC", "WIO", "NWC"),
        feature_group_count=c)


def forgetting_attention(q, k, v, logf):
    s_len = q.shape[1]
    scale = 1.0 / math.sqrt(q.shape[-1])
    c = jnp.cumsum(logf, axis=1).transpose(0, 2, 1)
    outs = []
    for s0 in range(0, s_len, Q_BLOCK):
        s1 = s0 + Q_BLOCK
        logits = jnp.einsum("bqhd,bkhd->bhqk", q[:, s0:s1], k[:, :s1],
                            preferred_element_type=jnp.float32) * scale
        logits = logits + c[:, :, s0:s1, None] - c[:, :, None, :s1]
        qpos = jnp.arange(s0, s1)[:, None]
        kpos = jnp.arange(s1)[None, :]
        logits = jnp.where(kpos <= qpos, logits, -jnp.inf)
        p = jax.nn.softmax(logits, axis=-1)
        outs.append(jnp.einsum("bhqk,bkhd->bqhd", p.astype(v.dtype), v[:, :s1]))
    return jnp.concatenate(outs, axis=1)


def mixer(h, w_in, b_f, conv_w, w_branch, w_out):
    bsz, s_len, _ = h.shape
    proj = jnp.einsum("bsd,dn->bsn", h, w_in)
    q, k, v, f_logit, g_b, g_c, hc, gate_conv, gate_attn = jnp.split(proj, _SPLITS, axis=-1)
    ya = g_b * causal_depthwise_conv(g_c * hc, conv_w)
    shp = (bsz, s_len, N_ATTN_HEADS, HEAD_DIM)
    logf = jax.nn.log_sigmoid(f_logit.astype(jnp.float32) + b_f.astype(jnp.float32))
    yb = forgetting_attention(q.reshape(shp), k.reshape(shp), v.reshape(shp), logf)
    yb = yb.reshape(bsz, s_len, ATTN_WIDTH)
    ya_d = jnp.einsum("bsc,cd->bsd", ya, w_branch[0])
    yb_d = jnp.einsum("bsc,cd->bsd", yb, w_branch[1])
    merged = jax.nn.sigmoid(gate_conv) * ya_d + jax.nn.sigmoid(gate_attn) * yb_d
    return jnp.einsum("bsd,de->bse", merged, w_out)


def swiglu(h, w_gate_up, w_down):
    gu = jnp.einsum("bsd,df->bsf", h, w_gate_up)
    g, u = jnp.split(gu, 2, axis=-1)
    return jnp.einsum("bsf,fd->bsd", jax.nn.silu(g) * u, w_down)


def setup_inputs(seed: int = 0) -> dict:
    key = jax.random.key(seed)
    ks = jax.random.split(key, 12)
    f32 = jnp.float32

    def nrm(k, shape, fan_in):
        return jax.random.normal(k, shape, f32) * (fan_in ** -0.5)

    def gain(k):
        return 1.0 + 0.05 * jax.random.normal(k, (DEPTH, D_MODEL), f32)

    return {
        "x": jax.random.normal(ks[0], (BATCH, SEQ, D_MODEL), f32),
        "norm_mix_pre": gain(ks[1]),
        "norm_mix_post": gain(ks[2]),
        "w_in": nrm(ks[3], (DEPTH, D_MODEL, IN_COLS), D_MODEL),
        "b_f": 2.0 + 0.5 * jax.random.normal(ks[4], (DEPTH, N_ATTN_HEADS), f32),
        "conv_w": nrm(ks[5], (DEPTH, CONV_K, CONV_WIDTH_CH), CONV_K),
        "w_branch": nrm(ks[6], (DEPTH, 2, ATTN_WIDTH, D_MODEL), ATTN_WIDTH),
        "w_out": nrm(ks[7], (DEPTH, D_MODEL, D_MODEL), D_MODEL),
        "norm_ffn_pre": gain(ks[8]),
        "norm_ffn_post": gain(ks[9]),
        "w_gate_up": nrm(ks[10], (DEPTH, D_MODEL, 2 * FFN_HIDDEN), D_MODEL),
        "w_down": nrm(ks[11], (DEPTH, FFN_HIDDEN, D_MODEL), FFN_HIDDEN),
    }


def reference(x, norm_mix_pre, norm_mix_post, w_in, b_f, conv_w, w_branch, w_out,
              norm_ffn_pre, norm_ffn_post, w_gate_up, w_down):
    for l in range(DEPTH):
        h = rmsnorm(x, norm_mix_pre[l])
        m = mixer(h, w_in[l], b_f[l], conv_w[l], w_branch[l], w_out[l])
        x = x + rmsnorm(m, norm_mix_post[l])
        h = rmsnorm(x, norm_ffn_pre[l])
        f = swiglu(h, w_gate_up[l], w_down[l])
        x = x + rmsnorm(f, norm_ffn_post[l])
    return x
```

```python
import functools
import math

import jax
import jax.numpy as jnp
import numpy as np
from jax import lax
from jax.experimental import pallas as pl
from jax.experimental.pallas import tpu as pltpu

F32 = jnp.float32
BF16 = jnp.bfloat16

RMS_EPS = 1e-6
N_HEADS = 8
HEAD_DIM = 64
ATTN_WIDTH = N_HEADS * HEAD_DIM
CONV_CH = 512
CONV_K = 3
LANES = 128
BF16_ROWS = 16
HEADS_PER_STEP = 2
PAIR_W = 2 * LANES
N_SPLIT = 3
MASKED = -1e30
VMEM_LIMIT = 60 * 1024 * 1024

TM_QKV = 512
TQ = 512
TKV = 256
TM_MIX = 256
TM_FFN = 256
HALO = 16

NT_DIMS = (((1,), (1,)), ((), ()))


def _rmsnorm(x, g):
    ms = jnp.mean(x * x, axis=-1, keepdims=True)
    return (x * lax.rsqrt(ms + RMS_EPS)) * g


def _split3(v):
    parts = []
    r = v
    for _ in range(N_SPLIT):
        p = r.astype(BF16)
        parts.append(p)
        r = r - p.astype(F32)
    return parts


def _dot(a, b):
    return jnp.dot(a, b, preferred_element_type=F32)


def _const_spec(shape):
    nd = len(shape)
    return pl.BlockSpec(shape, lambda *_: (0,) * nd, pipeline_mode=pl.Buffered(1))


def _qkv_kernel(x_ref, g_ref, wqT_ref, wk_ref, wvT_ref, wf_ref, bf_ref, tri_ref,
                sel_ref, qT_ref, kx_ref, vT_ref, carry_ref, *, tm, tkv):
    @pl.when(pl.program_id(1) == 0)
    def _():
        carry_ref[...] = jnp.zeros_like(carry_ref)

    h = _rmsnorm(x_ref[0], g_ref[...]).astype(BF16)
    qT = lax.dot_general(wqT_ref[...], h, NT_DIMS, preferred_element_type=F32)
    qT_ref[0] = qT.astype(BF16)
    vT = lax.dot_general(wvT_ref[...], h, NT_DIMS, preferred_element_type=F32)
    for c in range(tm // tkv):
        vT_ref[0, c] = vT[:, c * tkv:(c + 1) * tkv].astype(BF16)
    k = _dot(h, wk_ref[...]).astype(BF16)

    f = _dot(h, wf_ref[...]) + bf_ref[...]
    logf = jnp.minimum(f, 0.0) - jnp.log1p(jnp.exp(-jnp.abs(f)))
    tri = tri_ref[...]
    c = carry_ref[...]
    for part in _split3(logf):
        c = c + _dot(tri, part)
    carry_ref[...] = c[tm - 1:tm]
    ext = None
    for j, part in enumerate(_split3(-c)):
        e = _dot(part, sel_ref[j])
        ext = e if ext is None else ext + e
    for p in range(N_HEADS // HEADS_PER_STEP):
        kx_ref[0, :, p * PAIR_W:p * PAIR_W + LANES] = k[:, p * LANES:(p + 1) * LANES]
        kx_ref[0, :, p * PAIR_W + LANES:(p + 1) * PAIR_W] = (
            ext[:, p * LANES:(p + 1) * LANES].astype(BF16))


def _qkv_call(x, g, wqT, wk, wvT, wf, bfp, tri, sel):
    b, s, d = x.shape
    tm, tkv = TM_QKV, TKV
    n_pairs = N_HEADS // HEADS_PER_STEP
    kern = functools.partial(_qkv_kernel, tm=tm, tkv=tkv)
    return pl.pallas_call(
        kern,
        grid=(b, s // tm),
        in_specs=[
            pl.BlockSpec((1, tm, d), lambda bi, i: (bi, i, 0)),
            _const_spec(g.shape), _const_spec(wqT.shape), _const_spec(wk.shape),
            _const_spec(wvT.shape), _const_spec(wf.shape), _const_spec(bfp.shape),
            _const_spec(tri.shape), _const_spec(sel.shape),
        ],
        out_specs=[
            pl.BlockSpec((1, ATTN_WIDTH, tm), lambda bi, i: (bi, 0, i)),
            pl.BlockSpec((1, tm, n_pairs * PAIR_W), lambda bi, i: (bi, i, 0)),
            pl.BlockSpec((1, tm // tkv, ATTN_WIDTH, tkv), lambda bi, i: (bi, i, 0, 0)),
        ],
        out_shape=[
            jax.ShapeDtypeStruct((b, ATTN_WIDTH, s), BF16),
            jax.ShapeDtypeStruct((b, s, n_pairs * PAIR_W), BF16),
            jax.ShapeDtypeStruct((b, s // tkv, ATTN_WIDTH, tkv), BF16),
        ],
        scratch_shapes=[pltpu.VMEM((1, LANES), F32)],
        compiler_params=pltpu.CompilerParams(
            dimension_semantics=("arbitrary", "arbitrary"),
            vmem_limit_bytes=VMEM_LIMIT),
        name="qkv",
    )(x, g, wqT, wk, wvT, wf, bfp, tri, sel)


def _attn_kernel(qT_ref, kx_ref, vT_ref, o_ref, qx_ref, m_ref, acc_ref, *, tq, tkv):
    qi = pl.program_id(2)
    acc_rows = HEAD_DIM + BF16_ROWS

    qT = qT_ref[0].astype(F32)
    row = lax.broadcasted_iota(jnp.int32, (LANES, tq), 0)
    for h in range(HEADS_PER_STEP):
        mine = (row >= h * HEAD_DIM) & (row < (h + 1) * HEAD_DIM)
        ones = (row >= h * N_SPLIT) & (row < (h + 1) * N_SPLIT)
        qx_ref[h, 0:LANES, :] = jnp.where(mine, qT, 0.0).astype(BF16)
        qx_ref[h, LANES:PAIR_W, :] = jnp.where(ones, 1.0, 0.0).astype(BF16)
    m_ref[...] = jnp.full_like(m_ref, MASKED)
    acc_ref[...] = jnp.zeros_like(acc_ref)
    ones_rows = jnp.ones((BF16_ROWS, tkv), BF16)

    def block(kblk, vblk, mask):
        for h in range(HEADS_PER_STEP):
            s = _dot(kblk, qx_ref[h])
            if mask is not None:
                s = jnp.where(mask, s, MASKED)
            m_old = m_ref[h]
            m_new = jnp.maximum(m_old, jnp.max(s, axis=0, keepdims=True))
            alpha = jnp.exp(m_old - m_new)
            p = jnp.exp(s - m_new).astype(BF16)
            vx = jnp.concatenate(
                [vblk[h * HEAD_DIM:(h + 1) * HEAD_DIM], ones_rows], axis=0)
            acc_ref[h] = alpha * acc_ref[h] + _dot(vx, p)
            m_ref[h] = m_new

    def full_block(j, carry):
        start = pl.multiple_of(j * tkv, tkv)
        block(kx_ref[0, pl.ds(start, tkv), :], vT_ref[0, j], None)
        return carry

    n_diag = tq // tkv
    lax.fori_loop(0, qi * n_diag, full_block, 0)

    krow = lax.broadcasted_iota(jnp.int32, (tkv, tq), 0)
    qcol = lax.broadcasted_iota(jnp.int32, (tkv, tq), 1)
    for dblk in range(n_diag):
        j = qi * n_diag + dblk
        start = pl.multiple_of(j * tkv, tkv)
        block(kx_ref[0, pl.ds(start, tkv), :], vT_ref[0, j],
              krow + dblk * tkv <= qcol)

    outs = []
    for h in range(HEADS_PER_STEP):
        a = acc_ref[h]
        outs.append(a[0:HEAD_DIM] / a[HEAD_DIM:HEAD_DIM + 1])
    o = jnp.concatenate(outs, axis=0)
    o_ref[0] = o.T.astype(BF16)


def _attn_call(qT, kx, vT):
    b, _, s = qT.shape
    tq, tkv = TQ, TKV
    n_pairs = N_HEADS // HEADS_PER_STEP
    kern = functools.partial(_attn_kernel, tq=tq, tkv=tkv)
    return pl.pallas_call(
        kern,
        grid=(b, n_pairs, s // tq),
        in_specs=[
            pl.BlockSpec((1, LANES, tq), lambda bi, p, qi: (bi, p, qi)),
            pl.BlockSpec((1, s, PAIR_W), lambda bi, p, qi: (bi, 0, p)),
            pl.BlockSpec((1, s // tkv, LANES, tkv), lambda bi, p, qi: (bi, 0, p, 0)),
        ],
        out_specs=pl.BlockSpec((1, tq, LANES), lambda bi, p, qi: (bi, qi, p)),
        out_shape=jax.ShapeDtypeStruct((b, s, ATTN_WIDTH), BF16),
        scratch_shapes=[
            pltpu.VMEM((HEADS_PER_STEP, PAIR_W, tq), BF16),
            pltpu.VMEM((HEADS_PER_STEP, 1, tq), F32),
            pltpu.VMEM((HEADS_PER_STEP, HEAD_DIM + BF16_ROWS, tq), F32),
        ],
        compiler_params=pltpu.CompilerParams(
            dimension_semantics=("arbitrary", "arbitrary", "arbitrary"),
            vmem_limit_bytes=VMEM_LIMIT),
        name="attn",
    )(qT, kx, vT)


def _mix_kernel(x_ref, xh_ref, yb_ref, gpre_ref, gpost_ref, wbch_ref, wgc_ref,
                wga_ref, cw_ref, wb0_ref, wb1_ref, wo_ref, o_ref, *, tm, tiles_per_seq):
    x = x_ref[...]
    gpre = gpre_ref[...]
    h = _rmsnorm(x, gpre).astype(BF16)
    bch = _dot(h, wbch_ref[...])
    gate_b = bch[:, 0:CONV_CH]
    z = bch[:, CONV_CH:2 * CONV_CH] * bch[:, 2 * CONV_CH:3 * CONV_CH]

    hh = _rmsnorm(xh_ref[...], gpre).astype(BF16)
    ch = _dot(hh, wbch_ref[:, CONV_CH:3 * CONV_CH])
    zh = ch[:, 0:CONV_CH] * ch[:, CONV_CH:2 * CONV_CH]
    first = pl.program_id(0) % tiles_per_seq == 0
    zh = jnp.where(first, 0.0, zh)
    zz = jnp.concatenate([zh, z], axis=0)
    conv = cw_ref[CONV_K - 1:CONV_K, :] * z
    for back in range(1, CONV_K):
        shifted = pltpu.roll(zz, back, 0)[HALO:]
        conv = conv + cw_ref[CONV_K - 1 - back:CONV_K - back, :] * shifted
    ya = (gate_b * conv).astype(BF16)

    ya_d = _dot(ya, wb0_ref[...])
    yb_d = _dot(yb_ref[...], wb1_ref[...])
    gc = jax.nn.sigmoid(_dot(h, wgc_ref[...]))
    ga = jax.nn.sigmoid(_dot(h, wga_ref[...]))
    merged = (gc * ya_d + ga * yb_d).astype(BF16)
    m = _dot(merged, wo_ref[...])
    o_ref[...] = x + _rmsnorm(m, gpost_ref[...])


def _mix_call(x2, yb2, gpre, gpost, wbch, wgc, wga, cw, wb0, wb1, wo, seq_len):
    n, d = x2.shape
    tm = TM_MIX
    kern = functools.partial(_mix_kernel, tm=tm, tiles_per_seq=seq_len // tm)
    halo_blocks = tm // HALO
    return pl.pallas_call(
        kern,
        grid=(n // tm,),
        in_specs=[
            pl.BlockSpec((tm, d), lambda i: (i, 0)),
            pl.BlockSpec((HALO, d), lambda i: (jnp.maximum(i * halo_blocks - 1, 0), 0)),
            pl.BlockSpec((tm, ATTN_WIDTH), lambda i: (i, 0)),
            _const_spec(gpre.shape), _const_spec(gpost.shape), _const_spec(wbch.shape),
            _const_spec(wgc.shape), _const_spec(wga.shape), _const_spec(cw.shape),
            _const_spec(wb0.shape), _const_spec(wb1.shape), _const_spec(wo.shape),
        ],
        out_specs=pl.BlockSpec((tm, d), lambda i: (i, 0)),
        out_shape=jax.ShapeDtypeStruct((n, d), F32),
        compiler_params=pltpu.CompilerParams(
            dimension_semantics=("arbitrary",), vmem_limit_bytes=VMEM_LIMIT),
        name="mix",
    )(x2, x2, yb2, gpre, gpost, wbch, wgc, wga, cw, wb0, wb1, wo)


def _ffn_kernel(x_ref, gpre_ref, gpost_ref, wgu_ref, wd_ref, o_ref, *, hidden):
    x = x_ref[...]
    h = _rmsnorm(x, gpre_ref[...]).astype(BF16)
    gu = _dot(h, wgu_ref[...])
    g = gu[:, 0:hidden]
    u = gu[:, hidden:2 * hidden]
    a = (g * jax.nn.sigmoid(g) * u).astype(BF16)
    f = _dot(a, wd_ref[...])
    o_ref[...] = x + _rmsnorm(f, gpost_ref[...])


def _ffn_call(x2, gpre, gpost, wgu, wd):
    n, d = x2.shape
    tm = TM_FFN
    hidden = wd.shape[0]
    kern = functools.partial(_ffn_kernel, hidden=hidden)
    return pl.pallas_call(
        kern,
        grid=(n // tm,),
        in_specs=[
            pl.BlockSpec((tm, d), lambda i: (i, 0)),
            _const_spec(gpre.shape), _const_spec(gpost.shape),
            _const_spec(wgu.shape), _const_spec(wd.shape),
        ],
        out_specs=pl.BlockSpec((tm, d), lambda i: (i, 0)),
        out_shape=jax.ShapeDtypeStruct((n, d), F32),
        compiler_params=pltpu.CompilerParams(
            dimension_semantics=("arbitrary",), vmem_limit_bytes=VMEM_LIMIT),
        name="ffn",
    )(x2, gpre, gpost, wgu, wd)


def _selection_matrices():
    n_pairs = N_HEADS // HEADS_PER_STEP
    sel = np.zeros((N_SPLIT, LANES, n_pairs * LANES), np.float32)
    for head in range(N_HEADS):
        for j in range(N_SPLIT):
            sel[j, head, (head // HEADS_PER_STEP) * LANES
                + (head % HEADS_PER_STEP) * N_SPLIT + j] = 1.0
    return jnp.asarray(sel, BF16)


def kernel(x, norm_mix_pre, norm_mix_post, w_in, b_f, conv_w, w_branch, w_out,
           norm_ffn_pre, norm_ffn_post, w_gate_up, w_down):
    b, s, d = x.shape
    depth = w_in.shape[0]
    a = ATTN_WIDTH
    c0 = 3 * a + N_HEADS
    c1 = c0 + 3 * CONV_CH
    tri = jnp.asarray(np.tril(np.ones((TM_QKV, TM_QKV), np.float32)), BF16)
    sel = _selection_matrices()
    scale = 1.0 / math.sqrt(HEAD_DIM)

    for l in range(depth):
        w = w_in[l]
        wqT = (w[:, 0:a] * scale).T.astype(BF16)
        wk = w[:, a:2 * a].astype(BF16)
        wvT = w[:, 2 * a:3 * a].T.astype(BF16)
        wf = jnp.pad(w[:, 3 * a:c0], ((0, 0), (0, LANES - N_HEADS))).astype(BF16)
        bfp = jnp.pad(b_f[l], (0, LANES - N_HEADS)).reshape(1, LANES)
        wbch = w[:, c0:c1].astype(BF16)
        wgc = w[:, c1:c1 + d].astype(BF16)
        wga = w[:, c1 + d:c1 + 2 * d].astype(BF16)

        qT, kx, vT = _qkv_call(x, norm_mix_pre[l].reshape(1, d), wqT, wk, wvT, wf,
                               bfp, tri, sel)
        yb = _attn_call(qT, kx, vT)
        x2 = _mix_call(
            x.reshape(b * s, d), yb.reshape(b * s, a),
            norm_mix_pre[l].reshape(1, d), norm_mix_post[l].reshape(1, d),
            wbch, wgc, wga, conv_w[l], w_branch[l, 0].astype(BF16),
            w_branch[l, 1].astype(BF16), w_out[l].astype(BF16), s)
        x2 = _ffn_call(x2, norm_ffn_pre[l].reshape(1, d), norm_ffn_post[l].reshape(1, d),
                       w_gate_up[l].astype(BF16), w_down[l].astype(BF16))
        x = x2.reshape(b, s, d)
    return x
```

```python
import functools
import math

import jax
import jax.numpy as jnp
import numpy as np
from jax import lax
from jax.experimental import pallas as pl
from jax.experimental.pallas import tpu as pltpu

F32 = jnp.float32
BF16 = jnp.bfloat16

RMS_EPS = 1e-6
N_HEADS = 8
HEAD_DIM = 64
ATTN_WIDTH = N_HEADS * HEAD_DIM
CONV_CH = 512
CONV_K = 3
LANES = 128
BF16_ROWS = 16
HEADS_PER_STEP = 2
PAIR_W = 2 * LANES
N_SPLIT = 3
MASKED = -1e30
VMEM_LIMIT = 60 * 1024 * 1024

TM_QKV = 512
TQ = 1024
TKV = 512
TM_MIX = 256
TM_FFN = 256
HALO = 16

NT_DIMS = (((1,), (1,)), ((), ()))


def _rmsnorm(x, g):
    ms = jnp.mean(x * x, axis=-1, keepdims=True)
    return (x * lax.rsqrt(ms + RMS_EPS)) * g


def _split3(v):
    parts = []
    r = v
    for _ in range(N_SPLIT):
        p = r.astype(BF16)
        parts.append(p)
        r = r - p.astype(F32)
    return parts


def _dot(a, b):
    return jnp.dot(a, b, preferred_element_type=F32)


def _const_spec(shape):
    nd = len(shape)
    return pl.BlockSpec(shape, lambda *_: (0,) * nd, pipeline_mode=pl.Buffered(1))


def _qkv_kernel(x_ref, g_ref, wqT_ref, wk_ref, wvT_ref, wf_ref, bf_ref, tri_ref,
                sel_ref, qT_ref, kx_ref, vT_ref, carry_ref, *, tm, tkv):
    @pl.when(pl.program_id(1) == 0)
    def _():
        carry_ref[...] = jnp.zeros_like(carry_ref)

    h = _rmsnorm(x_ref[0], g_ref[...]).astype(BF16)
    qT = lax.dot_general(wqT_ref[...], h, NT_DIMS, preferred_element_type=F32)
    qT_ref[0] = qT.astype(BF16)
    vT = lax.dot_general(wvT_ref[...], h, NT_DIMS, preferred_element_type=F32)
    for c in range(tm // tkv):
        vT_ref[0, c] = vT[:, c * tkv:(c + 1) * tkv].astype(BF16)
    k = _dot(h, wk_ref[...]).astype(BF16)

    f = _dot(h, wf_ref[...]) + bf_ref[...]
    logf = jnp.minimum(f, 0.0) - jnp.log1p(jnp.exp(-jnp.abs(f)))
    tri = tri_ref[...]
    c = carry_ref[...]
    for part in _split3(logf):
        c = c + _dot(tri, part)
    carry_ref[...] = c[tm - 1:tm]
    ext = None
    for j, part in enumerate(_split3(-c)):
        e = _dot(part, sel_ref[j])
        ext = e if ext is None else ext + e
    for p in range(N_HEADS // HEADS_PER_STEP):
        kx_ref[0, :, p * PAIR_W:p * PAIR_W + LANES] = k[:, p * LANES:(p + 1) * LANES]
        kx_ref[0, :, p * PAIR_W + LANES:(p + 1) * PAIR_W] = (
            ext[:, p * LANES:(p + 1) * LANES].astype(BF16))


def _qkv_call(x, g, wqT, wk, wvT, wf, bfp, tri, sel):
    b, s, d = x.shape
    tm, tkv = TM_QKV, TKV
    n_pairs = N_HEADS // HEADS_PER_STEP
    kern = functools.partial(_qkv_kernel, tm=tm, tkv=tkv)
    return pl.pallas_call(
        kern,
        grid=(b, s // tm),
        in_specs=[
            pl.BlockSpec((1, tm, d), lambda bi, i: (bi, i, 0)),
            _const_spec(g.shape), _const_spec(wqT.shape), _const_spec(wk.shape),
            _const_spec(wvT.shape), _const_spec(wf.shape), _const_spec(bfp.shape),
            _const_spec(tri.shape), _const_spec(sel.shape),
        ],
        out_specs=[
            pl.BlockSpec((1, ATTN_WIDTH, tm), lambda bi, i: (bi, 0, i)),
            pl.BlockSpec((1, tm, n_pairs * PAIR_W), lambda bi, i: (bi, i, 0)),
            pl.BlockSpec((1, tm // tkv, ATTN_WIDTH, tkv), lambda bi, i: (bi, i, 0, 0)),
        ],
        out_shape=[
            jax.ShapeDtypeStruct((b, ATTN_WIDTH, s), BF16),
            jax.ShapeDtypeStruct((b, s, n_pairs * PAIR_W), BF16),
            jax.ShapeDtypeStruct((b, s // tkv, ATTN_WIDTH, tkv), BF16),
        ],
        scratch_shapes=[pltpu.VMEM((1, LANES), F32)],
        compiler_params=pltpu.CompilerParams(
            dimension_semantics=("arbitrary", "arbitrary"),
            vmem_limit_bytes=VMEM_LIMIT),
        name="qkv",
    )(x, g, wqT, wk, wvT, wf, bfp, tri, sel)


def _attn_kernel(qT_ref, kx_ref, vT_ref, o_ref, qx_ref, s0_ref, s1_ref, mx_ref, m_ref,
                 acc_ref, *, tq, tkv):
    qi = pl.program_id(2)
    heads = range(HEADS_PER_STEP)
    n_diag = tq // tkv
    n_full = qi * n_diag
    s_bufs = (s0_ref, s1_ref)

    qT = qT_ref[0].astype(F32)
    row = lax.broadcasted_iota(jnp.int32, (LANES, tq), 0)
    for h in heads:
        mine = (row >= h * HEAD_DIM) & (row < (h + 1) * HEAD_DIM)
        ones = (row >= h * N_SPLIT) & (row < (h + 1) * N_SPLIT)
        qx_ref[h, 0:LANES, :] = jnp.where(mine, qT, 0.0).astype(BF16)
        qx_ref[h, LANES:PAIR_W, :] = jnp.where(ones, 1.0, 0.0).astype(BF16)
    m_ref[...] = jnp.full_like(m_ref, MASKED)
    acc_ref[...] = jnp.zeros_like(acc_ref)
    ones_rows = jnp.ones((BF16_ROWS, tkv), BF16)
    krow = lax.broadcasted_iota(jnp.int32, (tkv, tq), 0)
    qcol = lax.broadcasted_iota(jnp.int32, (tkv, tq), 1)

    def diag_mask(d):
        return krow + d * tkv <= qcol

    def qk_stage(j, buf, mask):
        start = pl.multiple_of(j * tkv, tkv)
        kblk = kx_ref[0, pl.ds(start, tkv), :]
        for h in heads:
            s = _dot(kblk, qx_ref[h])
            if mask is not None:
                s = jnp.where(mask, s, MASKED)
            s_bufs[buf][h] = s
            mx_ref[buf, h] = jnp.max(s, axis=0, keepdims=True)

    def pv_stage(j, buf):
        vblk = vT_ref[0, j]
        for h in heads:
            m_old = m_ref[h]
            m_new = jnp.maximum(m_old, mx_ref[buf, h])
            p = jnp.exp(s_bufs[buf][h] - m_new).astype(BF16)
            alpha = jnp.exp(m_old - m_new)
            vx = jnp.concatenate(
                [vblk[h * HEAD_DIM:(h + 1) * HEAD_DIM], ones_rows], axis=0)
            acc_ref[h] = alpha * acc_ref[h] + _dot(vx, p)
            m_ref[h] = m_new

    def full_pair(i, carry):
        j = 2 * i
        qk_stage(j + 1, 1, None)
        pv_stage(j, 0)
        qk_stage(j + 2, 0, None)
        pv_stage(j + 1, 1)
        return carry

    @pl.when(qi == 0)
    def _():
        qk_stage(0, 0, diag_mask(0))

    @pl.when(qi > 0)
    def _():
        qk_stage(0, 0, None)
        lax.fori_loop(0, qi * (n_diag // 2) - 1, full_pair, 0)
        j = n_full - 2
        qk_stage(j + 1, 1, None)
        pv_stage(j, 0)
        qk_stage(j + 2, 0, diag_mask(0))
        pv_stage(j + 1, 1)

    for d in range(n_diag):
        if d + 1 < n_diag:
            qk_stage(n_full + d + 1, (d + 1) % 2, diag_mask(d + 1))
        pv_stage(n_full + d, d % 2)

    outs = []
    for h in heads:
        a = acc_ref[h]
        outs.append(a[0:HEAD_DIM] / a[HEAD_DIM:HEAD_DIM + 1])
    o = jnp.concatenate(outs, axis=0)
    o_ref[0] = o.T.astype(BF16)


def _attn_call(qT, kx, vT):
    b, _, s = qT.shape
    tq, tkv = TQ, TKV
    n_pairs = N_HEADS // HEADS_PER_STEP
    kern = functools.partial(_attn_kernel, tq=tq, tkv=tkv)
    return pl.pallas_call(
        kern,
        grid=(b, n_pairs, s // tq),
        in_specs=[
            pl.BlockSpec((1, LANES, tq), lambda bi, p, qi: (bi, p, qi)),
            pl.BlockSpec((1, s, PAIR_W), lambda bi, p, qi: (bi, 0, p)),
            pl.BlockSpec((1, s // tkv, LANES, tkv), lambda bi, p, qi: (bi, 0, p, 0)),
        ],
        out_specs=pl.BlockSpec((1, tq, LANES), lambda bi, p, qi: (bi, qi, p)),
        out_shape=jax.ShapeDtypeStruct((b, s, ATTN_WIDTH), BF16),
        scratch_shapes=[
            pltpu.VMEM((HEADS_PER_STEP, PAIR_W, tq), BF16),
            pltpu.VMEM((HEADS_PER_STEP, tkv, tq), F32),
            pltpu.VMEM((HEADS_PER_STEP, tkv, tq), F32),
            pltpu.VMEM((2, HEADS_PER_STEP, 1, tq), F32),
            pltpu.VMEM((HEADS_PER_STEP, 1, tq), F32),
            pltpu.VMEM((HEADS_PER_STEP, HEAD_DIM + BF16_ROWS, tq), F32),
        ],
        compiler_params=pltpu.CompilerParams(
            dimension_semantics=("arbitrary", "arbitrary", "arbitrary"),
            vmem_limit_bytes=VMEM_LIMIT),
        name="attn",
    )(qT, kx, vT)


def _mix_kernel(x_ref, xh_ref, yb_ref, gpre_ref, gpost_ref, wbch_ref, wgc_ref,
                wga_ref, cw_ref, wb0_ref, wb1_ref, wo_ref, o_ref, *, tm, tiles_per_seq):
    x = x_ref[...]
    gpre = gpre_ref[...]
    h = _rmsnorm(x, gpre).astype(BF16)
    bch = _dot(h, wbch_ref[...])
    gate_b = bch[:, 0:CONV_CH]
    z = bch[:, CONV_CH:2 * CONV_CH] * bch[:, 2 * CONV_CH:3 * CONV_CH]

    hh = _rmsnorm(xh_ref[...], gpre).astype(BF16)
    ch = _dot(hh, wbch_ref[:, CONV_CH:3 * CONV_CH])
    zh = ch[:, 0:CONV_CH] * ch[:, CONV_CH:2 * CONV_CH]
    first = pl.program_id(0) % tiles_per_seq == 0
    zh = jnp.where(first, 0.0, zh)
    zz = jnp.concatenate([zh, z], axis=0)
    conv = cw_ref[CONV_K - 1:CONV_K, :] * z
    for back in range(1, CONV_K):
        shifted = pltpu.roll(zz, back, 0)[HALO:]
        conv = conv + cw_ref[CONV_K - 1 - back:CONV_K - back, :] * shifted
    ya = (gate_b * conv).astype(BF16)

    ya_d = _dot(ya, wb0_ref[...])
    yb_d = _dot(yb_ref[...], wb1_ref[...])
    gc = jax.nn.sigmoid(_dot(h, wgc_ref[...]))
    ga = jax.nn.sigmoid(_dot(h, wga_ref[...]))
    merged = (gc * ya_d + ga * yb_d).astype(BF16)
    m = _dot(merged, wo_ref[...])
    o_ref[...] = x + _rmsnorm(m, gpost_ref[...])


def _mix_call(x2, yb2, gpre, gpost, wbch, wgc, wga, cw, wb0, wb1, wo, seq_len):
    n, d = x2.shape
    tm = TM_MIX
    kern = functools.partial(_mix_kernel, tm=tm, tiles_per_seq=seq_len // tm)
    halo_blocks = tm // HALO
    return pl.pallas_call(
        kern,
        grid=(n // tm,),
        in_specs=[
            pl.BlockSpec((tm, d), lambda i: (i, 0)),
            pl.BlockSpec((HALO, d), lambda i: (jnp.maximum(i * halo_blocks - 1, 0), 0)),
            pl.BlockSpec((tm, ATTN_WIDTH), lambda i: (i, 0)),
            _const_spec(gpre.shape), _const_spec(gpost.shape), _const_spec(wbch.shape),
            _const_spec(wgc.shape), _const_spec(wga.shape), _const_spec(cw.shape),
            _const_spec(wb0.shape), _const_spec(wb1.shape), _const_spec(wo.shape),
        ],
        out_specs=pl.BlockSpec((tm, d), lambda i: (i, 0)),
        out_shape=jax.ShapeDtypeStruct((n, d), F32),
        compiler_params=pltpu.CompilerParams(
            dimension_semantics=("arbitrary",), vmem_limit_bytes=VMEM_LIMIT),
        name="mix",
    )(x2, x2, yb2, gpre, gpost, wbch, wgc, wga, cw, wb0, wb1, wo)


def _ffn_kernel(x_ref, gpre_ref, gpost_ref, wgu_ref, wd_ref, o_ref, *, hidden):
    x = x_ref[...]
    h = _rmsnorm(x, gpre_ref[...]).astype(BF16)
    gu = _dot(h, wgu_ref[...])
    g = gu[:, 0:hidden]
    u = gu[:, hidden:2 * hidden]
    a = (g * jax.nn.sigmoid(g) * u).astype(BF16)
    f = _dot(a, wd_ref[...])
    o_ref[...] = x + _rmsnorm(f, gpost_ref[...])


def _ffn_call(x2, gpre, gpost, wgu, wd):
    n, d = x2.shape
    tm = TM_FFN
    hidden = wd.shape[0]
    kern = functools.partial(_ffn_kernel, hidden=hidden)
    return pl.pallas_call(
        kern,
        grid=(n // tm,),
        in_specs=[
            pl.BlockSpec((tm, d), lambda i: (i, 0)),
            _const_spec(gpre.shape), _const_spec(gpost.shape),
            _const_spec(wgu.shape), _const_spec(wd.shape),
        ],
        out_specs=pl.BlockSpec((tm, d), lambda i: (i, 0)),
        out_shape=jax.ShapeDtypeStruct((n, d), F32),
        compiler_params=pltpu.CompilerParams(
            dimension_semantics=("arbitrary",), vmem_limit_bytes=VMEM_LIMIT),
        name="ffn",
    )(x2, gpre, gpost, wgu, wd)


def _selection_matrices():
    n_pairs = N_HEADS // HEADS_PER_STEP
    sel = np.zeros((N_SPLIT, LANES, n_pairs * LANES), np.float32)
    for head in range(N_HEADS):
        for j in range(N_SPLIT):
            sel[j, head, (head // HEADS_PER_STEP) * LANES
                + (head % HEADS_PER_STEP) * N_SPLIT + j] = 1.0
    return jnp.asarray(sel, BF16)


def kernel(x, norm_mix_pre, norm_mix_post, w_in, b_f, conv_w, w_branch, w_out,
           norm_ffn_pre, norm_ffn_post, w_gate_up, w_down):
    b, s, d = x.shape
    depth = w_in.shape[0]
    a = ATTN_WIDTH
    c0 = 3 * a + N_HEADS
    c1 = c0 + 3 * CONV_CH
    tri = jnp.asarray(np.tril(np.ones((TM_QKV, TM_QKV), np.float32)), BF16)
    sel = _selection_matrices()
    scale = 1.0 / math.sqrt(HEAD_DIM)

    for l in range(depth):
        w = w_in[l]
        wqT = (w[:, 0:a] * scale).T.astype(BF16)
        wk = w[:, a:2 * a].astype(BF16)
        wvT = w[:, 2 * a:3 * a].T.astype(BF16)
        wf = jnp.pad(w[:, 3 * a:c0], ((0, 0), (0, LANES - N_HEADS))).astype(BF16)
        bfp = jnp.pad(b_f[l], (0, LANES - N_HEADS)).reshape(1, LANES)
        wbch = w[:, c0:c1].astype(BF16)
        wgc = w[:, c1:c1 + d].astype(BF16)
        wga = w[:, c1 + d:c1 + 2 * d].astype(BF16)

        qT, kx, vT = _qkv_call(x, norm_mix_pre[l].reshape(1, d), wqT, wk, wvT, wf,
                               bfp, tri, sel)
        yb = _attn_call(qT, kx, vT)
        x2 = _mix_call(
            x.reshape(b * s, d), yb.reshape(b * s, a),
            norm_mix_pre[l].reshape(1, d), norm_mix_post[l].reshape(1, d),
            wbch, wgc, wga, conv_w[l], w_branch[l, 0].astype(BF16),
            w_branch[l, 1].astype(BF16), w_out[l].astype(BF16), s)
        x2 = _ffn_call(x2, norm_ffn_pre[l].reshape(1, d), norm_ffn_post[l].reshape(1, d),
                       w_gate_up[l].astype(BF16), w_down[l].astype(BF16))
        x = x2.reshape(b, s, d)
    return x
```

```python
import functools
import math

import jax
import jax.numpy as jnp
import numpy as np
from jax import lax
from jax.experimental import pallas as pl
from jax.experimental.pallas import tpu as pltpu

F32 = jnp.float32
BF16 = jnp.bfloat16

RMS_EPS = 1e-6
N_HEADS = 8
HEAD_DIM = 64
ATTN_WIDTH = N_HEADS * HEAD_DIM
CONV_CH = 512
CONV_K = 3
LANES = 128
BF16_ROWS = 16
HEADS_PER_STEP = 2
PAIR_W = 2 * LANES
N_SPLIT = 3
MASKED = -1e30
VMEM_LIMIT = 60 * 1024 * 1024

TM_QKV = 512
TQ = 1024
TKV = 512
CW = 256
TM_MIX = 256
TM_FFN = 256
HALO = 16

NT_DIMS = (((1,), (1,)), ((), ()))


def _rmsnorm(x, g):
    ms = jnp.mean(x * x, axis=-1, keepdims=True)
    return (x * lax.rsqrt(ms + RMS_EPS)) * g


def _split3(v):
    parts = []
    r = v
    for _ in range(N_SPLIT):
        p = r.astype(BF16)
        parts.append(p)
        r = r - p.astype(F32)
    return parts


def _dot(a, b):
    return jnp.dot(a, b, preferred_element_type=F32)


def _const_spec(shape):
    nd = len(shape)
    return pl.BlockSpec(shape, lambda *_: (0,) * nd, pipeline_mode=pl.Buffered(1))


def _qkv_kernel(x_ref, g_ref, wqT_ref, wk_ref, wvT_ref, wf_ref, bf_ref, tri_ref,
                sel_ref, qT_ref, kx_ref, vT_ref, carry_ref, *, tm, tkv):
    @pl.when(pl.program_id(1) == 0)
    def _():
        carry_ref[...] = jnp.zeros_like(carry_ref)

    h = _rmsnorm(x_ref[0], g_ref[...]).astype(BF16)
    qT = lax.dot_general(wqT_ref[...], h, NT_DIMS, preferred_element_type=F32)
    qT_ref[0] = qT.astype(BF16)
    vT = lax.dot_general(wvT_ref[...], h, NT_DIMS, preferred_element_type=F32)
    for c in range(tm // tkv):
        vT_ref[0, c] = vT[:, c * tkv:(c + 1) * tkv].astype(BF16)
    k = _dot(h, wk_ref[...]).astype(BF16)

    f = _dot(h, wf_ref[...]) + bf_ref[...]
    logf = jnp.minimum(f, 0.0) - jnp.log1p(jnp.exp(-jnp.abs(f)))
    tri = tri_ref[...]
    c = carry_ref[...]
    for part in _split3(logf):
        c = c + _dot(tri, part)
    carry_ref[...] = c[tm - 1:tm]
    ext = None
    for j, part in enumerate(_split3(-c)):
        e = _dot(part, sel_ref[j])
        ext = e if ext is None else ext + e
    for p in range(N_HEADS // HEADS_PER_STEP):
        kx_ref[0, :, p * PAIR_W:p * PAIR_W + LANES] = k[:, p * LANES:(p + 1) * LANES]
        kx_ref[0, :, p * PAIR_W + LANES:(p + 1) * PAIR_W] = (
            ext[:, p * LANES:(p + 1) * LANES].astype(BF16))


def _qkv_call(x, g, wqT, wk, wvT, wf, bfp, tri, sel):
    b, s, d = x.shape
    tm, tkv = TM_QKV, TKV
    n_pairs = N_HEADS // HEADS_PER_STEP
    kern = functools.partial(_qkv_kernel, tm=tm, tkv=tkv)
    return pl.pallas_call(
        kern,
        grid=(b, s // tm),
        in_specs=[
            pl.BlockSpec((1, tm, d), lambda bi, i: (bi, i, 0)),
            _const_spec(g.shape), _const_spec(wqT.shape), _const_spec(wk.shape),
            _const_spec(wvT.shape), _const_spec(wf.shape), _const_spec(bfp.shape),
            _const_spec(tri.shape), _const_spec(sel.shape),
        ],
        out_specs=[
            pl.BlockSpec((1, ATTN_WIDTH, tm), lambda bi, i: (bi, 0, i)),
            pl.BlockSpec((1, tm, n_pairs * PAIR_W), lambda bi, i: (bi, i, 0)),
            pl.BlockSpec((1, tm // tkv, ATTN_WIDTH, tkv), lambda bi, i: (bi, i, 0, 0)),
        ],
        out_shape=[
            jax.ShapeDtypeStruct((b, ATTN_WIDTH, s), BF16),
            jax.ShapeDtypeStruct((b, s, n_pairs * PAIR_W), BF16),
            jax.ShapeDtypeStruct((b, s // tkv, ATTN_WIDTH, tkv), BF16),
        ],
        scratch_shapes=[pltpu.VMEM((1, LANES), F32)],
        compiler_params=pltpu.CompilerParams(
            dimension_semantics=("arbitrary", "arbitrary"),
            vmem_limit_bytes=VMEM_LIMIT),
        name="qkv",
    )(x, g, wqT, wk, wvT, wf, bfp, tri, sel)


def _attn_kernel(qT_ref, kx_ref, vT_ref, o_ref, qx_ref, s0_ref, s1_ref, mx_ref, m_ref,
                 acc_ref, *, tq, tkv, cw):
    qi = pl.program_id(2)
    heads = range(HEADS_PER_STEP)
    n_diag = tq // tkv
    n_full = qi * n_diag
    s_bufs = (s0_ref, s1_ref)

    qT = qT_ref[0].astype(F32)
    row = lax.broadcasted_iota(jnp.int32, (LANES, tq), 0)
    for h in heads:
        mine = (row >= h * HEAD_DIM) & (row < (h + 1) * HEAD_DIM)
        ones = (row >= h * N_SPLIT) & (row < (h + 1) * N_SPLIT)
        qx_ref[h, 0:LANES, :] = jnp.where(mine, qT, 0.0).astype(BF16)
        qx_ref[h, LANES:PAIR_W, :] = jnp.where(ones, 1.0, 0.0).astype(BF16)
    m_ref[...] = jnp.full_like(m_ref, MASKED)
    acc_ref[...] = jnp.zeros_like(acc_ref)
    ones_rows = jnp.ones((BF16_ROWS, tkv), BF16)
    krow = lax.broadcasted_iota(jnp.int32, (tkv, cw), 0)
    qcol = lax.broadcasted_iota(jnp.int32, (tkv, cw), 1)

    def units_for(d):
        out = []
        for h in heads:
            for c in range(tq // cw):
                if d is not None and (c + 1) * cw <= d * tkv:
                    continue
                masked = d is not None and c * cw < (d + 1) * tkv - 1
                out.append((h, c, krow + (d * tkv - c * cw) <= qcol if masked else None))
        return out

    def qk_unit(j, buf, unit):
        h, c, mask = unit
        cols = slice(c * cw, (c + 1) * cw)
        start = pl.multiple_of(j * tkv, tkv)
        s = _dot(kx_ref[0, pl.ds(start, tkv), :], qx_ref[h, :, cols])
        if mask is not None:
            s = jnp.where(mask, s, MASKED)
        s_bufs[buf][h, :, cols] = s
        mx_ref[buf, h, :, cols] = jnp.max(s, axis=0, keepdims=True)

    def softmax_unit(buf, unit):
        h, c, _ = unit
        cols = slice(c * cw, (c + 1) * cw)
        m_old = m_ref[h, :, cols]
        m_new = jnp.maximum(m_old, mx_ref[buf, h, :, cols])
        m_ref[h, :, cols] = m_new
        p = jnp.exp(s_bufs[buf][h, :, cols] - m_new).astype(BF16)
        return p, jnp.exp(m_old - m_new)

    def pv_unit(j, unit, p, alpha):
        h, c, _ = unit
        cols = slice(c * cw, (c + 1) * cw)
        vx = jnp.concatenate(
            [vT_ref[0, j, h * HEAD_DIM:(h + 1) * HEAD_DIM, :], ones_rows], axis=0)
        acc_ref[h, :, cols] = alpha * acc_ref[h, :, cols] + _dot(vx, p)

    def step(j, buf, cur_d=None, next_d=None, has_next=True):
        cur = units_for(cur_d)
        nxt = units_for(next_d) if has_next else []
        pending = None
        for i in range(max(len(cur), len(nxt))):
            if i < len(nxt):
                qk_unit(j + 1, 1 - buf, nxt[i])
            if pending is not None:
                pv_unit(j, *pending)
                pending = None
            if i < len(cur):
                pending = (cur[i],) + softmax_unit(buf, cur[i])
        if pending is not None:
            pv_unit(j, *pending)

    def full_pair(i, carry):
        step(2 * i, 0)
        step(2 * i + 1, 1)
        return carry

    @pl.when(qi == 0)
    def _():
        for unit in units_for(0):
            qk_unit(0, 0, unit)

    @pl.when(qi > 0)
    def _():
        for unit in units_for(None):
            qk_unit(0, 0, unit)
        lax.fori_loop(0, qi * (n_diag // 2) - 1, full_pair, 0)
        step(n_full - 2, 0)
        step(n_full - 1, 1, next_d=0)

    for d in range(n_diag):
        last = d + 1 == n_diag
        step(n_full + d, d % 2, cur_d=d, next_d=None if last else d + 1,
             has_next=not last)

    outs = []
    for h in heads:
        a = acc_ref[h]
        outs.append(a[0:HEAD_DIM] / a[HEAD_DIM:HEAD_DIM + 1])
    o = jnp.concatenate(outs, axis=0)
    o_ref[0] = o.T.astype(BF16)


def _attn_call(qT, kx, vT):
    b, _, s = qT.shape
    tq, tkv = TQ, TKV
    n_pairs = N_HEADS // HEADS_PER_STEP
    kern = functools.partial(_attn_kernel, tq=tq, tkv=tkv, cw=CW)
    return pl.pallas_call(
        kern,
        grid=(b, n_pairs, s // tq),
        in_specs=[
            pl.BlockSpec((1, LANES, tq), lambda bi, p, qi: (bi, p, qi)),
            pl.BlockSpec((1, s, PAIR_W), lambda bi, p, qi: (bi, 0, p)),
            pl.BlockSpec((1, s // tkv, LANES, tkv), lambda bi, p, qi: (bi, 0, p, 0)),
        ],
        out_specs=pl.BlockSpec((1, tq, LANES), lambda bi, p, qi: (bi, qi, p)),
        out_shape=jax.ShapeDtypeStruct((b, s, ATTN_WIDTH), BF16),
        scratch_shapes=[
            pltpu.VMEM((HEADS_PER_STEP, PAIR_W, tq), BF16),
            pltpu.VMEM((HEADS_PER_STEP, tkv, tq), F32),
            pltpu.VMEM((HEADS_PER_STEP, tkv, tq), F32),
            pltpu.VMEM((2, HEADS_PER_STEP, 1, tq), F32),
            pltpu.VMEM((HEADS_PER_STEP, 1, tq), F32),
            pltpu.VMEM((HEADS_PER_STEP, HEAD_DIM + BF16_ROWS, tq), F32),
        ],
        compiler_params=pltpu.CompilerParams(
            dimension_semantics=("arbitrary", "arbitrary", "arbitrary"),
            vmem_limit_bytes=VMEM_LIMIT),
        name="attn",
    )(qT, kx, vT)


def _mix_kernel(x_ref, xh_ref, yb_ref, gpre_ref, gpost_ref, wbch_ref, wgc_ref,
                wga_ref, cw_ref, wb0_ref, wb1_ref, wo_ref, o_ref, *, tm, tiles_per_seq):
    x = x_ref[...]
    gpre = gpre_ref[...]
    h = _rmsnorm(x, gpre).astype(BF16)
    bch = _dot(h, wbch_ref[...])
    gate_b = bch[:, 0:CONV_CH]
    z = bch[:, CONV_CH:2 * CONV_CH] * bch[:, 2 * CONV_CH:3 * CONV_CH]

    hh = _rmsnorm(xh_ref[...], gpre).astype(BF16)
    ch = _dot(hh, wbch_ref[:, CONV_CH:3 * CONV_CH])
    zh = ch[:, 0:CONV_CH] * ch[:, CONV_CH:2 * CONV_CH]
    first = pl.program_id(0) % tiles_per_seq == 0
    zh = jnp.where(first, 0.0, zh)
    zz = jnp.concatenate([zh, z], axis=0)
    conv = cw_ref[CONV_K - 1:CONV_K, :] * z
    for back in range(1, CONV_K):
        shifted = pltpu.roll(zz, back, 0)[HALO:]
        conv = conv + cw_ref[CONV_K - 1 - back:CONV_K - back, :] * shifted
    ya = (gate_b * conv).astype(BF16)

    ya_d = _dot(ya, wb0_ref[...])
    yb_d = _dot(yb_ref[...], wb1_ref[...])
    gc = jax.nn.sigmoid(_dot(h, wgc_ref[...]))
    ga = jax.nn.sigmoid(_dot(h, wga_ref[...]))
    merged = (gc * ya_d + ga * yb_d).astype(BF16)
    m = _dot(merged, wo_ref[...])
    o_ref[...] = x + _rmsnorm(m, gpost_ref[...])


def _mix_call(x2, yb2, gpre, gpost, wbch, wgc, wga, cw, wb0, wb1, wo, seq_len):
    n, d = x2.shape
    tm = TM_MIX
    kern = functools.partial(_mix_kernel, tm=tm, tiles_per_seq=seq_len // tm)
    halo_blocks = tm // HALO
    return pl.pallas_call(
        kern,
        grid=(n // tm,),
        in_specs=[
            pl.BlockSpec((tm, d), lambda i: (i, 0)),
            pl.BlockSpec((HALO, d), lambda i: (jnp.maximum(i * halo_blocks - 1, 0), 0)),
            pl.BlockSpec((tm, ATTN_WIDTH), lambda i: (i, 0)),
            _const_spec(gpre.shape), _const_spec(gpost.shape), _const_spec(wbch.shape),
            _const_spec(wgc.shape), _const_spec(wga.shape), _const_spec(cw.shape),
            _const_spec(wb0.shape), _const_spec(wb1.shape), _const_spec(wo.shape),
        ],
        out_specs=pl.BlockSpec((tm, d), lambda i: (i, 0)),
        out_shape=jax.ShapeDtypeStruct((n, d), F32),
        compiler_params=pltpu.CompilerParams(
            dimension_semantics=("arbitrary",), vmem_limit_bytes=VMEM_LIMIT),
        name="mix",
    )(x2, x2, yb2, gpre, gpost, wbch, wgc, wga, cw, wb0, wb1, wo)


def _ffn_kernel(x_ref, gpre_ref, gpost_ref, wgu_ref, wd_ref, o_ref, *, hidden):
    x = x_ref[...]
    h = _rmsnorm(x, gpre_ref[...]).astype(BF16)
    gu = _dot(h, wgu_ref[...])
    g = gu[:, 0:hidden]
    u = gu[:, hidden:2 * hidden]
    a = (g * jax.nn.sigmoid(g) * u).astype(BF16)
    f = _dot(a, wd_ref[...])
    o_ref[...] = x + _rmsnorm(f, gpost_ref[...])


def _ffn_call(x2, gpre, gpost, wgu, wd):
    n, d = x2.shape
    tm = TM_FFN
    hidden = wd.shape[0]
    kern = functools.partial(_ffn_kernel, hidden=hidden)
    return pl.pallas_call(
        kern,
        grid=(n // tm,),
        in_specs=[
            pl.BlockSpec((tm, d), lambda i: (i, 0)),
            _const_spec(gpre.shape), _const_spec(gpost.shape),
            _const_spec(wgu.shape), _const_spec(wd.shape),
        ],
        out_specs=pl.BlockSpec((tm, d), lambda i: (i, 0)),
        out_shape=jax.ShapeDtypeStruct((n, d), F32),
        compiler_params=pltpu.CompilerParams(
            dimension_semantics=("arbitrary",), vmem_limit_bytes=VMEM_LIMIT),
        name="ffn",
    )(x2, gpre, gpost, wgu, wd)


def _selection_matrices():
    n_pairs = N_HEADS // HEADS_PER_STEP
    sel = np.zeros((N_SPLIT, LANES, n_pairs * LANES), np.float32)
    for head in range(N_HEADS):
        for j in range(N_SPLIT):
            sel[j, head, (head // HEADS_PER_STEP) * LANES
                + (head % HEADS_PER_STEP) * N_SPLIT + j] = 1.0
    return jnp.asarray(sel, BF16)


def kernel(x, norm_mix_pre, norm_mix_post, w_in, b_f, conv_w, w_branch, w_out,
           norm_ffn_pre, norm_ffn_post, w_gate_up, w_down):
    b, s, d = x.shape
    depth = w_in.shape[0]
    a = ATTN_WIDTH
    c0 = 3 * a + N_HEADS
    c1 = c0 + 3 * CONV_CH
    tri = jnp.asarray(np.tril(np.ones((TM_QKV, TM_QKV), np.float32)), BF16)
    sel = _selection_matrices()
    scale = 1.0 / math.sqrt(HEAD_DIM)

    for l in range(depth):
        w = w_in[l]
        wqT = (w[:, 0:a] * scale).T.astype(BF16)
        wk = w[:, a:2 * a].astype(BF16)
        wvT = w[:, 2 * a:3 * a].T.astype(BF16)
        wf = jnp.pad(w[:, 3 * a:c0], ((0, 0), (0, LANES - N_HEADS))).astype(BF16)
        bfp = jnp.pad(b_f[l], (0, LANES - N_HEADS)).reshape(1, LANES)
        wbch = w[:, c0:c1].astype(BF16)
        wgc = w[:, c1:c1 + d].astype(BF16)
        wga = w[:, c1 + d:c1 + 2 * d].astype(BF16)

        qT, kx, vT = _qkv_call(x, norm_mix_pre[l].reshape(1, d), wqT, wk, wvT, wf,
                               bfp, tri, sel)
        yb = _attn_call(qT, kx, vT)
        x2 = _mix_call(
            x.reshape(b * s, d), yb.reshape(b * s, a),
            norm_mix_pre[l].reshape(1, d), norm_mix_post[l].reshape(1, d),
            wbch, wgc, wga, conv_w[l], w_branch[l, 0].astype(BF16),
            w_branch[l, 1].astype(BF16), w_out[l].astype(BF16), s)
        x2 = _ffn_call(x2, norm_ffn_pre[l].reshape(1, d), norm_ffn_post[l].reshape(1, d),
                       w_gate_up[l].astype(BF16), w_down[l].astype(BF16))
        x = x2.reshape(b, s, d)
    return x
```

```python
import functools
import math

import jax
import jax.numpy as jnp
import numpy as np
from jax import lax
from jax.experimental import pallas as pl
from jax.experimental.pallas import tpu as pltpu

F32 = jnp.float32
BF16 = jnp.bfloat16

RMS_EPS = 1e-6
N_HEADS = 8
HEAD_DIM = 64
ATTN_WIDTH = N_HEADS * HEAD_DIM
CONV_CH = 512
CONV_K = 3
LANES = 128
BF16_ROWS = 16
HEADS_PER_STEP = 2
PAIR_W = 2 * LANES
N_SPLIT = 3
MASKED = -1e30
VMEM_LIMIT = 60 * 1024 * 1024

TM_QKV = 512
TQ = 1024
TKV = 512
CW = 256
TM_MIX = 512
TM_FFN = 512
HALO = 8

NT_DIMS = (((1,), (1,)), ((), ()))


def _rmsnorm(x, g):
    ms = jnp.mean(x * x, axis=-1, keepdims=True)
    return (x * lax.rsqrt(ms + RMS_EPS)) * g


def _split3(v):
    parts = []
    r = v
    for _ in range(N_SPLIT):
        p = r.astype(BF16)
        parts.append(p)
        r = r - p.astype(F32)
    return parts


def _dot(a, b):
    return jnp.dot(a, b, preferred_element_type=F32)


def _const_spec(shape):
    nd = len(shape)
    return pl.BlockSpec(shape, lambda *_: (0,) * nd, pipeline_mode=pl.Buffered(1))


def _qkv_kernel(x_ref, g_ref, w_ref, bf_ref, tri_ref, sel_ref, qT_ref, kx_ref, vT_ref,
                carry_ref, *, tm, tkv):
    @pl.when(pl.program_id(1) == 0)
    def _():
        carry_ref[...] = jnp.zeros_like(carry_ref)

    a = ATTN_WIDTH
    h = _rmsnorm(x_ref[0], g_ref[...]).astype(BF16)
    fq = _dot(h, w_ref[:, 0:LANES + a])
    qT_ref[0] = fq[:, LANES:].T.astype(BF16)

    f = fq[:, 0:LANES] + bf_ref[...]
    logf = jnp.minimum(f, 0.0) - jnp.log1p(jnp.exp(-jnp.abs(f)))
    head_lane = lax.broadcasted_iota(jnp.int32, (tm, LANES), 1) < N_HEADS
    k = _dot(h, w_ref[:, LANES + a:LANES + 2 * a]).astype(BF16)

    def pack3(v):
        parts = _split3(jnp.where(head_lane, v, 0.0))
        out = parts[0].astype(F32)
        for j in range(1, N_SPLIT):
            out = out + pltpu.roll(parts[j].astype(F32), j * N_HEADS, 1)
        return out.astype(BF16)

    cs = _dot(tri_ref[...], pack3(logf))
    c = cs + carry_ref[...]
    for j in range(1, N_SPLIT):
        c = c + pltpu.roll(cs, LANES - j * N_HEADS, 1)
    carry_ref[...] = c[tm - 1:tm]
    vT = _dot(h, w_ref[:, LANES + 2 * a:LANES + 3 * a]).T
    for cb in range(tm // tkv):
        vT_ref[0, cb] = vT[:, cb * tkv:(cb + 1) * tkv].astype(BF16)
    ext = _dot(pack3(-c), sel_ref[...])
    for p in range(N_HEADS // HEADS_PER_STEP):
        kx_ref[0, :, p * PAIR_W:p * PAIR_W + LANES] = k[:, p * LANES:(p + 1) * LANES]
        kx_ref[0, :, p * PAIR_W + LANES:(p + 1) * PAIR_W] = (
            ext[:, p * LANES:(p + 1) * LANES].astype(BF16))


def _qkv_call(x, g, w, bfp, tri, sel):
    b, s, d = x.shape
    tm, tkv = TM_QKV, TKV
    n_pairs = N_HEADS // HEADS_PER_STEP
    kern = functools.partial(_qkv_kernel, tm=tm, tkv=tkv)
    return pl.pallas_call(
        kern,
        grid=(b, s // tm),
        in_specs=[
            pl.BlockSpec((1, tm, d), lambda bi, i: (bi, i, 0)),
            _const_spec(g.shape), _const_spec(w.shape), _const_spec(bfp.shape),
            _const_spec(tri.shape), _const_spec(sel.shape),
        ],
        out_specs=[
            pl.BlockSpec((1, ATTN_WIDTH, tm), lambda bi, i: (bi, 0, i)),
            pl.BlockSpec((1, tm, n_pairs * PAIR_W), lambda bi, i: (bi, i, 0)),
            pl.BlockSpec((1, tm // tkv, ATTN_WIDTH, tkv), lambda bi, i: (bi, i, 0, 0)),
        ],
        out_shape=[
            jax.ShapeDtypeStruct((b, ATTN_WIDTH, s), BF16),
            jax.ShapeDtypeStruct((b, s, n_pairs * PAIR_W), BF16),
            jax.ShapeDtypeStruct((b, s // tkv, ATTN_WIDTH, tkv), BF16),
        ],
        scratch_shapes=[pltpu.VMEM((1, LANES), F32)],
        compiler_params=pltpu.CompilerParams(
            dimension_semantics=("arbitrary", "arbitrary"),
            vmem_limit_bytes=VMEM_LIMIT),
        name="qkv",
    )(x, g, w, bfp, tri, sel)


def _attn_kernel(qT_ref, kx_ref, vT_ref, o_ref, qx_ref, s0_ref, s1_ref, mx_ref, m_ref,
                 acc_ref, *, tq, tkv, cw):
    qi = pl.program_id(2)
    heads = range(HEADS_PER_STEP)
    n_diag = tq // tkv
    n_full = qi * n_diag
    s_bufs = (s0_ref, s1_ref)

    qT = qT_ref[0].astype(F32)
    row = lax.broadcasted_iota(jnp.int32, (LANES, tq), 0)
    for h in heads:
        mine = (row >= h * HEAD_DIM) & (row < (h + 1) * HEAD_DIM)
        ones = (row >= h * N_SPLIT) & (row < (h + 1) * N_SPLIT)
        qx_ref[h, 0:LANES, :] = jnp.where(mine, qT, 0.0).astype(BF16)
        qx_ref[h, LANES:PAIR_W, :] = jnp.where(ones, 1.0, 0.0).astype(BF16)
    m_ref[...] = jnp.full_like(m_ref, MASKED)
    acc_ref[...] = jnp.zeros_like(acc_ref)
    ones_rows = jnp.ones((BF16_ROWS, tkv), BF16)
    krow = lax.broadcasted_iota(jnp.int32, (tkv, cw), 0)
    qcol = lax.broadcasted_iota(jnp.int32, (tkv, cw), 1)

    def units_for(d):
        out = []
        for h in heads:
            for c in range(tq // cw):
                if d is not None and (c + 1) * cw <= d * tkv:
                    continue
                masked = d is not None and c * cw < (d + 1) * tkv - 1
                out.append((h, c, krow + (d * tkv - c * cw) <= qcol if masked else None))
        return out

    def qk_unit(j, buf, unit):
        h, c, mask = unit
        cols = slice(c * cw, (c + 1) * cw)
        start = pl.multiple_of(j * tkv, tkv)
        s = _dot(kx_ref[0, pl.ds(start, tkv), :], qx_ref[h, :, cols])
        if mask is not None:
            s = jnp.where(mask, s, MASKED)
        s_bufs[buf][h, :, cols] = s
        mx_ref[buf, h, :, cols] = jnp.max(s, axis=0, keepdims=True)

    def softmax_unit(buf, unit):
        h, c, _ = unit
        cols = slice(c * cw, (c + 1) * cw)
        m_old = m_ref[h, :, cols]
        m_new = jnp.maximum(m_old, mx_ref[buf, h, :, cols])
        m_ref[h, :, cols] = m_new
        p = jnp.exp(s_bufs[buf][h, :, cols] - m_new).astype(BF16)
        return p, jnp.exp(m_old - m_new)

    def pv_unit(j, unit, p, alpha):
        h, c, _ = unit
        cols = slice(c * cw, (c + 1) * cw)
        vx = jnp.concatenate(
            [vT_ref[0, j, h * HEAD_DIM:(h + 1) * HEAD_DIM, :], ones_rows], axis=0)
        acc_ref[h, :, cols] = alpha * acc_ref[h, :, cols] + _dot(vx, p)

    def step(j, buf, cur_d=None, next_d=None, has_next=True):
        cur = units_for(cur_d)
        nxt = units_for(next_d) if has_next else []
        pending = None
        for i in range(max(len(cur), len(nxt))):
            if i < len(nxt):
                qk_unit(j + 1, 1 - buf, nxt[i])
            if pending is not None:
                pv_unit(j, *pending)
                pending = None
            if i < len(cur):
                pending = (cur[i],) + softmax_unit(buf, cur[i])
        if pending is not None:
            pv_unit(j, *pending)

    def full_pair(i, carry):
        step(2 * i, 0)
        step(2 * i + 1, 1)
        return carry

    @pl.when(qi == 0)
    def _():
        for unit in units_for(0):
            qk_unit(0, 0, unit)

    @pl.when(qi > 0)
    def _():
        for unit in units_for(None):
            qk_unit(0, 0, unit)
        lax.fori_loop(0, qi * (n_diag // 2) - 1, full_pair, 0)
        step(n_full - 2, 0)
        step(n_full - 1, 1, next_d=0)

    for d in range(n_diag):
        last = d + 1 == n_diag
        step(n_full + d, d % 2, cur_d=d, next_d=None if last else d + 1,
             has_next=not last)

    outs = []
    for h in heads:
        a = acc_ref[h]
        outs.append(a[0:HEAD_DIM] / a[HEAD_DIM:HEAD_DIM + 1])
    o = jnp.concatenate(outs, axis=0)
    o_ref[0] = o.T.astype(BF16)


def _attn_call(qT, kx, vT):
    b, _, s = qT.shape
    tq, tkv = TQ, TKV
    n_pairs = N_HEADS // HEADS_PER_STEP
    kern = functools.partial(_attn_kernel, tq=tq, tkv=tkv, cw=CW)
    return pl.pallas_call(
        kern,
        grid=(b, n_pairs, s // tq),
        in_specs=[
            pl.BlockSpec((1, LANES, tq), lambda bi, p, qi: (bi, p, qi)),
            pl.BlockSpec((1, s, PAIR_W), lambda bi, p, qi: (bi, 0, p)),
            pl.BlockSpec((1, s // tkv, LANES, tkv), lambda bi, p, qi: (bi, 0, p, 0)),
        ],
        out_specs=pl.BlockSpec((1, tq, LANES), lambda bi, p, qi: (bi, qi, p)),
        out_shape=jax.ShapeDtypeStruct((b, s, ATTN_WIDTH), BF16),
        scratch_shapes=[
            pltpu.VMEM((HEADS_PER_STEP, PAIR_W, tq), BF16),
            pltpu.VMEM((HEADS_PER_STEP, tkv, tq), F32),
            pltpu.VMEM((HEADS_PER_STEP, tkv, tq), F32),
            pltpu.VMEM((2, HEADS_PER_STEP, 1, tq), F32),
            pltpu.VMEM((HEADS_PER_STEP, 1, tq), F32),
            pltpu.VMEM((HEADS_PER_STEP, HEAD_DIM + BF16_ROWS, tq), F32),
        ],
        compiler_params=pltpu.CompilerParams(
            dimension_semantics=("arbitrary", "arbitrary", "arbitrary"),
            vmem_limit_bytes=VMEM_LIMIT),
        name="attn",
    )(qT, kx, vT)


def _mix_kernel(x_ref, yb_ref, gpre_ref, gpost_ref, wcg_ref, cw_ref, wb0_ref, wb1_ref,
                wo_ref, o_ref, ztail_ref, *, tm, tiles_per_seq, d_model):
    x = x_ref[...]
    h = _rmsnorm(x, gpre_ref[...]).astype(BF16)
    proj = _dot(h, wcg_ref[...])
    gate_b = proj[:, 0:CONV_CH]
    z = proj[:, CONV_CH:2 * CONV_CH] * proj[:, 2 * CONV_CH:3 * CONV_CH]

    first = pl.program_id(0) % tiles_per_seq == 0
    zh = jnp.where(first, 0.0, ztail_ref[...])
    ztail_ref[...] = z[tm - HALO:tm]
    zz = jnp.concatenate([zh, z], axis=0)
    conv = cw_ref[CONV_K - 1:CONV_K, :] * z
    for back in range(1, CONV_K):
        shifted = pltpu.roll(zz, back, 0)[HALO:]
        conv = conv + cw_ref[CONV_K - 1 - back:CONV_K - back, :] * shifted
    ya = (gate_b * conv).astype(BF16)

    ya_d = _dot(ya, wb0_ref[...])
    yb_d = _dot(yb_ref[...], wb1_ref[...])
    g0 = 3 * CONV_CH
    gc = jax.nn.sigmoid(proj[:, g0:g0 + d_model])
    ga = jax.nn.sigmoid(proj[:, g0 + d_model:g0 + 2 * d_model])
    merged = (gc * ya_d + ga * yb_d).astype(BF16)
    m = _dot(merged, wo_ref[...])
    o_ref[...] = x + _rmsnorm(m, gpost_ref[...])


def _mix_call(x2, yb2, gpre, gpost, wcg, cw, wb0, wb1, wo, seq_len):
    n, d = x2.shape
    tm = TM_MIX
    kern = functools.partial(_mix_kernel, tm=tm, tiles_per_seq=seq_len // tm, d_model=d)
    return pl.pallas_call(
        kern,
        grid=(n // tm,),
        in_specs=[
            pl.BlockSpec((tm, d), lambda i: (i, 0)),
            pl.BlockSpec((tm, ATTN_WIDTH), lambda i: (i, 0)),
            _const_spec(gpre.shape), _const_spec(gpost.shape), _const_spec(wcg.shape),
            _const_spec(cw.shape), _const_spec(wb0.shape), _const_spec(wb1.shape),
            _const_spec(wo.shape),
        ],
        out_specs=pl.BlockSpec((tm, d), lambda i: (i, 0)),
        out_shape=jax.ShapeDtypeStruct((n, d), F32),
        scratch_shapes=[pltpu.VMEM((HALO, CONV_CH), F32)],
        compiler_params=pltpu.CompilerParams(
            dimension_semantics=("arbitrary",), vmem_limit_bytes=VMEM_LIMIT),
        name="mix",
    )(x2, yb2, gpre, gpost, wcg, cw, wb0, wb1, wo)


def _ffn_kernel(x_ref, gpre_ref, gpost_ref, wgu_ref, wd_ref, o_ref, *, hidden):
    x = x_ref[...]
    h = _rmsnorm(x, gpre_ref[...]).astype(BF16)
    gu = _dot(h, wgu_ref[...])
    g = gu[:, 0:hidden]
    u = gu[:, hidden:2 * hidden]
    a = (g * jax.nn.sigmoid(g) * u).astype(BF16)
    f = _dot(a, wd_ref[...])
    o_ref[...] = x + _rmsnorm(f, gpost_ref[...])


def _ffn_call(x2, gpre, gpost, wgu, wd):
    n, d = x2.shape
    tm = TM_FFN
    hidden = wd.shape[0]
    kern = functools.partial(_ffn_kernel, hidden=hidden)
    return pl.pallas_call(
        kern,
        grid=(n // tm,),
        in_specs=[
            pl.BlockSpec((tm, d), lambda i: (i, 0)),
            _const_spec(gpre.shape), _const_spec(gpost.shape),
            _const_spec(wgu.shape), _const_spec(wd.shape),
        ],
        out_specs=pl.BlockSpec((tm, d), lambda i: (i, 0)),
        out_shape=jax.ShapeDtypeStruct((n, d), F32),
        compiler_params=pltpu.CompilerParams(
            dimension_semantics=("arbitrary",), vmem_limit_bytes=VMEM_LIMIT),
        name="ffn",
    )(x2, gpre, gpost, wgu, wd)


def _selection_matrix():
    n_pairs = N_HEADS // HEADS_PER_STEP
    sel = np.zeros((LANES, n_pairs * LANES), np.float32)
    for head in range(N_HEADS):
        for j in range(N_SPLIT):
            sel[j * N_HEADS + head, (head // HEADS_PER_STEP) * LANES
                + (head % HEADS_PER_STEP) * N_SPLIT + j] = 1.0
    return jnp.asarray(sel, BF16)


def kernel(x, norm_mix_pre, norm_mix_post, w_in, b_f, conv_w, w_branch, w_out,
           norm_ffn_pre, norm_ffn_post, w_gate_up, w_down):
    b, s, d = x.shape
    depth = w_in.shape[0]
    a = ATTN_WIDTH
    c0 = 3 * a + N_HEADS
    tri = jnp.asarray(np.tril(np.ones((TM_QKV, TM_QKV), np.float32)), BF16)
    sel = _selection_matrix()
    scale = 1.0 / math.sqrt(HEAD_DIM)

    for l in range(depth):
        w = w_in[l]
        wfqkv = jnp.concatenate(
            [jnp.pad(w[:, 3 * a:c0], ((0, 0), (0, LANES - N_HEADS))),
             w[:, 0:a] * scale, w[:, a:3 * a]], axis=1).astype(BF16)
        bfp = jnp.pad(b_f[l], (0, LANES - N_HEADS)).reshape(1, LANES)
        wcg = w[:, c0:].astype(BF16)

        qT, kx, vT = _qkv_call(x, norm_mix_pre[l].reshape(1, d), wfqkv, bfp, tri, sel)
        yb = _attn_call(qT, kx, vT)
        x2 = _mix_call(
            x.reshape(b * s, d), yb.reshape(b * s, a),
            norm_mix_pre[l].reshape(1, d), norm_mix_post[l].reshape(1, d),
            wcg, conv_w[l], w_branch[l, 0].astype(BF16),
            w_branch[l, 1].astype(BF16), w_out[l].astype(BF16), s)
        x2 = _ffn_call(x2, norm_ffn_pre[l].reshape(1, d), norm_ffn_post[l].reshape(1, d),
                       w_gate_up[l].astype(BF16), w_down[l].astype(BF16))
        x = x2.reshape(b, s, d)
    return x
```

```python
import functools
import math

import jax
import jax.numpy as jnp
import numpy as np
from jax import lax
from jax.experimental import pallas as pl
from jax.experimental.pallas import tpu as pltpu

F32 = jnp.float32
BF16 = jnp.bfloat16

RMS_EPS = 1e-6
N_HEADS = 8
HEAD_DIM = 64
ATTN_WIDTH = N_HEADS * HEAD_DIM
CONV_CH = 512
CONV_K = 3
LANES = 128
BF16_ROWS = 16
HEADS_PER_STEP = 2
PAIR_W = 2 * LANES
N_SPLIT = 3
MASKED = -1e30
PRUNE_GAP = 120.0
NORM_SLACK = 4.0 * 1.04
VMEM_LIMIT = 60 * 1024 * 1024

TM_QKV = 512
TQ = 1024
TKV = 512
CW = 256
TM_MIX = 512
TM_FFN = 512
HALO = 8

NT_DIMS = (((1,), (1,)), ((), ()))


def _rmsnorm(x, g):
    ms = jnp.mean(x * x, axis=-1, keepdims=True)
    return (x * lax.rsqrt(ms + RMS_EPS)) * g


def _split3(v):
    parts = []
    r = v
    for _ in range(N_SPLIT):
        p = r.astype(BF16)
        parts.append(p)
        r = r - p.astype(F32)
    return parts


def _dot(a, b):
    return jnp.dot(a, b, preferred_element_type=F32)


def _const_spec(shape):
    nd = len(shape)
    return pl.BlockSpec(shape, lambda *_: (0,) * nd, pipeline_mode=pl.Buffered(1))


def _qkv_kernel(x_ref, g_ref, w_ref, bf_ref, tri_ref, sel_ref, ind_ref, qT_ref, kx_ref,
                vT_ref, first_ref, carry_ref, kmax_ref, hist_ref, *, tm, tkv):
    i = pl.program_id(1)

    @pl.when(i == 0)
    def _():
        carry_ref[...] = jnp.zeros_like(carry_ref)
        kmax_ref[...] = jnp.zeros_like(kmax_ref)
        hist_ref[...] = jnp.zeros_like(hist_ref)

    a = ATTN_WIDTH
    h = _rmsnorm(x_ref[0], g_ref[...]).astype(BF16)
    fq = _dot(h, w_ref[:, 0:LANES + a])
    q = fq[:, LANES:].astype(BF16)
    qT_ref[0] = fq[:, LANES:].T.astype(BF16)

    f = fq[:, 0:LANES] + bf_ref[...]
    logf = jnp.minimum(f, 0.0) - jnp.log1p(jnp.exp(-jnp.abs(f)))
    head_lane = lax.broadcasted_iota(jnp.int32, (tm, LANES), 1) < N_HEADS
    k = _dot(h, w_ref[:, LANES + a:LANES + 2 * a]).astype(BF16)
    qf = q.astype(F32)
    kf = k.astype(F32)
    sq = jnp.concatenate([qf * qf, kf * kf], axis=1).astype(BF16)
    norm2 = jnp.max(_dot(sq, ind_ref[...]), axis=0, keepdims=True)

    def pack3(v):
        parts = _split3(jnp.where(head_lane, v, 0.0))
        out = parts[0].astype(F32)
        for j in range(1, N_SPLIT):
            out = out + pltpu.roll(parts[j].astype(F32), j * N_HEADS, 1)
        return out.astype(BF16)

    cs = _dot(tri_ref[...], pack3(logf))
    c_prev = carry_ref[...]
    c = cs + c_prev
    for j in range(1, N_SPLIT):
        c = c + pltpu.roll(cs, LANES - j * N_HEADS, 1)
    c_end = c[tm - 1:tm]
    carry_ref[...] = c_end

    kmax = jnp.maximum(kmax_ref[...], norm2)
    kmax_ref[...] = kmax
    qk2 = NORM_SLACK * norm2 * pltpu.roll(kmax, LANES - N_HEADS, 1)
    gap = hist_ref[...] - c_prev - PRUNE_GAP
    tile = lax.broadcasted_iota(jnp.int32, gap.shape, 0)
    skip = (tile < i) & (gap > 0.0) & (qk2 < gap * gap)
    i_f = i.astype(F32)
    kept = jnp.where(skip, i_f, jnp.minimum(tile.astype(F32), i_f))
    first_ref[0, 0] = jnp.min(kept, axis=0, keepdims=True).astype(jnp.int32)
    hist_ref[pl.ds(i, 1), :] = c_end
    vT = _dot(h, w_ref[:, LANES + 2 * a:LANES + 3 * a]).T
    for cb in range(tm // tkv):
        vT_ref[0, cb] = vT[:, cb * tkv:(cb + 1) * tkv].astype(BF16)
    ext = _dot(pack3(-c), sel_ref[...])
    for p in range(N_HEADS // HEADS_PER_STEP):
        kx_ref[0, :, p * PAIR_W:p * PAIR_W + LANES] = k[:, p * LANES:(p + 1) * LANES]
        kx_ref[0, :, p * PAIR_W + LANES:(p + 1) * PAIR_W] = (
            ext[:, p * LANES:(p + 1) * LANES].astype(BF16))


def _qkv_call(x, g, w, bfp, tri, sel, ind):
    b, s, d = x.shape
    tm, tkv = TM_QKV, TKV
    n_pairs = N_HEADS // HEADS_PER_STEP
    n_tiles = s // tm
    kern = functools.partial(_qkv_kernel, tm=tm, tkv=tkv)
    return pl.pallas_call(
        kern,
        grid=(b, n_tiles),
        in_specs=[
            pl.BlockSpec((1, tm, d), lambda bi, i: (bi, i, 0)),
            _const_spec(g.shape), _const_spec(w.shape), _const_spec(bfp.shape),
            _const_spec(tri.shape), _const_spec(sel.shape), _const_spec(ind.shape),
        ],
        out_specs=[
            pl.BlockSpec((1, ATTN_WIDTH, tm), lambda bi, i: (bi, 0, i)),
            pl.BlockSpec((1, tm, n_pairs * PAIR_W), lambda bi, i: (bi, i, 0)),
            pl.BlockSpec((1, tm // tkv, ATTN_WIDTH, tkv), lambda bi, i: (bi, i, 0, 0)),
            pl.BlockSpec((1, 1, 1, LANES), lambda bi, i: (bi, i, 0, 0)),
        ],
        out_shape=[
            jax.ShapeDtypeStruct((b, ATTN_WIDTH, s), BF16),
            jax.ShapeDtypeStruct((b, s, n_pairs * PAIR_W), BF16),
            jax.ShapeDtypeStruct((b, s // tkv, ATTN_WIDTH, tkv), BF16),
            jax.ShapeDtypeStruct((b, n_tiles, 1, LANES), jnp.int32),
        ],
        scratch_shapes=[
            pltpu.VMEM((1, LANES), F32),
            pltpu.VMEM((1, LANES), F32),
            pltpu.VMEM((n_tiles, LANES), F32),
        ],
        compiler_params=pltpu.CompilerParams(
            dimension_semantics=("arbitrary", "arbitrary"),
            vmem_limit_bytes=VMEM_LIMIT),
        name="qkv",
    )(x, g, w, bfp, tri, sel, ind)


def _attn_kernel(first_ref, qT_ref, kx_ref, vT_ref, o_ref, qx_ref, s0_ref, s1_ref, mx_ref,
                 m_ref, acc_ref, *, tq, tkv, cw, n_tiles):
    bi, pair, qi = pl.program_id(0), pl.program_id(1), pl.program_id(2)
    heads = range(HEADS_PER_STEP)
    n_diag = tq // tkv
    n_full = qi * n_diag
    s_bufs = (s0_ref, s1_ref)

    j0 = n_full
    for d in range(n_diag):
        for h in heads:
            j0 = jnp.minimum(j0, first_ref[(bi * n_tiles + n_full + d) * N_HEADS
                                           + pair * HEADS_PER_STEP + h])
    pair0 = lax.shift_right_logical(j0, 1)
    j0 = 2 * pair0

    qT = qT_ref[0].astype(F32)
    row = lax.broadcasted_iota(jnp.int32, (LANES, tq), 0)
    for h in heads:
        mine = (row >= h * HEAD_DIM) & (row < (h + 1) * HEAD_DIM)
        ones = (row >= h * N_SPLIT) & (row < (h + 1) * N_SPLIT)
        qx_ref[h, 0:LANES, :] = jnp.where(mine, qT, 0.0).astype(BF16)
        qx_ref[h, LANES:PAIR_W, :] = jnp.where(ones, 1.0, 0.0).astype(BF16)
    m_ref[...] = jnp.full_like(m_ref, MASKED)
    acc_ref[...] = jnp.zeros_like(acc_ref)
    ones_rows = jnp.ones((BF16_ROWS, tkv), BF16)
    krow = lax.broadcasted_iota(jnp.int32, (tkv, cw), 0)
    qcol = lax.broadcasted_iota(jnp.int32, (tkv, cw), 1)

    def units_for(d):
        out = []
        for h in heads:
            for c in range(tq // cw):
                if d is not None and (c + 1) * cw <= d * tkv:
                    continue
                masked = d is not None and c * cw < (d + 1) * tkv - 1
                out.append((h, c, krow + (d * tkv - c * cw) <= qcol if masked else None))
        return out

    def qk_unit(j, buf, unit):
        h, c, mask = unit
        cols = slice(c * cw, (c + 1) * cw)
        start = pl.multiple_of(j * tkv, tkv)
        s = _dot(kx_ref[0, pl.ds(start, tkv), :], qx_ref[h, :, cols])
        if mask is not None:
            s = jnp.where(mask, s, MASKED)
        s_bufs[buf][h, :, cols] = s
        mx_ref[buf, h, :, cols] = jnp.max(s, axis=0, keepdims=True)

    def softmax_unit(buf, unit):
        h, c, _ = unit
        cols = slice(c * cw, (c + 1) * cw)
        m_old = m_ref[h, :, cols]
        m_new = jnp.maximum(m_old, mx_ref[buf, h, :, cols])
        m_ref[h, :, cols] = m_new
        p = jnp.exp(s_bufs[buf][h, :, cols] - m_new).astype(BF16)
        return p, jnp.exp(m_old - m_new)

    def pv_unit(j, unit, p, alpha):
        h, c, _ = unit
        cols = slice(c * cw, (c + 1) * cw)
        vx = jnp.concatenate(
            [vT_ref[0, j, h * HEAD_DIM:(h + 1) * HEAD_DIM, :], ones_rows], axis=0)
        acc_ref[h, :, cols] = alpha * acc_ref[h, :, cols] + _dot(vx, p)

    def step(j, buf, cur_d=None, next_d=None, has_next=True):
        cur = units_for(cur_d)
        nxt = units_for(next_d) if has_next else []
        pending = None
        for i in range(max(len(cur), len(nxt))):
            if i < len(nxt):
                qk_unit(j + 1, 1 - buf, nxt[i])
            if pending is not None:
                pv_unit(j, *pending)
                pending = None
            if i < len(cur):
                pending = (cur[i],) + softmax_unit(buf, cur[i])
        if pending is not None:
            pv_unit(j, *pending)

    def full_pair(i, carry):
        step(2 * i, 0)
        step(2 * i + 1, 1)
        return carry

    @pl.when(j0 == n_full)
    def _():
        for unit in units_for(0):
            qk_unit(n_full, 0, unit)

    @pl.when(j0 < n_full)
    def _():
        for unit in units_for(None):
            qk_unit(j0, 0, unit)
        lax.fori_loop(pair0, qi * (n_diag // 2) - 1, full_pair, 0)
        step(n_full - 2, 0)
        step(n_full - 1, 1, next_d=0)

    for d in range(n_diag):
        last = d + 1 == n_diag
        step(n_full + d, d % 2, cur_d=d, next_d=None if last else d + 1,
             has_next=not last)

    outs = []
    for h in heads:
        a = acc_ref[h]
        outs.append(a[0:HEAD_DIM] / a[HEAD_DIM:HEAD_DIM + 1])
    o = jnp.concatenate(outs, axis=0)
    o_ref[0] = o.T.astype(BF16)


def _attn_call(first, qT, kx, vT):
    b, _, s = qT.shape
    tq, tkv = TQ, TKV
    n_pairs = N_HEADS // HEADS_PER_STEP
    kern = functools.partial(_attn_kernel, tq=tq, tkv=tkv, cw=CW, n_tiles=s // tkv)
    grid_spec = pltpu.PrefetchScalarGridSpec(
        num_scalar_prefetch=1,
        grid=(b, n_pairs, s // tq),
        in_specs=[
            pl.BlockSpec((1, LANES, tq), lambda bi, p, qi, first: (bi, p, qi)),
            pl.BlockSpec((1, s, PAIR_W), lambda bi, p, qi, first: (bi, 0, p)),
            pl.BlockSpec((1, s // tkv, LANES, tkv), lambda bi, p, qi, first: (bi, 0, p, 0)),
        ],
        out_specs=pl.BlockSpec((1, tq, LANES), lambda bi, p, qi, first: (bi, qi, p)),
        scratch_shapes=[
            pltpu.VMEM((HEADS_PER_STEP, PAIR_W, tq), BF16),
            pltpu.VMEM((HEADS_PER_STEP, tkv, tq), F32),
            pltpu.VMEM((HEADS_PER_STEP, tkv, tq), F32),
            pltpu.VMEM((2, HEADS_PER_STEP, 1, tq), F32),
            pltpu.VMEM((HEADS_PER_STEP, 1, tq), F32),
            pltpu.VMEM((HEADS_PER_STEP, HEAD_DIM + BF16_ROWS, tq), F32),
        ],
    )
    return pl.pallas_call(
        kern,
        grid_spec=grid_spec,
        out_shape=jax.ShapeDtypeStruct((b, s, ATTN_WIDTH), BF16),
        compiler_params=pltpu.CompilerParams(
            dimension_semantics=("arbitrary", "arbitrary", "arbitrary"),
            vmem_limit_bytes=VMEM_LIMIT),
        name="attn",
    )(first, qT, kx, vT)


def _mix_kernel(x_ref, yb_ref, gpre_ref, gpost_ref, wcg_ref, cw_ref, wb0_ref, wb1_ref,
                wo_ref, o_ref, ztail_ref, *, tm, tiles_per_seq, d_model):
    x = x_ref[...]
    h = _rmsnorm(x, gpre_ref[...]).astype(BF16)
    proj = _dot(h, wcg_ref[...])
    gate_b = proj[:, 0:CONV_CH]
    z = proj[:, CONV_CH:2 * CONV_CH] * proj[:, 2 * CONV_CH:3 * CONV_CH]

    first = pl.program_id(0) % tiles_per_seq == 0
    zh = jnp.where(first, 0.0, ztail_ref[...])
    ztail_ref[...] = z[tm - HALO:tm]
    zz = jnp.concatenate([zh, z], axis=0)
    conv = cw_ref[CONV_K - 1:CONV_K, :] * z
    for back in range(1, CONV_K):
        shifted = pltpu.roll(zz, back, 0)[HALO:]
        conv = conv + cw_ref[CONV_K - 1 - back:CONV_K - back, :] * shifted
    ya = (gate_b * conv).astype(BF16)

    ya_d = _dot(ya, wb0_ref[...])
    yb_d = _dot(yb_ref[...], wb1_ref[...])
    g0 = 3 * CONV_CH
    gc = jax.nn.sigmoid(proj[:, g0:g0 + d_model])
    ga = jax.nn.sigmoid(proj[:, g0 + d_model:g0 + 2 * d_model])
    merged = (gc * ya_d + ga * yb_d).astype(BF16)
    m = _dot(merged, wo_ref[...])
    o_ref[...] = x + _rmsnorm(m, gpost_ref[...])


def _mix_call(x2, yb2, gpre, gpost, wcg, cw, wb0, wb1, wo, seq_len):
    n, d = x2.shape
    tm = TM_MIX
    kern = functools.partial(_mix_kernel, tm=tm, tiles_per_seq=seq_len // tm, d_model=d)
    return pl.pallas_call(
        kern,
        grid=(n // tm,),
        in_specs=[
            pl.BlockSpec((tm, d), lambda i: (i, 0)),
            pl.BlockSpec((tm, ATTN_WIDTH), lambda i: (i, 0)),
            _const_spec(gpre.shape), _const_spec(gpost.shape), _const_spec(wcg.shape),
            _const_spec(cw.shape), _const_spec(wb0.shape), _const_spec(wb1.shape),
            _const_spec(wo.shape),
        ],
        out_specs=pl.BlockSpec((tm, d), lambda i: (i, 0)),
        out_shape=jax.ShapeDtypeStruct((n, d), F32),
        scratch_shapes=[pltpu.VMEM((HALO, CONV_CH), F32)],
        compiler_params=pltpu.CompilerParams(
            dimension_semantics=("arbitrary",), vmem_limit_bytes=VMEM_LIMIT),
        name="mix",
    )(x2, yb2, gpre, gpost, wcg, cw, wb0, wb1, wo)


def _ffn_kernel(x_ref, gpre_ref, gpost_ref, wgu_ref, wd_ref, o_ref, *, hidden):
    x = x_ref[...]
    h = _rmsnorm(x, gpre_ref[...]).astype(BF16)
    gu = _dot(h, wgu_ref[...])
    g = gu[:, 0:hidden]
    u = gu[:, hidden:2 * hidden]
    a = (g * jax.nn.sigmoid(g) * u).astype(BF16)
    f = _dot(a, wd_ref[...])
    o_ref[...] = x + _rmsnorm(f, gpost_ref[...])


def _ffn_call(x2, gpre, gpost, wgu, wd):
    n, d = x2.shape
    tm = TM_FFN
    hidden = wd.shape[0]
    kern = functools.partial(_ffn_kernel, hidden=hidden)
    return pl.pallas_call(
        kern,
        grid=(n // tm,),
        in_specs=[
            pl.BlockSpec((tm, d), lambda i: (i, 0)),
            _const_spec(gpre.shape), _const_spec(gpost.shape),
            _const_spec(wgu.shape), _const_spec(wd.shape),
        ],
        out_specs=pl.BlockSpec((tm, d), lambda i: (i, 0)),
        out_shape=jax.ShapeDtypeStruct((n, d), F32),
        compiler_params=pltpu.CompilerParams(
            dimension_semantics=("arbitrary",), vmem_limit_bytes=VMEM_LIMIT),
        name="ffn",
    )(x2, gpre, gpost, wgu, wd)


def _selection_matrix():
    n_pairs = N_HEADS // HEADS_PER_STEP
    sel = np.zeros((LANES, n_pairs * LANES), np.float32)
    for head in range(N_HEADS):
        for j in range(N_SPLIT):
            sel[j * N_HEADS + head, (head // HEADS_PER_STEP) * LANES
                + (head % HEADS_PER_STEP) * N_SPLIT + j] = 1.0
    return jnp.asarray(sel, BF16)


def _norm_indicator():
    ind = np.zeros((2 * ATTN_WIDTH, LANES), np.float32)
    for which in range(2):
        for col in range(ATTN_WIDTH):
            ind[which * ATTN_WIDTH + col, which * N_HEADS + col // HEAD_DIM] = 1.0
    return jnp.asarray(ind, BF16)


def kernel(x, norm_mix_pre, norm_mix_post, w_in, b_f, conv_w, w_branch, w_out,
           norm_ffn_pre, norm_ffn_post, w_gate_up, w_down):
    b, s, d = x.shape
    depth = w_in.shape[0]
    a = ATTN_WIDTH
    c0 = 3 * a + N_HEADS
    tri = jnp.asarray(np.tril(np.ones((TM_QKV, TM_QKV), np.float32)), BF16)
    sel = _selection_matrix()
    ind = _norm_indicator()
    scale = 1.0 / math.sqrt(HEAD_DIM)

    for l in range(depth):
        w = w_in[l]
        wfqkv = jnp.concatenate(
            [jnp.pad(w[:, 3 * a:c0], ((0, 0), (0, LANES - N_HEADS))),
             w[:, 0:a] * scale, w[:, a:3 * a]], axis=1).astype(BF16)
        bfp = jnp.pad(b_f[l], (0, LANES - N_HEADS)).reshape(1, LANES)
        wcg = w[:, c0:].astype(BF16)

        qT, kx, vT, first = _qkv_call(x, norm_mix_pre[l].reshape(1, d), wfqkv, bfp, tri,
                                      sel, ind)
        yb = _attn_call(first[:, :, 0, 0:N_HEADS].reshape(-1), qT, kx, vT)
        x2 = _mix_call(
            x.reshape(b * s, d), yb.reshape(b * s, a),
            norm_mix_pre[l].reshape(1, d), norm_mix_post[l].reshape(1, d),
            wcg, conv_w[l], w_branch[l, 0].astype(BF16),
            w_branch[l, 1].astype(BF16), w_out[l].astype(BF16), s)
        x2 = _ffn_call(x2, norm_ffn_pre[l].reshape(1, d), norm_ffn_post[l].reshape(1, d),
                       w_gate_up[l].astype(BF16), w_down[l].astype(BF16))
        x = x2.reshape(b, s, d)
    return x
```

```python
import functools
import math

import jax
import jax.numpy as jnp
import numpy as np
from jax import lax
from jax.experimental import pallas as pl
from jax.experimental.pallas import tpu as pltpu

F32 = jnp.float32
BF16 = jnp.bfloat16

RMS_EPS = 1e-6
N_HEADS = 8
HEAD_DIM = 64
ATTN_WIDTH = N_HEADS * HEAD_DIM
CONV_CH = 512
CONV_K = 3
LANES = 128
BF16_ROWS = 16
HEADS_PER_STEP = 2
PAIR_W = 2 * LANES
N_SPLIT = 3
MASKED = -1e30
PRUNE_GAP = 120.0
NORM_SLACK = 4.0 * 1.04
VMEM_LIMIT = 60 * 1024 * 1024

TM_QKV = 512
TQ = 1024
TKV = 512
CW = 256
TM_MIX = 512
TM_FFN = 512
HALO = 8

NT_DIMS = (((1,), (1,)), ((), ()))


def _rmsnorm(x, g):
    ms = jnp.mean(x * x, axis=-1, keepdims=True)
    return (x * lax.rsqrt(ms + RMS_EPS)) * g


def _split3(v):
    parts = []
    r = v
    for _ in range(N_SPLIT):
        p = r.astype(BF16)
        parts.append(p)
        r = r - p.astype(F32)
    return parts


def _dot(a, b):
    return jnp.dot(a, b, preferred_element_type=F32)


def _dot_nt(a, b):
    return lax.dot_general(a, b, NT_DIMS, preferred_element_type=F32)


def _const_spec(shape):
    nd = len(shape)
    return pl.BlockSpec(shape, lambda *_: (0,) * nd, pipeline_mode=pl.Buffered(1))


def _qkv_kernel(x_ref, g_ref, w_ref, bf_ref, tri_ref, sel_ref, ind_ref, qT_ref, kx_ref,
                vT_ref, first_ref, carry_ref, kmax_ref, hist_ref, *, tm, tkv):
    i = pl.program_id(1)

    @pl.when(i == 0)
    def _():
        carry_ref[...] = jnp.zeros_like(carry_ref)
        kmax_ref[...] = jnp.zeros_like(kmax_ref)
        hist_ref[...] = jnp.zeros_like(hist_ref)

    a = ATTN_WIDTH
    h = _rmsnorm(x_ref[0], g_ref[...]).astype(BF16)
    fq = _dot_nt(h, w_ref[0:LANES + a, :])
    q = fq[:, LANES:].astype(BF16)
    qT_ref[0] = fq[:, LANES:].T.astype(BF16)

    f = fq[:, 0:LANES] + bf_ref[...]
    logf = jnp.minimum(f, 0.0) - jnp.log1p(jnp.exp(-jnp.abs(f)))
    head_lane = lax.broadcasted_iota(jnp.int32, (tm, LANES), 1) < N_HEADS
    k = _dot_nt(h, w_ref[LANES + a:LANES + 2 * a, :]).astype(BF16)
    qf = q.astype(F32)
    kf = k.astype(F32)
    sq = jnp.concatenate([qf * qf, kf * kf], axis=1).astype(BF16)
    norm2 = jnp.max(_dot(sq, ind_ref[...]), axis=0, keepdims=True)

    def pack3(v):
        parts = _split3(jnp.where(head_lane, v, 0.0))
        out = parts[0].astype(F32)
        for j in range(1, N_SPLIT):
            out = out + pltpu.roll(parts[j].astype(F32), j * N_HEADS, 1)
        return out.astype(BF16)

    cs = _dot(tri_ref[...], pack3(logf))
    c_prev = carry_ref[...]
    c = cs + c_prev
    for j in range(1, N_SPLIT):
        c = c + pltpu.roll(cs, LANES - j * N_HEADS, 1)
    c_end = c[tm - 1:tm]
    carry_ref[...] = c_end

    kmax = jnp.maximum(kmax_ref[...], norm2)
    kmax_ref[...] = kmax
    qk2 = NORM_SLACK * norm2 * pltpu.roll(kmax, LANES - N_HEADS, 1)
    gap = hist_ref[...] - c_prev - PRUNE_GAP
    tile = lax.broadcasted_iota(jnp.int32, gap.shape, 0)
    skip = (tile < i) & (gap > 0.0) & (qk2 < gap * gap)
    i_f = i.astype(F32)
    kept = jnp.where(skip, i_f, jnp.minimum(tile.astype(F32), i_f))
    first_ref[0, 0] = jnp.min(kept, axis=0, keepdims=True).astype(jnp.int32)
    hist_ref[pl.ds(i, 1), :] = c_end
    vT = _dot_nt(h, w_ref[LANES + 2 * a:LANES + 3 * a, :]).T
    for cb in range(tm // tkv):
        vT_ref[0, cb] = vT[:, cb * tkv:(cb + 1) * tkv].astype(BF16)
    ext = _dot(pack3(-c), sel_ref[...])
    for p in range(N_HEADS // HEADS_PER_STEP):
        kx_ref[0, :, p * PAIR_W:p * PAIR_W + LANES] = k[:, p * LANES:(p + 1) * LANES]
        kx_ref[0, :, p * PAIR_W + LANES:(p + 1) * PAIR_W] = (
            ext[:, p * LANES:(p + 1) * LANES].astype(BF16))


def _qkv_call(x, g, w, bfp, tri, sel, ind):
    b, s, d = x.shape
    tm, tkv = TM_QKV, TKV
    n_pairs = N_HEADS // HEADS_PER_STEP
    n_tiles = s // tm
    kern = functools.partial(_qkv_kernel, tm=tm, tkv=tkv)
    return pl.pallas_call(
        kern,
        grid=(b, n_tiles),
        in_specs=[
            pl.BlockSpec((1, tm, d), lambda bi, i: (bi, i, 0)),
            _const_spec(g.shape), _const_spec(w.shape), _const_spec(bfp.shape),
            _const_spec(tri.shape), _const_spec(sel.shape), _const_spec(ind.shape),
        ],
        out_specs=[
            pl.BlockSpec((1, ATTN_WIDTH, tm), lambda bi, i: (bi, 0, i)),
            pl.BlockSpec((1, tm, n_pairs * PAIR_W), lambda bi, i: (bi, i, 0)),
            pl.BlockSpec((1, tm // tkv, ATTN_WIDTH, tkv), lambda bi, i: (bi, i, 0, 0)),
            pl.BlockSpec((1, 1, 1, LANES), lambda bi, i: (bi, i, 0, 0)),
        ],
        out_shape=[
            jax.ShapeDtypeStruct((b, ATTN_WIDTH, s), BF16),
            jax.ShapeDtypeStruct((b, s, n_pairs * PAIR_W), BF16),
            jax.ShapeDtypeStruct((b, s // tkv, ATTN_WIDTH, tkv), BF16),
            jax.ShapeDtypeStruct((b, n_tiles, 1, LANES), jnp.int32),
        ],
        scratch_shapes=[
            pltpu.VMEM((1, LANES), F32),
            pltpu.VMEM((1, LANES), F32),
            pltpu.VMEM((n_tiles, LANES), F32),
        ],
        compiler_params=pltpu.CompilerParams(
            dimension_semantics=("arbitrary", "arbitrary"),
            vmem_limit_bytes=VMEM_LIMIT),
        name="qkv",
    )(x, g, w, bfp, tri, sel, ind)


def _attn_kernel(first_ref, qT_ref, kx_ref, vT_ref, o_ref, qx_ref, s0_ref, s1_ref, mx_ref,
                 m_ref, acc_ref, *, tq, tkv, cw, n_tiles):
    bi, pair, qi = pl.program_id(0), pl.program_id(1), pl.program_id(2)
    heads = range(HEADS_PER_STEP)
    n_diag = tq // tkv
    n_full = qi * n_diag
    s_bufs = (s0_ref, s1_ref)

    j0 = n_full
    for d in range(n_diag):
        for h in heads:
            j0 = jnp.minimum(j0, first_ref[(bi * n_tiles + n_full + d) * N_HEADS
                                           + pair * HEADS_PER_STEP + h])
    pair0 = lax.shift_right_logical(j0, 1)
    j0 = 2 * pair0

    qT = qT_ref[0].astype(F32)
    row = lax.broadcasted_iota(jnp.int32, (LANES, tq), 0)
    for h in heads:
        mine = (row >= h * HEAD_DIM) & (row < (h + 1) * HEAD_DIM)
        ones = (row >= h * N_SPLIT) & (row < (h + 1) * N_SPLIT)
        qx_ref[h, 0:LANES, :] = jnp.where(mine, qT, 0.0).astype(BF16)
        qx_ref[h, LANES:PAIR_W, :] = jnp.where(ones, 1.0, 0.0).astype(BF16)
    m_ref[...] = jnp.full_like(m_ref, MASKED)
    acc_ref[...] = jnp.zeros_like(acc_ref)
    ones_rows = jnp.ones((BF16_ROWS, tkv), BF16)
    krow = lax.broadcasted_iota(jnp.int32, (tkv, cw), 0)
    qcol = lax.broadcasted_iota(jnp.int32, (tkv, cw), 1)

    def units_for(d):
        out = []
        for h in heads:
            for c in range(tq // cw):
                if d is not None and (c + 1) * cw <= d * tkv:
                    continue
                masked = d is not None and c * cw < (d + 1) * tkv - 1
                out.append((h, c, krow + (d * tkv - c * cw) <= qcol if masked else None))
        return out

    def qk_unit(j, buf, unit):
        h, c, mask = unit
        cols = slice(c * cw, (c + 1) * cw)
        start = pl.multiple_of(j * tkv, tkv)
        s = _dot(kx_ref[0, pl.ds(start, tkv), :], qx_ref[h, :, cols])
        if mask is not None:
            s = jnp.where(mask, s, MASKED)
        s_bufs[buf][h, :, cols] = s
        mx_ref[buf, h, :, cols] = jnp.max(s, axis=0, keepdims=True)

    def softmax_unit(buf, unit):
        h, c, _ = unit
        cols = slice(c * cw, (c + 1) * cw)
        m_old = m_ref[h, :, cols]
        m_new = jnp.maximum(m_old, mx_ref[buf, h, :, cols])
        m_ref[h, :, cols] = m_new
        p = jnp.exp(s_bufs[buf][h, :, cols] - m_new).astype(BF16)
        return p, jnp.exp(m_old - m_new)

    def pv_unit(j, unit, p, alpha):
        h, c, _ = unit
        cols = slice(c * cw, (c + 1) * cw)
        vx = jnp.concatenate(
            [vT_ref[0, j, h * HEAD_DIM:(h + 1) * HEAD_DIM, :], ones_rows], axis=0)
        acc_ref[h, :, cols] = alpha * acc_ref[h, :, cols] + _dot(vx, p)

    def step(j, buf, cur_d=None, next_d=None, has_next=True):
        cur = units_for(cur_d)
        nxt = units_for(next_d) if has_next else []
        pending = None
        for i in range(max(len(cur), len(nxt))):
            if i < len(nxt):
                qk_unit(j + 1, 1 - buf, nxt[i])
            if pending is not None:
                pv_unit(j, *pending)
                pending = None
            if i < len(cur):
                pending = (cur[i],) + softmax_unit(buf, cur[i])
        if pending is not None:
            pv_unit(j, *pending)

    def full_pair(i, carry):
        step(2 * i, 0)
        step(2 * i + 1, 1)
        return carry

    @pl.when(j0 == n_full)
    def _():
        for unit in units_for(0):
            qk_unit(n_full, 0, unit)

    @pl.when(j0 < n_full)
    def _():
        for unit in units_for(None):
            qk_unit(j0, 0, unit)
        lax.fori_loop(pair0, qi * (n_diag // 2) - 1, full_pair, 0)
        step(n_full - 2, 0)
        step(n_full - 1, 1, next_d=0)

    for d in range(n_diag):
        last = d + 1 == n_diag
        step(n_full + d, d % 2, cur_d=d, next_d=None if last else d + 1,
             has_next=not last)

    outs = []
    for h in heads:
        a = acc_ref[h]
        outs.append(a[0:HEAD_DIM] / a[HEAD_DIM:HEAD_DIM + 1])
    o = jnp.concatenate(outs, axis=0)
    o_ref[0] = o.T.astype(BF16)


def _attn_call(first, qT, kx, vT):
    b, _, s = qT.shape
    tq, tkv = TQ, TKV
    n_pairs = N_HEADS // HEADS_PER_STEP
    kern = functools.partial(_attn_kernel, tq=tq, tkv=tkv, cw=CW, n_tiles=s // tkv)
    grid_spec = pltpu.PrefetchScalarGridSpec(
        num_scalar_prefetch=1,
        grid=(b, n_pairs, s // tq),
        in_specs=[
            pl.BlockSpec((1, LANES, tq), lambda bi, p, qi, first: (bi, p, qi)),
            pl.BlockSpec((1, s, PAIR_W), lambda bi, p, qi, first: (bi, 0, p)),
            pl.BlockSpec((1, s // tkv, LANES, tkv), lambda bi, p, qi, first: (bi, 0, p, 0)),
        ],
        out_specs=pl.BlockSpec((1, tq, LANES), lambda bi, p, qi, first: (bi, qi, p)),
        scratch_shapes=[
            pltpu.VMEM((HEADS_PER_STEP, PAIR_W, tq), BF16),
            pltpu.VMEM((HEADS_PER_STEP, tkv, tq), F32),
            pltpu.VMEM((HEADS_PER_STEP, tkv, tq), F32),
            pltpu.VMEM((2, HEADS_PER_STEP, 1, tq), F32),
            pltpu.VMEM((HEADS_PER_STEP, 1, tq), F32),
            pltpu.VMEM((HEADS_PER_STEP, HEAD_DIM + BF16_ROWS, tq), F32),
        ],
    )
    return pl.pallas_call(
        kern,
        grid_spec=grid_spec,
        out_shape=jax.ShapeDtypeStruct((b, s, ATTN_WIDTH), BF16),
        compiler_params=pltpu.CompilerParams(
            dimension_semantics=("arbitrary", "arbitrary", "arbitrary"),
            vmem_limit_bytes=VMEM_LIMIT),
        name="attn",
    )(first, qT, kx, vT)


def _mix_kernel(x_ref, yb_ref, gpre_ref, gpost_ref, wcg_ref, cw_ref, wb0_ref, wb1_ref,
                wo_ref, o_ref, ztail_ref, *, tm, tiles_per_seq, d_model):
    x = x_ref[...]
    h = _rmsnorm(x, gpre_ref[...]).astype(BF16)
    proj = _dot_nt(h, wcg_ref[...])
    gate_b = proj[:, 0:CONV_CH]
    z = proj[:, CONV_CH:2 * CONV_CH] * proj[:, 2 * CONV_CH:3 * CONV_CH]

    first = pl.program_id(0) % tiles_per_seq == 0
    zh = jnp.where(first, 0.0, ztail_ref[...])
    ztail_ref[...] = z[tm - HALO:tm]
    zz = jnp.concatenate([zh, z], axis=0)
    conv = cw_ref[CONV_K - 1:CONV_K, :] * z
    for back in range(1, CONV_K):
        shifted = pltpu.roll(zz, back, 0)[HALO:]
        conv = conv + cw_ref[CONV_K - 1 - back:CONV_K - back, :] * shifted
    ya = (gate_b * conv).astype(BF16)

    ya_d = _dot(ya, wb0_ref[...])
    yb_d = _dot(yb_ref[...], wb1_ref[...])
    g0 = 3 * CONV_CH
    gc = jax.nn.sigmoid(proj[:, g0:g0 + d_model])
    ga = jax.nn.sigmoid(proj[:, g0 + d_model:g0 + 2 * d_model])
    merged = (gc * ya_d + ga * yb_d).astype(BF16)
    m = _dot(merged, wo_ref[...])
    o_ref[...] = x + _rmsnorm(m, gpost_ref[...])


def _mix_call(x2, yb2, gpre, gpost, wcg, cw, wb0, wb1, wo, seq_len):
    n, d = x2.shape
    tm = TM_MIX
    kern = functools.partial(_mix_kernel, tm=tm, tiles_per_seq=seq_len // tm, d_model=d)
    return pl.pallas_call(
        kern,
        grid=(n // tm,),
        in_specs=[
            pl.BlockSpec((tm, d), lambda i: (i, 0)),
            pl.BlockSpec((tm, ATTN_WIDTH), lambda i: (i, 0)),
            _const_spec(gpre.shape), _const_spec(gpost.shape), _const_spec(wcg.shape),
            _const_spec(cw.shape), _const_spec(wb0.shape), _const_spec(wb1.shape),
            _const_spec(wo.shape),
        ],
        out_specs=pl.BlockSpec((tm, d), lambda i: (i, 0)),
        out_shape=jax.ShapeDtypeStruct((n, d), F32),
        scratch_shapes=[pltpu.VMEM((HALO, CONV_CH), F32)],
        compiler_params=pltpu.CompilerParams(
            dimension_semantics=("arbitrary",), vmem_limit_bytes=VMEM_LIMIT),
        name="mix",
    )(x2, yb2, gpre, gpost, wcg, cw, wb0, wb1, wo)


def _ffn_kernel(x_ref, gpre_ref, gpost_ref, wgu_ref, wd_ref, o_ref, *, hidden):
    x = x_ref[...]
    h = _rmsnorm(x, gpre_ref[...]).astype(BF16)
    gu = _dot(h, wgu_ref[...])
    g = gu[:, 0:hidden]
    u = gu[:, hidden:2 * hidden]
    a = (g * jax.nn.sigmoid(g) * u).astype(BF16)
    f = _dot(a, wd_ref[...])
    o_ref[...] = x + _rmsnorm(f, gpost_ref[...])


def _ffn_call(x2, gpre, gpost, wgu, wd):
    n, d = x2.shape
    tm = TM_FFN
    hidden = wd.shape[0]
    kern = functools.partial(_ffn_kernel, hidden=hidden)
    return pl.pallas_call(
        kern,
        grid=(n // tm,),
        in_specs=[
            pl.BlockSpec((tm, d), lambda i: (i, 0)),
            _const_spec(gpre.shape), _const_spec(gpost.shape),
            _const_spec(wgu.shape), _const_spec(wd.shape),
        ],
        out_specs=pl.BlockSpec((tm, d), lambda i: (i, 0)),
        out_shape=jax.ShapeDtypeStruct((n, d), F32),
        compiler_params=pltpu.CompilerParams(
            dimension_semantics=("arbitrary",), vmem_limit_bytes=VMEM_LIMIT),
        name="ffn",
    )(x2, gpre, gpost, wgu, wd)


def _selection_matrix():
    n_pairs = N_HEADS // HEADS_PER_STEP
    sel = np.zeros((LANES, n_pairs * LANES), np.float32)
    for head in range(N_HEADS):
        for j in range(N_SPLIT):
            sel[j * N_HEADS + head, (head // HEADS_PER_STEP) * LANES
                + (head % HEADS_PER_STEP) * N_SPLIT + j] = 1.0
    return jnp.asarray(sel, BF16)


def _norm_indicator():
    ind = np.zeros((2 * ATTN_WIDTH, LANES), np.float32)
    for which in range(2):
        for col in range(ATTN_WIDTH):
            ind[which * ATTN_WIDTH + col, which * N_HEADS + col // HEAD_DIM] = 1.0
    return jnp.asarray(ind, BF16)


def kernel(x, norm_mix_pre, norm_mix_post, w_in, b_f, conv_w, w_branch, w_out,
           norm_ffn_pre, norm_ffn_post, w_gate_up, w_down):
    b, s, d = x.shape
    depth = w_in.shape[0]
    a = ATTN_WIDTH
    c0 = 3 * a + N_HEADS
    tri = jnp.asarray(np.tril(np.ones((TM_QKV, TM_QKV), np.float32)), BF16)
    sel = _selection_matrix()
    ind = _norm_indicator()
    scale = 1.0 / math.sqrt(HEAD_DIM)

    for l in range(depth):
        wT = w_in[l].T
        wfqkv = jnp.concatenate(
            [jnp.pad(wT[3 * a:c0], ((0, LANES - N_HEADS), (0, 0))),
             wT[0:a] * scale, wT[a:3 * a]], axis=0).astype(BF16)
        bfp = jnp.pad(b_f[l], (0, LANES - N_HEADS)).reshape(1, LANES)
        wcg = wT[c0:].astype(BF16)

        qT, kx, vT, first = _qkv_call(x, norm_mix_pre[l].reshape(1, d), wfqkv, bfp, tri,
                                      sel, ind)
        yb = _attn_call(first[:, :, 0, 0:N_HEADS].reshape(-1), qT, kx, vT)
        x2 = _mix_call(
            x.reshape(b * s, d), yb.reshape(b * s, a),
            norm_mix_pre[l].reshape(1, d), norm_mix_post[l].reshape(1, d),
            wcg, conv_w[l], w_branch[l, 0].astype(BF16),
            w_branch[l, 1].astype(BF16), w_out[l].astype(BF16), s)
        x2 = _ffn_call(x2, norm_ffn_pre[l].reshape(1, d), norm_ffn_post[l].reshape(1, d),
                       w_gate_up[l].astype(BF16), w_down[l].astype(BF16))
        x = x2.reshape(b, s, d)
    return x
```

```python
import functools
import math

import jax
import jax.numpy as jnp
import numpy as np
from jax import lax
from jax.experimental import pallas as pl
from jax.experimental.pallas import tpu as pltpu

F32 = jnp.float32
BF16 = jnp.bfloat16

RMS_EPS = 1e-6
N_HEADS = 8
HEAD_DIM = 64
ATTN_WIDTH = N_HEADS * HEAD_DIM
CONV_CH = 512
CONV_K = 3
LANES = 128
BF16_ROWS = 16
HEADS_PER_STEP = 2
PAIR_W = 2 * LANES
N_SPLIT = 3
MASKED = -1e30
LOG2E = math.log2(math.e)
PRUNE_GAP = 106.0 * LOG2E
NORM_SLACK = 4.0 * 1.04
VMEM_LIMIT = 60 * 1024 * 1024

TM_QKV = 512
TQ = 1024
TKV = 512
CW = 256
TM_MIX = 512
TM_FFN = 512
SUB_ROWS = 256
HALO = 8

NT_DIMS = (((1,), (1,)), ((), ()))


def _rmsnorm(x, g):
    ms = jnp.mean(x * x, axis=-1, keepdims=True)
    return (x * lax.rsqrt(ms + RMS_EPS)) * g


def _split3(v):
    parts = []
    r = v
    for _ in range(N_SPLIT):
        p = r.astype(BF16)
        parts.append(p)
        r = r - p.astype(F32)
    return parts


def _dot(a, b):
    return jnp.dot(a, b, preferred_element_type=F32)


def _dot_nt(a, b):
    return lax.dot_general(a, b, NT_DIMS, preferred_element_type=F32)


def _const_spec(shape):
    nd = len(shape)
    return pl.BlockSpec(shape, lambda *_: (0,) * nd, pipeline_mode=pl.Buffered(1))


def _qkv_kernel(x_ref, g_ref, w_ref, bf_ref, tri_ref, sel_ref, ind_ref, qT_ref, kx_ref,
                vT_ref, first_ref, carry_ref, kmax_ref, hist_ref, *, tm, tkv):
    i = pl.program_id(1)

    @pl.when(i == 0)
    def _():
        carry_ref[...] = jnp.zeros_like(carry_ref)
        kmax_ref[...] = jnp.zeros_like(kmax_ref)
        hist_ref[...] = jnp.zeros_like(hist_ref)

    a = ATTN_WIDTH
    h = _rmsnorm(x_ref[0], g_ref[...]).astype(BF16)
    fq = _dot_nt(h, w_ref[0:LANES + a, :])
    q = fq[:, LANES:].astype(BF16)
    qT_ref[0] = fq[:, LANES:].T.astype(BF16)

    f = fq[:, 0:LANES] + bf_ref[...]
    logf = (jnp.minimum(f, 0.0) - jnp.log1p(jnp.exp(-jnp.abs(f)))) * LOG2E
    head_lane = lax.broadcasted_iota(jnp.int32, (tm, LANES), 1) < N_HEADS
    k = _dot_nt(h, w_ref[LANES + a:LANES + 2 * a, :]).astype(BF16)
    qf = q.astype(F32)
    kf = k.astype(F32)
    sq = jnp.concatenate([qf * qf, kf * kf], axis=1).astype(BF16)
    norm2 = jnp.max(_dot(sq, ind_ref[...]), axis=0, keepdims=True)

    def pack3(v):
        parts = _split3(jnp.where(head_lane, v, 0.0))
        out = parts[0].astype(F32)
        for j in range(1, N_SPLIT):
            out = out + pltpu.roll(parts[j].astype(F32), j * N_HEADS, 1)
        return out.astype(BF16)

    cs = _dot(tri_ref[...], pack3(logf))
    c_prev = carry_ref[...]
    c = cs + c_prev
    for j in range(1, N_SPLIT):
        c = c + pltpu.roll(cs, LANES - j * N_HEADS, 1)
    c_end = c[tm - 1:tm]
    carry_ref[...] = c_end

    kmax = jnp.maximum(kmax_ref[...], norm2)
    kmax_ref[...] = kmax
    qk2 = NORM_SLACK * norm2 * pltpu.roll(kmax, LANES - N_HEADS, 1)
    gap = hist_ref[...] - c_prev - PRUNE_GAP
    tile = lax.broadcasted_iota(jnp.int32, gap.shape, 0)
    skip = (tile < i) & (gap > 0.0) & (qk2 < gap * gap)
    i_f = i.astype(F32)
    kept = jnp.where(skip, i_f, jnp.minimum(tile.astype(F32), i_f))
    first_ref[0, 0] = jnp.min(kept, axis=0, keepdims=True).astype(jnp.int32)
    hist_ref[pl.ds(i, 1), :] = c_end
    vT = _dot_nt(h, w_ref[LANES + 2 * a:LANES + 3 * a, :]).T
    for cb in range(tm // tkv):
        vT_ref[0, cb] = vT[:, cb * tkv:(cb + 1) * tkv].astype(BF16)
    ext = _dot(pack3(-c), sel_ref[...])
    for p in range(N_HEADS // HEADS_PER_STEP):
        kx_ref[0, :, p * PAIR_W:p * PAIR_W + LANES] = k[:, p * LANES:(p + 1) * LANES]
        kx_ref[0, :, p * PAIR_W + LANES:(p + 1) * PAIR_W] = (
            ext[:, p * LANES:(p + 1) * LANES].astype(BF16))


def _qkv_call(x, g, w, bfp, tri, sel, ind):
    b, s, d = x.shape
    tm, tkv = TM_QKV, TKV
    n_pairs = N_HEADS // HEADS_PER_STEP
    n_tiles = s // tm
    kern = functools.partial(_qkv_kernel, tm=tm, tkv=tkv)
    return pl.pallas_call(
        kern,
        grid=(b, n_tiles),
        in_specs=[
            pl.BlockSpec((1, tm, d), lambda bi, i: (bi, i, 0)),
            _const_spec(g.shape), _const_spec(w.shape), _const_spec(bfp.shape),
            _const_spec(tri.shape), _const_spec(sel.shape), _const_spec(ind.shape),
        ],
        out_specs=[
            pl.BlockSpec((1, ATTN_WIDTH, tm), lambda bi, i: (bi, 0, i)),
            pl.BlockSpec((1, tm, n_pairs * PAIR_W), lambda bi, i: (bi, i, 0)),
            pl.BlockSpec((1, tm // tkv, ATTN_WIDTH, tkv), lambda bi, i: (bi, i, 0, 0)),
            pl.BlockSpec((1, 1, 1, LANES), lambda bi, i: (bi, i, 0, 0)),
        ],
        out_shape=[
            jax.ShapeDtypeStruct((b, ATTN_WIDTH, s), BF16),
            jax.ShapeDtypeStruct((b, s, n_pairs * PAIR_W), BF16),
            jax.ShapeDtypeStruct((b, s // tkv, ATTN_WIDTH, tkv), BF16),
            jax.ShapeDtypeStruct((b, n_tiles, 1, LANES), jnp.int32),
        ],
        scratch_shapes=[
            pltpu.VMEM((1, LANES), F32),
            pltpu.VMEM((1, LANES), F32),
            pltpu.VMEM((n_tiles, LANES), F32),
        ],
        compiler_params=pltpu.CompilerParams(
            dimension_semantics=("arbitrary", "arbitrary"),
            vmem_limit_bytes=VMEM_LIMIT),
        name="qkv",
    )(x, g, w, bfp, tri, sel, ind)


def _attn_kernel(first_ref, qT_ref, kx_ref, vT_ref, o_ref, qx_ref, s0_ref, s1_ref, mx_ref,
                 m_ref, acc_ref, *, tq, tkv, cw, tile, n_tiles):
    bi, pair, qi = pl.program_id(0), pl.program_id(1), pl.program_id(2)
    heads = range(HEADS_PER_STEP)
    n_diag = tq // tkv
    n_full = qi * n_diag
    s_bufs = (s0_ref, s1_ref)

    t0 = n_tiles
    for d in range(tq // tile):
        for h in heads:
            t0 = jnp.minimum(t0, first_ref[(bi * n_tiles + qi * (tq // tile) + d) * N_HEADS
                                           + pair * HEADS_PER_STEP + h])
    pair0 = lax.shift_right_logical(jnp.minimum(t0 * (tile // tkv), n_full), 1)
    j0 = 2 * pair0

    qT = qT_ref[0].astype(F32)
    row = lax.broadcasted_iota(jnp.int32, (LANES, tq), 0)
    for h in heads:
        mine = (row >= h * HEAD_DIM) & (row < (h + 1) * HEAD_DIM)
        ones = (row >= h * N_SPLIT) & (row < (h + 1) * N_SPLIT)
        qx_ref[h, 0:LANES, :] = jnp.where(mine, qT, 0.0).astype(BF16)
        qx_ref[h, LANES:PAIR_W, :] = jnp.where(ones, 1.0, 0.0).astype(BF16)
    m_ref[...] = jnp.full_like(m_ref, MASKED)
    acc_ref[...] = jnp.zeros_like(acc_ref)
    ones_rows = jnp.ones((BF16_ROWS, tkv), BF16)
    krow = lax.broadcasted_iota(jnp.int32, (tkv, cw), 0)
    qcol = lax.broadcasted_iota(jnp.int32, (tkv, cw), 1)

    def units_for(d):
        out = []
        for h in heads:
            for c in range(tq // cw):
                if d is not None and (c + 1) * cw <= d * tkv:
                    continue
                masked = d is not None and c * cw < (d + 1) * tkv - 1
                out.append((h, c, krow + (d * tkv - c * cw) <= qcol if masked else None))
        return out

    def qk_unit(j, buf, unit):
        h, c, mask = unit
        cols = slice(c * cw, (c + 1) * cw)
        start = pl.multiple_of(j * tkv, tkv)
        s = _dot(kx_ref[0, pl.ds(start, tkv), :], qx_ref[h, :, cols])
        if mask is not None:
            s = jnp.where(mask, s, MASKED)
        s_bufs[buf][h, :, cols] = s
        mx_ref[buf, h, :, cols] = jnp.max(s, axis=0, keepdims=True)

    def softmax_unit(buf, unit):
        h, c, _ = unit
        cols = slice(c * cw, (c + 1) * cw)
        m_old = m_ref[h, :, cols]
        m_new = jnp.maximum(m_old, mx_ref[buf, h, :, cols])
        m_ref[h, :, cols] = m_new
        p = jnp.exp2(s_bufs[buf][h, :, cols] - m_new).astype(BF16)
        return p, jnp.exp2(m_old - m_new)

    def pv_unit(j, unit, p, alpha):
        h, c, _ = unit
        cols = slice(c * cw, (c + 1) * cw)
        vx = jnp.concatenate(
            [vT_ref[0, j, h * HEAD_DIM:(h + 1) * HEAD_DIM, :], ones_rows], axis=0)
        acc_ref[h, :, cols] = alpha * acc_ref[h, :, cols] + _dot(vx, p)

    def step(j, buf, cur_d=None, next_d=None, has_next=True):
        cur = units_for(cur_d)
        nxt = units_for(next_d) if has_next else []
        pending = None
        for i in range(max(len(cur), len(nxt))):
            if i < len(nxt):
                qk_unit(j + 1, 1 - buf, nxt[i])
            if pending is not None:
                pv_unit(j, *pending)
                pending = None
            if i < len(cur):
                pending = (cur[i],) + softmax_unit(buf, cur[i])
        if pending is not None:
            pv_unit(j, *pending)

    def full_pair(i, carry):
        step(2 * i, 0)
        step(2 * i + 1, 1)
        return carry

    @pl.when(j0 == n_full)
    def _():
        for unit in units_for(0):
            qk_unit(n_full, 0, unit)

    @pl.when(j0 < n_full)
    def _():
        for unit in units_for(None):
            qk_unit(j0, 0, unit)
        lax.fori_loop(pair0, qi * (n_diag // 2) - 1, full_pair, 0)
        step(n_full - 2, 0)
        step(n_full - 1, 1, next_d=0)

    for d in range(n_diag):
        last = d + 1 == n_diag
        step(n_full + d, d % 2, cur_d=d, next_d=None if last else d + 1,
             has_next=not last)

    outs = []
    for h in heads:
        a = acc_ref[h]
        outs.append(a[0:HEAD_DIM] / a[HEAD_DIM:HEAD_DIM + 1])
    o = jnp.concatenate(outs, axis=0)
    o_ref[0] = o.T.astype(BF16)


def _attn_call(first, qT, kx, vT):
    b, _, s = qT.shape
    tq, tkv = TQ, TKV
    n_pairs = N_HEADS // HEADS_PER_STEP
    kern = functools.partial(_attn_kernel, tq=tq, tkv=tkv, cw=CW, tile=TM_QKV,
                             n_tiles=s // TM_QKV)
    grid_spec = pltpu.PrefetchScalarGridSpec(
        num_scalar_prefetch=1,
        grid=(b, n_pairs, s // tq),
        in_specs=[
            pl.BlockSpec((1, LANES, tq), lambda bi, p, qi, first: (bi, p, qi)),
            pl.BlockSpec((1, s, PAIR_W), lambda bi, p, qi, first: (bi, 0, p)),
            pl.BlockSpec((1, s // tkv, LANES, tkv), lambda bi, p, qi, first: (bi, 0, p, 0)),
        ],
        out_specs=pl.BlockSpec((1, tq, LANES), lambda bi, p, qi, first: (bi, qi, p)),
        scratch_shapes=[
            pltpu.VMEM((HEADS_PER_STEP, PAIR_W, tq), BF16),
            pltpu.VMEM((HEADS_PER_STEP, tkv, tq), F32),
            pltpu.VMEM((HEADS_PER_STEP, tkv, tq), F32),
            pltpu.VMEM((2, HEADS_PER_STEP, 1, tq), F32),
            pltpu.VMEM((HEADS_PER_STEP, 1, tq), F32),
            pltpu.VMEM((HEADS_PER_STEP, HEAD_DIM + BF16_ROWS, tq), F32),
        ],
    )
    return pl.pallas_call(
        kern,
        grid_spec=grid_spec,
        out_shape=jax.ShapeDtypeStruct((b, s, ATTN_WIDTH), BF16),
        compiler_params=pltpu.CompilerParams(
            dimension_semantics=("arbitrary", "arbitrary", "arbitrary"),
            vmem_limit_bytes=VMEM_LIMIT),
        name="attn",
    )(first, qT, kx, vT)


def _mix_kernel(x_ref, yb_ref, gpre_ref, gpost_ref, wcg_ref, cw_ref, wb0_ref, wb1_ref,
                wo_ref, o_ref, ztail_ref, *, tm, sub, tiles_per_seq, d_model):
    parts = [slice(r * sub, (r + 1) * sub) for r in range(tm // sub)]
    xs = [x_ref[rows, :] for rows in parts]
    projs = []
    for x in xs:
        h = _rmsnorm(x, gpre_ref[...]).astype(BF16)
        projs.append(_dot_nt(h, wcg_ref[...]))

    first = pl.program_id(0) % tiles_per_seq == 0
    zh = jnp.where(first, 0.0, ztail_ref[...])
    branches = []
    for rows, proj in zip(parts, projs):
        gate_b = proj[:, 0:CONV_CH]
        z = proj[:, CONV_CH:2 * CONV_CH] * proj[:, 2 * CONV_CH:3 * CONV_CH]
        zz = jnp.concatenate([zh, z], axis=0)
        zh = z[sub - HALO:sub]
        conv = cw_ref[CONV_K - 1:CONV_K, :] * z
        for back in range(1, CONV_K):
            shifted = pltpu.roll(zz, back, 0)[HALO:]
            conv = conv + cw_ref[CONV_K - 1 - back:CONV_K - back, :] * shifted
        ya = (gate_b * conv).astype(BF16)
        branches.append((_dot(ya, wb0_ref[...]), _dot(yb_ref[rows, :], wb1_ref[...])))
    ztail_ref[...] = zh

    g0 = 3 * CONV_CH
    ms = []
    for proj, (ya_d, yb_d) in zip(projs, branches):
        gc = jax.nn.sigmoid(proj[:, g0:g0 + d_model])
        ga = jax.nn.sigmoid(proj[:, g0 + d_model:g0 + 2 * d_model])
        merged = (gc * ya_d + ga * yb_d).astype(BF16)
        ms.append(_dot(merged, wo_ref[...]))
    for rows, x, m in zip(parts, xs, ms):
        o_ref[rows, :] = x + _rmsnorm(m, gpost_ref[...])


def _mix_call(x2, yb2, gpre, gpost, wcg, cw, wb0, wb1, wo, seq_len):
    n, d = x2.shape
    tm = TM_MIX
    kern = functools.partial(_mix_kernel, tm=tm, sub=SUB_ROWS, tiles_per_seq=seq_len // tm,
                             d_model=d)
    return pl.pallas_call(
        kern,
        grid=(n // tm,),
        in_specs=[
            pl.BlockSpec((tm, d), lambda i: (i, 0)),
            pl.BlockSpec((tm, ATTN_WIDTH), lambda i: (i, 0)),
            _const_spec(gpre.shape), _const_spec(gpost.shape), _const_spec(wcg.shape),
            _const_spec(cw.shape), _const_spec(wb0.shape), _const_spec(wb1.shape),
            _const_spec(wo.shape),
        ],
        out_specs=pl.BlockSpec((tm, d), lambda i: (i, 0)),
        out_shape=jax.ShapeDtypeStruct((n, d), F32),
        scratch_shapes=[pltpu.VMEM((HALO, CONV_CH), F32)],
        compiler_params=pltpu.CompilerParams(
            dimension_semantics=("arbitrary",), vmem_limit_bytes=VMEM_LIMIT),
        name="mix",
    )(x2, yb2, gpre, gpost, wcg, cw, wb0, wb1, wo)


def _ffn_kernel(x_ref, gpre_ref, gpost_ref, wgu_ref, wd_ref, o_ref, *, hidden, sub):
    parts = [slice(r * sub, (r + 1) * sub) for r in range(x_ref.shape[0] // sub)]
    xs = [x_ref[rows, :] for rows in parts]
    gus = []
    for x in xs:
        h = _rmsnorm(x, gpre_ref[...]).astype(BF16)
        gus.append(_dot(h, wgu_ref[...]))
    fs = []
    for gu in gus:
        g = gu[:, 0:hidden]
        u = gu[:, hidden:2 * hidden]
        a = (g * jax.nn.sigmoid(g) * u).astype(BF16)
        fs.append(_dot(a, wd_ref[...]))
    for rows, x, f in zip(parts, xs, fs):
        o_ref[rows, :] = x + _rmsnorm(f, gpost_ref[...])


def _ffn_call(x2, gpre, gpost, wgu, wd):
    n, d = x2.shape
    tm = TM_FFN
    hidden = wd.shape[0]
    kern = functools.partial(_ffn_kernel, hidden=hidden, sub=SUB_ROWS)
    return pl.pallas_call(
        kern,
        grid=(n // tm,),
        in_specs=[
            pl.BlockSpec((tm, d), lambda i: (i, 0)),
            _const_spec(gpre.shape), _const_spec(gpost.shape),
            _const_spec(wgu.shape), _const_spec(wd.shape),
        ],
        out_specs=pl.BlockSpec((tm, d), lambda i: (i, 0)),
        out_shape=jax.ShapeDtypeStruct((n, d), F32),
        compiler_params=pltpu.CompilerParams(
            dimension_semantics=("arbitrary",), vmem_limit_bytes=VMEM_LIMIT),
        name="ffn",
    )(x2, gpre, gpost, wgu, wd)


def _selection_matrix():
    n_pairs = N_HEADS // HEADS_PER_STEP
    sel = np.zeros((LANES, n_pairs * LANES), np.float32)
    for head in range(N_HEADS):
        for j in range(N_SPLIT):
            sel[j * N_HEADS + head, (head // HEADS_PER_STEP) * LANES
                + (head % HEADS_PER_STEP) * N_SPLIT + j] = 1.0
    return jnp.asarray(sel, BF16)


def _norm_indicator():
    ind = np.zeros((2 * ATTN_WIDTH, LANES), np.float32)
    for which in range(2):
        for col in range(ATTN_WIDTH):
            ind[which * ATTN_WIDTH + col, which * N_HEADS + col // HEAD_DIM] = 1.0
    return jnp.asarray(ind, BF16)


def kernel(x, norm_mix_pre, norm_mix_post, w_in, b_f, conv_w, w_branch, w_out,
           norm_ffn_pre, norm_ffn_post, w_gate_up, w_down):
    b, s, d = x.shape
    depth = w_in.shape[0]
    a = ATTN_WIDTH
    c0 = 3 * a + N_HEADS
    tri = jnp.asarray(np.tril(np.ones((TM_QKV, TM_QKV), np.float32)), BF16)
    sel = _selection_matrix()
    ind = _norm_indicator()
    scale = LOG2E / math.sqrt(HEAD_DIM)

    for l in range(depth):
        wT = w_in[l].T
        wfqkv = jnp.concatenate(
            [jnp.pad(wT[3 * a:c0], ((0, LANES - N_HEADS), (0, 0))),
             wT[0:a] * scale, wT[a:3 * a]], axis=0).astype(BF16)
        bfp = jnp.pad(b_f[l], (0, LANES - N_HEADS)).reshape(1, LANES)
        wcg = wT[c0:].astype(BF16)

        qT, kx, vT, first = _qkv_call(x, norm_mix_pre[l].reshape(1, d), wfqkv, bfp, tri,
                                      sel, ind)
        yb = _attn_call(first[:, :, 0, 0:N_HEADS].reshape(-1), qT, kx, vT)
        x2 = _mix_call(
            x.reshape(b * s, d), yb.reshape(b * s, a),
            norm_mix_pre[l].reshape(1, d), norm_mix_post[l].reshape(1, d),
            wcg, conv_w[l], w_branch[l, 0].astype(BF16),
            w_branch[l, 1].astype(BF16), w_out[l].astype(BF16), s)
        x2 = _ffn_call(x2, norm_ffn_pre[l].reshape(1, d), norm_ffn_post[l].reshape(1, d),
                       w_gate_up[l].astype(BF16), w_down[l].astype(BF16))
        x = x2.reshape(b, s, d)
    return x
```

```python
import functools
import math

import jax
import jax.numpy as jnp
import numpy as np
from jax import lax
from jax.experimental import pallas as pl
from jax.experimental.pallas import tpu as pltpu

F32 = jnp.float32
BF16 = jnp.bfloat16

RMS_EPS = 1e-6
N_HEADS = 8
HEAD_DIM = 64
ATTN_WIDTH = N_HEADS * HEAD_DIM
CONV_CH = 512
CONV_K = 3
LANES = 128
BF16_ROWS = 16
HEADS_PER_STEP = 2
PAIR_W = 2 * LANES
N_SPLIT = 3
MASKED = -1e30
LOG2E = math.log2(math.e)
PRUNE_GAP = 106.0 * LOG2E
NORM_SLACK = 4.0 * 1.04
VMEM_LIMIT = 60 * 1024 * 1024

TM_QKV = 512
TQ = 1024
TKV = 512
CW = 256
TM_MIX = 1024
TM_FFN = 1024
SUB_ROWS = 256
HALO = 8

NT_DIMS = (((1,), (1,)), ((), ()))


def _rmsnorm(x, g):
    ms = jnp.mean(x * x, axis=-1, keepdims=True)
    return (x * lax.rsqrt(ms + RMS_EPS)) * g


def _split3(v):
    parts = []
    r = v
    for _ in range(N_SPLIT):
        p = r.astype(BF16)
        parts.append(p)
        r = r - p.astype(F32)
    return parts


def _dot(a, b):
    return jnp.dot(a, b, preferred_element_type=F32)


def _dot_nt(a, b):
    return lax.dot_general(a, b, NT_DIMS, preferred_element_type=F32)


def _const_spec(shape):
    nd = len(shape)
    return pl.BlockSpec(shape, lambda *_: (0,) * nd, pipeline_mode=pl.Buffered(1))


def _qkv_kernel(x_ref, g_ref, w_ref, bf_ref, tri_ref, sel_ref, ind_ref, qT_ref, kx_ref,
                vT_ref, first_ref, carry_ref, kmax_ref, hist_ref, *, tm, tkv):
    i = pl.program_id(1)

    @pl.when(i == 0)
    def _():
        carry_ref[...] = jnp.zeros_like(carry_ref)
        kmax_ref[...] = jnp.zeros_like(kmax_ref)
        hist_ref[...] = jnp.zeros_like(hist_ref)

    a = ATTN_WIDTH
    h = _rmsnorm(x_ref[0], g_ref[...]).astype(BF16)
    fq = _dot_nt(h, w_ref[0:LANES + a, :])
    q = fq[:, LANES:].astype(BF16)
    qT_ref[0] = fq[:, LANES:].T.astype(BF16)

    f = fq[:, 0:LANES] + bf_ref[...]
    logf = (jnp.minimum(f, 0.0) - jnp.log1p(jnp.exp(-jnp.abs(f)))) * LOG2E
    head_lane = lax.broadcasted_iota(jnp.int32, (tm, LANES), 1) < N_HEADS
    k = _dot_nt(h, w_ref[LANES + a:LANES + 2 * a, :]).astype(BF16)
    qf = q.astype(F32)
    kf = k.astype(F32)
    sq = jnp.concatenate([qf * qf, kf * kf], axis=1).astype(BF16)
    norm2 = jnp.max(_dot(sq, ind_ref[...]), axis=0, keepdims=True)

    def pack3(v):
        parts = _split3(jnp.where(head_lane, v, 0.0))
        out = parts[0].astype(F32)
        for j in range(1, N_SPLIT):
            out = out + pltpu.roll(parts[j].astype(F32), j * N_HEADS, 1)
        return out.astype(BF16)

    cs = _dot(tri_ref[...], pack3(logf))
    c_prev = carry_ref[...]
    c = cs + c_prev
    for j in range(1, N_SPLIT):
        c = c + pltpu.roll(cs, LANES - j * N_HEADS, 1)
    c_end = c[tm - 1:tm]
    carry_ref[...] = c_end

    kmax = jnp.maximum(kmax_ref[...], norm2)
    kmax_ref[...] = kmax
    qk2 = NORM_SLACK * norm2 * pltpu.roll(kmax, LANES - N_HEADS, 1)
    gap = hist_ref[...] - c_prev - PRUNE_GAP
    tile = lax.broadcasted_iota(jnp.int32, gap.shape, 0)
    skip = (tile < i) & (gap > 0.0) & (qk2 < gap * gap)
    i_f = i.astype(F32)
    kept = jnp.where(skip, i_f, jnp.minimum(tile.astype(F32), i_f))
    first_ref[0, 0] = jnp.min(kept, axis=0, keepdims=True).astype(jnp.int32)
    hist_ref[pl.ds(i, 1), :] = c_end
    vT = _dot_nt(h, w_ref[LANES + 2 * a:LANES + 3 * a, :]).T
    for cb in range(tm // tkv):
        vT_ref[0, cb] = vT[:, cb * tkv:(cb + 1) * tkv].astype(BF16)
    ext = _dot(pack3(-c), sel_ref[...])
    for p in range(N_HEADS // HEADS_PER_STEP):
        kx_ref[0, :, p * PAIR_W:p * PAIR_W + LANES] = k[:, p * LANES:(p + 1) * LANES]
        kx_ref[0, :, p * PAIR_W + LANES:(p + 1) * PAIR_W] = (
            ext[:, p * LANES:(p + 1) * LANES].astype(BF16))


def _qkv_call(x, g, w, bfp, tri, sel, ind):
    b, s, d = x.shape
    tm, tkv = TM_QKV, TKV
    n_pairs = N_HEADS // HEADS_PER_STEP
    n_tiles = s // tm
    kern = functools.partial(_qkv_kernel, tm=tm, tkv=tkv)
    return pl.pallas_call(
        kern,
        grid=(b, n_tiles),
        in_specs=[
            pl.BlockSpec((1, tm, d), lambda bi, i: (bi, i, 0)),
            _const_spec(g.shape), _const_spec(w.shape), _const_spec(bfp.shape),
            _const_spec(tri.shape), _const_spec(sel.shape), _const_spec(ind.shape),
        ],
        out_specs=[
            pl.BlockSpec((1, ATTN_WIDTH, tm), lambda bi, i: (bi, 0, i)),
            pl.BlockSpec((1, tm, n_pairs * PAIR_W), lambda bi, i: (bi, i, 0)),
            pl.BlockSpec((1, tm // tkv, ATTN_WIDTH, tkv), lambda bi, i: (bi, i, 0, 0)),
            pl.BlockSpec((1, 1, 1, LANES), lambda bi, i: (bi, i, 0, 0)),
        ],
        out_shape=[
            jax.ShapeDtypeStruct((b, ATTN_WIDTH, s), BF16),
            jax.ShapeDtypeStruct((b, s, n_pairs * PAIR_W), BF16),
            jax.ShapeDtypeStruct((b, s // tkv, ATTN_WIDTH, tkv), BF16),
            jax.ShapeDtypeStruct((b, n_tiles, 1, LANES), jnp.int32),
        ],
        scratch_shapes=[
            pltpu.VMEM((1, LANES), F32),
            pltpu.VMEM((1, LANES), F32),
            pltpu.VMEM((n_tiles, LANES), F32),
        ],
        compiler_params=pltpu.CompilerParams(
            dimension_semantics=("arbitrary", "arbitrary"),
            vmem_limit_bytes=VMEM_LIMIT),
        name="qkv",
    )(x, g, w, bfp, tri, sel, ind)


def _attn_kernel(first_ref, qT_ref, kx_ref, vT_ref, o_ref, qx_ref, s0_ref, s1_ref, mx_ref,
                 m_ref, acc_ref, *, tq, tkv, cw, tile, n_tiles):
    bi, pair, qi = pl.program_id(0), pl.program_id(1), pl.program_id(2)
    heads = range(HEADS_PER_STEP)
    n_diag = tq // tkv
    n_full = qi * n_diag
    s_bufs = (s0_ref, s1_ref)

    t0 = n_tiles
    for d in range(tq // tile):
        for h in heads:
            t0 = jnp.minimum(t0, first_ref[(bi * n_tiles + qi * (tq // tile) + d) * N_HEADS
                                           + pair * HEADS_PER_STEP + h])
    pair0 = lax.shift_right_logical(jnp.minimum(t0 * (tile // tkv), n_full), 1)
    j0 = 2 * pair0

    qT = qT_ref[0].astype(F32)
    row = lax.broadcasted_iota(jnp.int32, (LANES, tq), 0)
    for h in heads:
        mine = (row >= h * HEAD_DIM) & (row < (h + 1) * HEAD_DIM)
        ones = (row >= h * N_SPLIT) & (row < (h + 1) * N_SPLIT)
        qx_ref[h, 0:LANES, :] = jnp.where(mine, qT, 0.0).astype(BF16)
        qx_ref[h, LANES:PAIR_W, :] = jnp.where(ones, 1.0, 0.0).astype(BF16)
    m_ref[...] = jnp.full_like(m_ref, MASKED)
    acc_ref[...] = jnp.zeros_like(acc_ref)
    ones_rows = jnp.ones((BF16_ROWS, tkv), BF16)
    krow = lax.broadcasted_iota(jnp.int32, (tkv, cw), 0)
    qcol = lax.broadcasted_iota(jnp.int32, (tkv, cw), 1)

    def units_for(d):
        out = []
        for h in heads:
            for c in range(tq // cw):
                if d is not None and (c + 1) * cw <= d * tkv:
                    continue
                masked = d is not None and c * cw < (d + 1) * tkv - 1
                out.append((h, c, krow + (d * tkv - c * cw) <= qcol if masked else None))
        return out

    def qk_unit(j, buf, unit):
        h, c, mask = unit
        cols = slice(c * cw, (c + 1) * cw)
        start = pl.multiple_of(j * tkv, tkv)
        s = _dot(kx_ref[0, pl.ds(start, tkv), :], qx_ref[h, :, cols])
        if mask is not None:
            s = jnp.where(mask, s, MASKED)
        s_bufs[buf][h, :, cols] = s
        mx_ref[buf, h, :, cols] = jnp.max(s, axis=0, keepdims=True)

    def softmax_unit(buf, unit):
        h, c, _ = unit
        cols = slice(c * cw, (c + 1) * cw)
        m_old = m_ref[h, :, cols]
        m_new = jnp.maximum(m_old, mx_ref[buf, h, :, cols])
        m_ref[h, :, cols] = m_new
        p = jnp.exp2(s_bufs[buf][h, :, cols] - m_new).astype(BF16)
        return p, jnp.exp2(m_old - m_new)

    def pv_unit(j, unit, p, alpha):
        h, c, _ = unit
        cols = slice(c * cw, (c + 1) * cw)
        vx = jnp.concatenate(
            [vT_ref[0, j, h * HEAD_DIM:(h + 1) * HEAD_DIM, :], ones_rows], axis=0)
        acc_ref[h, :, cols] = alpha * acc_ref[h, :, cols] + _dot(vx, p)

    def step(j, buf, cur_d=None, next_d=None, has_next=True):
        cur = units_for(cur_d)
        nxt = units_for(next_d) if has_next else []
        pending = None
        for i in range(max(len(cur), len(nxt))):
            if i < len(nxt):
                qk_unit(j + 1, 1 - buf, nxt[i])
            if pending is not None:
                pv_unit(j, *pending)
                pending = None
            if i < len(cur):
                pending = (cur[i],) + softmax_unit(buf, cur[i])
        if pending is not None:
            pv_unit(j, *pending)

    def full_pair(i, carry):
        step(2 * i, 0)
        step(2 * i + 1, 1)
        return carry

    @pl.when(j0 == n_full)
    def _():
        for unit in units_for(0):
            qk_unit(n_full, 0, unit)

    @pl.when(j0 < n_full)
    def _():
        for unit in units_for(None):
            qk_unit(j0, 0, unit)
        lax.fori_loop(pair0, qi * (n_diag // 2) - 1, full_pair, 0)
        step(n_full - 2, 0)
        step(n_full - 1, 1, next_d=0)

    for d in range(n_diag):
        last = d + 1 == n_diag
        step(n_full + d, d % 2, cur_d=d, next_d=None if last else d + 1,
             has_next=not last)

    outs = []
    for h in heads:
        a = acc_ref[h]
        outs.append(a[0:HEAD_DIM] / a[HEAD_DIM:HEAD_DIM + 1])
    o = jnp.concatenate(outs, axis=0)
    o_ref[0] = o.T.astype(BF16)


def _attn_call(first, qT, kx, vT):
    b, _, s = qT.shape
    tq, tkv = TQ, TKV
    n_pairs = N_HEADS // HEADS_PER_STEP
    kern = functools.partial(_attn_kernel, tq=tq, tkv=tkv, cw=CW, tile=TM_QKV,
                             n_tiles=s // TM_QKV)
    grid_spec = pltpu.PrefetchScalarGridSpec(
        num_scalar_prefetch=1,
        grid=(b, n_pairs, s // tq),
        in_specs=[
            pl.BlockSpec((1, LANES, tq), lambda bi, p, qi, first: (bi, p, qi)),
            pl.BlockSpec((1, s, PAIR_W), lambda bi, p, qi, first: (bi, 0, p)),
            pl.BlockSpec((1, s // tkv, LANES, tkv), lambda bi, p, qi, first: (bi, 0, p, 0)),
        ],
        out_specs=pl.BlockSpec((1, tq, LANES), lambda bi, p, qi, first: (bi, qi, p)),
        scratch_shapes=[
            pltpu.VMEM((HEADS_PER_STEP, PAIR_W, tq), BF16),
            pltpu.VMEM((HEADS_PER_STEP, tkv, tq), F32),
            pltpu.VMEM((HEADS_PER_STEP, tkv, tq), F32),
            pltpu.VMEM((2, HEADS_PER_STEP, 1, tq), F32),
            pltpu.VMEM((HEADS_PER_STEP, 1, tq), F32),
            pltpu.VMEM((HEADS_PER_STEP, HEAD_DIM + BF16_ROWS, tq), F32),
        ],
    )
    return pl.pallas_call(
        kern,
        grid_spec=grid_spec,
        out_shape=jax.ShapeDtypeStruct((b, s, ATTN_WIDTH), BF16),
        compiler_params=pltpu.CompilerParams(
            dimension_semantics=("arbitrary", "arbitrary", "arbitrary"),
            vmem_limit_bytes=VMEM_LIMIT),
        name="attn",
    )(first, qT, kx, vT)


def _mix_kernel(x_ref, yb_ref, gpre_ref, gpost_ref, wcg_ref, cw_ref, wb0_ref, wb1_ref,
                wo_ref, o_ref, ztail_ref, *, tm, sub, tiles_per_seq, d_model):
    parts = [slice(r * sub, (r + 1) * sub) for r in range(tm // sub)]
    xs = [x_ref[rows, :] for rows in parts]
    projs = []
    for x in xs:
        h = _rmsnorm(x, gpre_ref[...]).astype(BF16)
        projs.append(_dot_nt(h, wcg_ref[...]))

    first = pl.program_id(0) % tiles_per_seq == 0
    zh = jnp.where(first, 0.0, ztail_ref[...])
    branches = []
    for rows, proj in zip(parts, projs):
        gate_b = proj[:, 0:CONV_CH]
        z = proj[:, CONV_CH:2 * CONV_CH] * proj[:, 2 * CONV_CH:3 * CONV_CH]
        zz = jnp.concatenate([zh, z], axis=0)
        zh = z[sub - HALO:sub]
        conv = cw_ref[CONV_K - 1:CONV_K, :] * z
        for back in range(1, CONV_K):
            shifted = pltpu.roll(zz, back, 0)[HALO:]
            conv = conv + cw_ref[CONV_K - 1 - back:CONV_K - back, :] * shifted
        ya = (gate_b * conv).astype(BF16)
        branches.append((_dot(ya, wb0_ref[...]), _dot(yb_ref[rows, :], wb1_ref[...])))
    ztail_ref[...] = zh

    g0 = 3 * CONV_CH
    ms = []
    for proj, (ya_d, yb_d) in zip(projs, branches):
        gc = jax.nn.sigmoid(proj[:, g0:g0 + d_model])
        ga = jax.nn.sigmoid(proj[:, g0 + d_model:g0 + 2 * d_model])
        merged = (gc * ya_d + ga * yb_d).astype(BF16)
        ms.append(_dot(merged, wo_ref[...]))
    for rows, x, m in zip(parts, xs, ms):
        o_ref[rows, :] = x + _rmsnorm(m, gpost_ref[...])


def _mix_call(x2, yb2, gpre, gpost, wcg, cw, wb0, wb1, wo, seq_len):
    n, d = x2.shape
    tm = TM_MIX
    kern = functools.partial(_mix_kernel, tm=tm, sub=SUB_ROWS, tiles_per_seq=seq_len // tm,
                             d_model=d)
    return pl.pallas_call(
        kern,
        grid=(n // tm,),
        in_specs=[
            pl.BlockSpec((tm, d), lambda i: (i, 0)),
            pl.BlockSpec((tm, ATTN_WIDTH), lambda i: (i, 0)),
            _const_spec(gpre.shape), _const_spec(gpost.shape), _const_spec(wcg.shape),
            _const_spec(cw.shape), _const_spec(wb0.shape), _const_spec(wb1.shape),
            _const_spec(wo.shape),
        ],
        out_specs=pl.BlockSpec((tm, d), lambda i: (i, 0)),
        out_shape=jax.ShapeDtypeStruct((n, d), F32),
        scratch_shapes=[pltpu.VMEM((HALO, CONV_CH), F32)],
        compiler_params=pltpu.CompilerParams(
            dimension_semantics=("arbitrary",), vmem_limit_bytes=VMEM_LIMIT),
        name="mix",
    )(x2, yb2, gpre, gpost, wcg, cw, wb0, wb1, wo)


def _ffn_kernel(x_ref, gpre_ref, gpost_ref, wgu_ref, wd_ref, o_ref, *, hidden, sub):
    parts = [slice(r * sub, (r + 1) * sub) for r in range(x_ref.shape[0] // sub)]
    xs = [x_ref[rows, :] for rows in parts]
    gus = []
    for x in xs:
        h = _rmsnorm(x, gpre_ref[...]).astype(BF16)
        gus.append(_dot(h, wgu_ref[...]))
    fs = []
    for gu in gus:
        g = gu[:, 0:hidden]
        u = gu[:, hidden:2 * hidden]
        a = (g * jax.nn.sigmoid(g) * u).astype(BF16)
        fs.append(_dot(a, wd_ref[...]))
    for rows, x, f in zip(parts, xs, fs):
        o_ref[rows, :] = x + _rmsnorm(f, gpost_ref[...])


def _ffn_call(x2, gpre, gpost, wgu, wd):
    n, d = x2.shape
    tm = TM_FFN
    hidden = wd.shape[0]
    kern = functools.partial(_ffn_kernel, hidden=hidden, sub=SUB_ROWS)
    return pl.pallas_call(
        kern,
        grid=(n // tm,),
        in_specs=[
            pl.BlockSpec((tm, d), lambda i: (i, 0)),
            _const_spec(gpre.shape), _const_spec(gpost.shape),
            _const_spec(wgu.shape), _const_spec(wd.shape),
        ],
        out_specs=pl.BlockSpec((tm, d), lambda i: (i, 0)),
        out_shape=jax.ShapeDtypeStruct((n, d), F32),
        compiler_params=pltpu.CompilerParams(
            dimension_semantics=("arbitrary",), vmem_limit_bytes=VMEM_LIMIT),
        name="ffn",
    )(x2, gpre, gpost, wgu, wd)


def _selection_matrix():
    n_pairs = N_HEADS // HEADS_PER_STEP
    sel = np.zeros((LANES, n_pairs * LANES), np.float32)
    for head in range(N_HEADS):
        for j in range(N_SPLIT):
            sel[j * N_HEADS + head, (head // HEADS_PER_STEP) * LANES
                + (head % HEADS_PER_STEP) * N_SPLIT + j] = 1.0
    return jnp.asarray(sel, BF16)


def _norm_indicator():
    ind = np.zeros((2 * ATTN_WIDTH, LANES), np.float32)
    for which in range(2):
        for col in range(ATTN_WIDTH):
            ind[which * ATTN_WIDTH + col, which * N_HEADS + col // HEAD_DIM] = 1.0
    return jnp.asarray(ind, BF16)


def kernel(x, norm_mix_pre, norm_mix_post, w_in, b_f, conv_w, w_branch, w_out,
           norm_ffn_pre, norm_ffn_post, w_gate_up, w_down):
    b, s, d = x.shape
    depth = w_in.shape[0]
    a = ATTN_WIDTH
    c0 = 3 * a + N_HEADS
    tri = jnp.asarray(np.tril(np.ones((TM_QKV, TM_QKV), np.float32)), BF16)
    sel = _selection_matrix()
    ind = _norm_indicator()
    scale = LOG2E / math.sqrt(HEAD_DIM)

    for l in range(depth):
        wT = w_in[l].T
        wfqkv = jnp.concatenate(
            [jnp.pad(wT[3 * a:c0], ((0, LANES - N_HEADS), (0, 0))),
             wT[0:a] * scale, wT[a:3 * a]], axis=0).astype(BF16)
        bfp = jnp.pad(b_f[l], (0, LANES - N_HEADS)).reshape(1, LANES)
        wcg = wT[c0:].astype(BF16)

        qT, kx, vT, first = _qkv_call(x, norm_mix_pre[l].reshape(1, d), wfqkv, bfp, tri,
                                      sel, ind)
        yb = _attn_call(first[:, :, 0, 0:N_HEADS].reshape(-1), qT, kx, vT)
        x2 = _mix_call(
            x.reshape(b * s, d), yb.reshape(b * s, a),
            norm_mix_pre[l].reshape(1, d), norm_mix_post[l].reshape(1, d),
            wcg, conv_w[l], w_branch[l, 0].astype(BF16),
            w_branch[l, 1].astype(BF16), w_out[l].astype(BF16), s)
        x2 = _ffn_call(x2, norm_ffn_pre[l].reshape(1, d), norm_ffn_post[l].reshape(1, d),
                       w_gate_up[l].astype(BF16), w_down[l].astype(BF16))
        x = x2.reshape(b, s, d)
    return x
```

```python
import functools
import math

import jax
import jax.numpy as jnp
import numpy as np
from jax import lax
from jax.experimental import pallas as pl
from jax.experimental.pallas import tpu as pltpu

F32 = jnp.float32
BF16 = jnp.bfloat16

RMS_EPS = 1e-6
N_HEADS = 8
HEAD_DIM = 64
ATTN_WIDTH = N_HEADS * HEAD_DIM
CONV_CH = 512
CONV_K = 3
LANES = 128
BF16_ROWS = 16
HEADS_PER_STEP = 2
PAIR_W = 2 * LANES
N_SPLIT = 3
MASKED = -1e30
LOG2E = math.log2(math.e)
PRUNE_GAP = 106.0 * LOG2E
NORM_SLACK = 4.0 * 1.04
VMEM_LIMIT = 60 * 1024 * 1024

TM_QKV = 512
TQ = 1024
TKV = 512
CW = 256
TM_MIX = 1024
TM_FFN = 1024
SUB_ROWS = 256
HALO = 8

NT_DIMS = (((1,), (1,)), ((), ()))


def _rmsnorm(x, g):
    ms = jnp.mean(x * x, axis=-1, keepdims=True)
    return (x * lax.rsqrt(ms + RMS_EPS)) * g


def _split3(v):
    parts = []
    r = v
    for _ in range(N_SPLIT):
        p = r.astype(BF16)
        parts.append(p)
        r = r - p.astype(F32)
    return parts


def _dot(a, b):
    return jnp.dot(a, b, preferred_element_type=F32)


def _dot_nt(a, b):
    return lax.dot_general(a, b, NT_DIMS, preferred_element_type=F32)


def _const_spec(shape):
    nd = len(shape)
    return pl.BlockSpec(shape, lambda *_: (0,) * nd, pipeline_mode=pl.Buffered(1))


def _qkv_kernel(*refs, tm, tkv, n_cast):
    (x_ref, g_ref, w_ref, bf_ref, tri_ref, sel_ref, ind_ref), refs = refs[:7], refs[7:]
    cast_in, refs = refs[:n_cast], refs[n_cast:]
    (qT_ref, kx_ref, vT_ref, first_ref), refs = refs[:4], refs[4:]
    cast_out, (carry_ref, kmax_ref, hist_ref) = refs[:n_cast], refs[n_cast:]
    i = pl.program_id(1)

    for src, dst in zip(cast_in, cast_out):
        dst[...] = src[...].astype(BF16)

    @pl.when(i == 0)
    def _():
        carry_ref[...] = jnp.zeros_like(carry_ref)
        kmax_ref[...] = jnp.zeros_like(kmax_ref)
        hist_ref[...] = jnp.zeros_like(hist_ref)

    a = ATTN_WIDTH
    h = _rmsnorm(x_ref[0], g_ref[...]).astype(BF16)
    fq = _dot_nt(h, w_ref[0:LANES + a, :])
    q = fq[:, LANES:].astype(BF16)
    qT_ref[0] = fq[:, LANES:].T.astype(BF16)

    f = fq[:, 0:LANES] + bf_ref[...]
    logf = (jnp.minimum(f, 0.0) - jnp.log1p(jnp.exp(-jnp.abs(f)))) * LOG2E
    head_lane = lax.broadcasted_iota(jnp.int32, (tm, LANES), 1) < N_HEADS
    k = _dot_nt(h, w_ref[LANES + a:LANES + 2 * a, :]).astype(BF16)
    qf = q.astype(F32)
    kf = k.astype(F32)
    sq = jnp.concatenate([qf * qf, kf * kf], axis=1).astype(BF16)
    norm2 = jnp.max(_dot(sq, ind_ref[...]), axis=0, keepdims=True)

    def pack3(v):
        parts = _split3(jnp.where(head_lane, v, 0.0))
        out = parts[0].astype(F32)
        for j in range(1, N_SPLIT):
            out = out + pltpu.roll(parts[j].astype(F32), j * N_HEADS, 1)
        return out.astype(BF16)

    cs = _dot(tri_ref[...], pack3(logf))
    c_prev = carry_ref[...]
    c = cs + c_prev
    for j in range(1, N_SPLIT):
        c = c + pltpu.roll(cs, LANES - j * N_HEADS, 1)
    c_end = c[tm - 1:tm]
    carry_ref[...] = c_end

    kmax = jnp.maximum(kmax_ref[...], norm2)
    kmax_ref[...] = kmax
    qk2 = NORM_SLACK * norm2 * pltpu.roll(kmax, LANES - N_HEADS, 1)
    gap = hist_ref[...] - c_prev - PRUNE_GAP
    tile = lax.broadcasted_iota(jnp.int32, gap.shape, 0)
    skip = (tile < i) & (gap > 0.0) & (qk2 < gap * gap)
    i_f = i.astype(F32)
    kept = jnp.where(skip, i_f, jnp.minimum(tile.astype(F32), i_f))
    first_ref[0, 0] = jnp.min(kept, axis=0, keepdims=True).astype(jnp.int32)
    hist_ref[pl.ds(i, 1), :] = c_end
    vT = _dot_nt(h, w_ref[LANES + 2 * a:LANES + 3 * a, :]).T
    for cb in range(tm // tkv):
        vT_ref[0, cb] = vT[:, cb * tkv:(cb + 1) * tkv].astype(BF16)
    ext = _dot(pack3(-c), sel_ref[...])
    for p in range(N_HEADS // HEADS_PER_STEP):
        kx_ref[0, :, p * PAIR_W:p * PAIR_W + LANES] = k[:, p * LANES:(p + 1) * LANES]
        kx_ref[0, :, p * PAIR_W + LANES:(p + 1) * PAIR_W] = (
            ext[:, p * LANES:(p + 1) * LANES].astype(BF16))


def _cast_rider_spec(shape, steps, n_tiles):
    rows, cols = shape
    share = 1
    while (rows * share) % (steps * BF16_ROWS) or steps % share:
        share += 1
    return pl.BlockSpec((rows * share // steps, cols),
                        lambda bi, i: ((bi * n_tiles + i) // share, 0))


def _qkv_call(x, g, w, bfp, tri, sel, ind, casts):
    b, s, d = x.shape
    tm, tkv = TM_QKV, TKV
    n_pairs = N_HEADS // HEADS_PER_STEP
    n_tiles = s // tm
    kern = functools.partial(_qkv_kernel, tm=tm, tkv=tkv, n_cast=len(casts))
    rider_specs = [_cast_rider_spec(c.shape, b * n_tiles, n_tiles) for c in casts]
    return pl.pallas_call(
        kern,
        grid=(b, n_tiles),
        in_specs=[
            pl.BlockSpec((1, tm, d), lambda bi, i: (bi, i, 0)),
            _const_spec(g.shape), _const_spec(w.shape), _const_spec(bfp.shape),
            _const_spec(tri.shape), _const_spec(sel.shape), _const_spec(ind.shape),
        ] + rider_specs,
        out_specs=[
            pl.BlockSpec((1, ATTN_WIDTH, tm), lambda bi, i: (bi, 0, i)),
            pl.BlockSpec((1, tm, n_pairs * PAIR_W), lambda bi, i: (bi, i, 0)),
            pl.BlockSpec((1, tm // tkv, ATTN_WIDTH, tkv), lambda bi, i: (bi, i, 0, 0)),
            pl.BlockSpec((1, 1, 1, LANES), lambda bi, i: (bi, i, 0, 0)),
        ] + rider_specs,
        out_shape=[
            jax.ShapeDtypeStruct((b, ATTN_WIDTH, s), BF16),
            jax.ShapeDtypeStruct((b, s, n_pairs * PAIR_W), BF16),
            jax.ShapeDtypeStruct((b, s // tkv, ATTN_WIDTH, tkv), BF16),
            jax.ShapeDtypeStruct((b, n_tiles, 1, LANES), jnp.int32),
        ] + [jax.ShapeDtypeStruct(c.shape, BF16) for c in casts],
        scratch_shapes=[
            pltpu.VMEM((1, LANES), F32),
            pltpu.VMEM((1, LANES), F32),
            pltpu.VMEM((n_tiles, LANES), F32),
        ],
        compiler_params=pltpu.CompilerParams(
            dimension_semantics=("arbitrary", "arbitrary"),
            vmem_limit_bytes=VMEM_LIMIT),
        name="qkv",
    )(x, g, w, bfp, tri, sel, ind, *casts)


def _attn_kernel(first_ref, qT_ref, kx_ref, vT_ref, o_ref, qx_ref, s0_ref, s1_ref, mx_ref,
                 m_ref, acc_ref, *, tq, tkv, cw, tile, n_tiles):
    bi, pair, qi = pl.program_id(0), pl.program_id(1), pl.program_id(2)
    heads = range(HEADS_PER_STEP)
    n_diag = tq // tkv
    n_full = qi * n_diag
    s_bufs = (s0_ref, s1_ref)

    t0 = n_tiles
    for d in range(tq // tile):
        for h in heads:
            t0 = jnp.minimum(t0, first_ref[(bi * n_tiles + qi * (tq // tile) + d) * N_HEADS
                                           + pair * HEADS_PER_STEP + h])
    pair0 = lax.shift_right_logical(jnp.minimum(t0 * (tile // tkv), n_full), 1)
    j0 = 2 * pair0

    qT = qT_ref[0].astype(F32)
    row = lax.broadcasted_iota(jnp.int32, (LANES, tq), 0)
    for h in heads:
        mine = (row >= h * HEAD_DIM) & (row < (h + 1) * HEAD_DIM)
        ones = (row >= h * N_SPLIT) & (row < (h + 1) * N_SPLIT)
        qx_ref[h, 0:LANES, :] = jnp.where(mine, qT, 0.0).astype(BF16)
        qx_ref[h, LANES:PAIR_W, :] = jnp.where(ones, 1.0, 0.0).astype(BF16)
    m_ref[...] = jnp.full_like(m_ref, MASKED)
    acc_ref[...] = jnp.zeros_like(acc_ref)
    ones_rows = jnp.ones((BF16_ROWS, tkv), BF16)
    krow = lax.broadcasted_iota(jnp.int32, (tkv, cw), 0)
    qcol = lax.broadcasted_iota(jnp.int32, (tkv, cw), 1)

    def units_for(d):
        out = []
        for h in heads:
            for c in range(tq // cw):
                if d is not None and (c + 1) * cw <= d * tkv:
                    continue
                masked = d is not None and c * cw < (d + 1) * tkv - 1
                out.append((h, c, krow + (d * tkv - c * cw) <= qcol if masked else None))
        return out

    def qk_unit(j, buf, unit):
        h, c, mask = unit
        cols = slice(c * cw, (c + 1) * cw)
        start = pl.multiple_of(j * tkv, tkv)
        s = _dot(kx_ref[0, pl.ds(start, tkv), :], qx_ref[h, :, cols])
        if mask is not None:
            s = jnp.where(mask, s, MASKED)
        s_bufs[buf][h, :, cols] = s
        mx_ref[buf, h, :, cols] = jnp.max(s, axis=0, keepdims=True)

    def softmax_unit(buf, unit):
        h, c, _ = unit
        cols = slice(c * cw, (c + 1) * cw)
        m_old = m_ref[h, :, cols]
        m_new = jnp.maximum(m_old, mx_ref[buf, h, :, cols])
        m_ref[h, :, cols] = m_new
        p = jnp.exp2(s_bufs[buf][h, :, cols] - m_new).astype(BF16)
        return p, jnp.exp2(m_old - m_new)

    def pv_unit(j, unit, p, alpha):
        h, c, _ = unit
        cols = slice(c * cw, (c + 1) * cw)
        vx = jnp.concatenate(
            [vT_ref[0, j, h * HEAD_DIM:(h + 1) * HEAD_DIM, :], ones_rows], axis=0)
        acc_ref[h, :, cols] = alpha * acc_ref[h, :, cols] + _dot(vx, p)

    def step(j, buf, cur_d=None, next_d=None, has_next=True):
        cur = units_for(cur_d)
        nxt = units_for(next_d) if has_next else []
        pending = None
        for i in range(max(len(cur), len(nxt))):
            if i < len(nxt):
                qk_unit(j + 1, 1 - buf, nxt[i])
            if pending is not None:
                pv_unit(j, *pending)
                pending = None
            if i < len(cur):
                pending = (cur[i],) + softmax_unit(buf, cur[i])
        if pending is not None:
            pv_unit(j, *pending)

    def full_pair(i, carry):
        step(2 * i, 0)
        step(2 * i + 1, 1)
        return carry

    @pl.when(j0 == n_full)
    def _():
        for unit in units_for(0):
            qk_unit(n_full, 0, unit)

    @pl.when(j0 < n_full)
    def _():
        for unit in units_for(None):
            qk_unit(j0, 0, unit)
        lax.fori_loop(pair0, qi * (n_diag // 2) - 1, full_pair, 0)
        step(n_full - 2, 0)
        step(n_full - 1, 1, next_d=0)

    for d in range(n_diag):
        last = d + 1 == n_diag
        step(n_full + d, d % 2, cur_d=d, next_d=None if last else d + 1,
             has_next=not last)

    outs = []
    for h in heads:
        a = acc_ref[h]
        outs.append(a[0:HEAD_DIM] / a[HEAD_DIM:HEAD_DIM + 1])
    o = jnp.concatenate(outs, axis=0)
    o_ref[0] = o.T.astype(BF16)


def _attn_call(first, qT, kx, vT):
    b, _, s = qT.shape
    tq, tkv = TQ, TKV
    n_pairs = N_HEADS // HEADS_PER_STEP
    kern = functools.partial(_attn_kernel, tq=tq, tkv=tkv, cw=CW, tile=TM_QKV,
                             n_tiles=s // TM_QKV)
    grid_spec = pltpu.PrefetchScalarGridSpec(
        num_scalar_prefetch=1,
        grid=(b, n_pairs, s // tq),
        in_specs=[
            pl.BlockSpec((1, LANES, tq), lambda bi, p, qi, first: (bi, p, qi)),
            pl.BlockSpec((1, s, PAIR_W), lambda bi, p, qi, first: (bi, 0, p)),
            pl.BlockSpec((1, s // tkv, LANES, tkv), lambda bi, p, qi, first: (bi, 0, p, 0)),
        ],
        out_specs=pl.BlockSpec((1, tq, LANES), lambda bi, p, qi, first: (bi, qi, p)),
        scratch_shapes=[
            pltpu.VMEM((HEADS_PER_STEP, PAIR_W, tq), BF16),
            pltpu.VMEM((HEADS_PER_STEP, tkv, tq), F32),
            pltpu.VMEM((HEADS_PER_STEP, tkv, tq), F32),
            pltpu.VMEM((2, HEADS_PER_STEP, 1, tq), F32),
            pltpu.VMEM((HEADS_PER_STEP, 1, tq), F32),
            pltpu.VMEM((HEADS_PER_STEP, HEAD_DIM + BF16_ROWS, tq), F32),
        ],
    )
    return pl.pallas_call(
        kern,
        grid_spec=grid_spec,
        out_shape=jax.ShapeDtypeStruct((b, s, ATTN_WIDTH), BF16),
        compiler_params=pltpu.CompilerParams(
            dimension_semantics=("arbitrary", "arbitrary", "arbitrary"),
            vmem_limit_bytes=VMEM_LIMIT),
        name="attn",
    )(first, qT, kx, vT)


def _mix_kernel(x_ref, yb_ref, gpre_ref, gpost_ref, wcg_ref, cw_ref, wb_ref, wo_ref,
                o_ref, ztail_ref, *, tm, sub, tiles_per_seq, d_model):
    parts = [slice(r * sub, (r + 1) * sub) for r in range(tm // sub)]
    xs = [x_ref[rows, :] for rows in parts]
    projs = []
    for x in xs:
        h = _rmsnorm(x, gpre_ref[...]).astype(BF16)
        projs.append(_dot_nt(h, wcg_ref[...]))

    first = pl.program_id(0) % tiles_per_seq == 0
    zh = jnp.where(first, 0.0, ztail_ref[...])
    branches = []
    for rows, proj in zip(parts, projs):
        gate_b = proj[:, 0:CONV_CH]
        z = proj[:, CONV_CH:2 * CONV_CH] * proj[:, 2 * CONV_CH:3 * CONV_CH]
        zz = jnp.concatenate([zh, z], axis=0)
        zh = z[sub - HALO:sub]
        conv = cw_ref[CONV_K - 1:CONV_K, :] * z
        for back in range(1, CONV_K):
            shifted = pltpu.roll(zz, back, 0)[HALO:]
            conv = conv + cw_ref[CONV_K - 1 - back:CONV_K - back, :] * shifted
        ya = (gate_b * conv).astype(BF16)
        branches.append((_dot(ya, wb_ref[0:CONV_CH, :]),
                         _dot(yb_ref[rows, :], wb_ref[CONV_CH:CONV_CH + ATTN_WIDTH, :])))
    ztail_ref[...] = zh

    g0 = 3 * CONV_CH
    ms = []
    for proj, (ya_d, yb_d) in zip(projs, branches):
        gc = jax.nn.sigmoid(proj[:, g0:g0 + d_model])
        ga = jax.nn.sigmoid(proj[:, g0 + d_model:g0 + 2 * d_model])
        merged = (gc * ya_d + ga * yb_d).astype(BF16)
        ms.append(_dot(merged, wo_ref[...]))
    for rows, x, m in zip(parts, xs, ms):
        o_ref[rows, :] = x + _rmsnorm(m, gpost_ref[...])


def _mix_call(x2, yb2, gpre, gpost, wcg, cw, wb, wo, seq_len):
    n, d = x2.shape
    tm = TM_MIX
    kern = functools.partial(_mix_kernel, tm=tm, sub=SUB_ROWS, tiles_per_seq=seq_len // tm,
                             d_model=d)
    return pl.pallas_call(
        kern,
        grid=(n // tm,),
        in_specs=[
            pl.BlockSpec((tm, d), lambda i: (i, 0)),
            pl.BlockSpec((tm, ATTN_WIDTH), lambda i: (i, 0)),
            _const_spec(gpre.shape), _const_spec(gpost.shape), _const_spec(wcg.shape),
            _const_spec(cw.shape), _const_spec(wb.shape), _const_spec(wo.shape),
        ],
        out_specs=pl.BlockSpec((tm, d), lambda i: (i, 0)),
        out_shape=jax.ShapeDtypeStruct((n, d), F32),
        scratch_shapes=[pltpu.VMEM((HALO, CONV_CH), F32)],
        compiler_params=pltpu.CompilerParams(
            dimension_semantics=("arbitrary",), vmem_limit_bytes=VMEM_LIMIT),
        name="mix",
    )(x2, yb2, gpre, gpost, wcg, cw, wb, wo)


def _ffn_kernel(x_ref, gpre_ref, gpost_ref, wgu_ref, wd_ref, o_ref, *, hidden, sub):
    parts = [slice(r * sub, (r + 1) * sub) for r in range(x_ref.shape[0] // sub)]
    xs = [x_ref[rows, :] for rows in parts]
    gus = []
    for x in xs:
        h = _rmsnorm(x, gpre_ref[...]).astype(BF16)
        gus.append(_dot(h, wgu_ref[...]))
    fs = []
    for gu in gus:
        g = gu[:, 0:hidden]
        u = gu[:, hidden:2 * hidden]
        a = (g * jax.nn.sigmoid(g) * u).astype(BF16)
        fs.append(_dot(a, wd_ref[...]))
    for rows, x, f in zip(parts, xs, fs):
        o_ref[rows, :] = x + _rmsnorm(f, gpost_ref[...])


def _ffn_call(x2, gpre, gpost, wgu, wd):
    n, d = x2.shape
    tm = TM_FFN
    hidden = wd.shape[0]
    kern = functools.partial(_ffn_kernel, hidden=hidden, sub=SUB_ROWS)
    return pl.pallas_call(
        kern,
        grid=(n // tm,),
        in_specs=[
            pl.BlockSpec((tm, d), lambda i: (i, 0)),
            _const_spec(gpre.shape), _const_spec(gpost.shape),
            _const_spec(wgu.shape), _const_spec(wd.shape),
        ],
        out_specs=pl.BlockSpec((tm, d), lambda i: (i, 0)),
        out_shape=jax.ShapeDtypeStruct((n, d), F32),
        compiler_params=pltpu.CompilerParams(
            dimension_semantics=("arbitrary",), vmem_limit_bytes=VMEM_LIMIT),
        name="ffn",
    )(x2, gpre, gpost, wgu, wd)


def _selection_matrix():
    n_pairs = N_HEADS // HEADS_PER_STEP
    sel = np.zeros((LANES, n_pairs * LANES), np.float32)
    for head in range(N_HEADS):
        for j in range(N_SPLIT):
            sel[j * N_HEADS + head, (head // HEADS_PER_STEP) * LANES
                + (head % HEADS_PER_STEP) * N_SPLIT + j] = 1.0
    return jnp.asarray(sel, BF16)


def _norm_indicator():
    ind = np.zeros((2 * ATTN_WIDTH, LANES), np.float32)
    for which in range(2):
        for col in range(ATTN_WIDTH):
            ind[which * ATTN_WIDTH + col, which * N_HEADS + col // HEAD_DIM] = 1.0
    return jnp.asarray(ind, BF16)


def kernel(x, norm_mix_pre, norm_mix_post, w_in, b_f, conv_w, w_branch, w_out,
           norm_ffn_pre, norm_ffn_post, w_gate_up, w_down):
    b, s, d = x.shape
    depth = w_in.shape[0]
    a = ATTN_WIDTH
    c0 = 3 * a + N_HEADS
    tri = jnp.asarray(np.tril(np.ones((TM_QKV, TM_QKV), np.float32)), BF16)
    sel = _selection_matrix()
    ind = _norm_indicator()
    scale = LOG2E / math.sqrt(HEAD_DIM)

    for l in range(depth):
        wT = w_in[l].T
        wfqkv = jnp.concatenate(
            [jnp.pad(wT[3 * a:c0], ((0, LANES - N_HEADS), (0, 0))),
             wT[0:a] * scale, wT[a:3 * a]], axis=0).astype(BF16)
        bfp = jnp.pad(b_f[l], (0, LANES - N_HEADS)).reshape(1, LANES)
        wcg = wT[c0:].astype(BF16)

        qT, kx, vT, first, wb, wo, wgu, wd = _qkv_call(
            x, norm_mix_pre[l].reshape(1, d), wfqkv, bfp, tri, sel, ind,
            [w_branch[l].reshape(CONV_CH + a, d), w_out[l], w_gate_up[l], w_down[l]])
        yb = _attn_call(first[:, :, 0, 0:N_HEADS].reshape(-1), qT, kx, vT)
        x2 = _mix_call(
            x.reshape(b * s, d), yb.reshape(b * s, a),
            norm_mix_pre[l].reshape(1, d), norm_mix_post[l].reshape(1, d),
            wcg, conv_w[l], wb, wo, s)
        x2 = _ffn_call(x2, norm_ffn_pre[l].reshape(1, d), norm_ffn_post[l].reshape(1, d),
                       wgu, wd)
        x = x2.reshape(b, s, d)
    return x
```

```python
import functools
import math

import jax
import jax.numpy as jnp
import numpy as np
from jax import lax
from jax.experimental import pallas as pl
from jax.experimental.pallas import tpu as pltpu

F32 = jnp.float32
BF16 = jnp.bfloat16

RMS_EPS = 1e-6
N_HEADS = 8
HEAD_DIM = 64
ATTN_WIDTH = N_HEADS * HEAD_DIM
CONV_CH = 512
CONV_K = 3
LANES = 128
BF16_ROWS = 16
HEADS_PER_STEP = 2
PAIR_W = 2 * LANES
N_SPLIT = 3
MASKED = -1e30
LOG2E = math.log2(math.e)
PRUNE_GAP = 106.0 * LOG2E
NORM_SLACK = 4.0 * 1.04
VMEM_LIMIT = 60 * 1024 * 1024

TM_QKV = 512
TQ = 1024
TKV = 512
CW = 256
TM_MIX = 1024
TM_FFN = 1024
SUB_ROWS = 256
HALO = 8

NT_DIMS = (((1,), (1,)), ((), ()))


def _rmsnorm(x, g):
    ms = jnp.mean(x * x, axis=-1, keepdims=True)
    return (x * lax.rsqrt(ms + RMS_EPS)) * g


def _split3(v):
    parts = []
    r = v
    for _ in range(N_SPLIT):
        p = r.astype(BF16)
        parts.append(p)
        r = r - p.astype(F32)
    return parts


def _dot(a, b):
    return jnp.dot(a, b, preferred_element_type=F32)


def _dot_nt(a, b):
    return lax.dot_general(a, b, NT_DIMS, preferred_element_type=F32)


def _const_spec(shape):
    nd = len(shape)
    return pl.BlockSpec(shape, lambda *_: (0,) * nd, pipeline_mode=pl.Buffered(1))


def _qkv_kernel(*refs, tm, tkv, n_cast):
    (x_ref, g_ref, w_ref, bf_ref, tri_ref, sel_ref, ind_ref), refs = refs[:7], refs[7:]
    cast_in, refs = refs[:n_cast], refs[n_cast:]
    (qT_ref, kx_ref, vT_ref, first_ref), refs = refs[:4], refs[4:]
    cast_out, (carry_ref, kmax_ref, hist_ref) = refs[:n_cast], refs[n_cast:]
    i = pl.program_id(1)

    for src, dst in zip(cast_in, cast_out):
        dst[...] = src[...].astype(BF16)

    @pl.when(i == 0)
    def _():
        carry_ref[...] = jnp.zeros_like(carry_ref)
        kmax_ref[...] = jnp.zeros_like(kmax_ref)
        hist_ref[...] = jnp.zeros_like(hist_ref)

    a = ATTN_WIDTH
    h = _rmsnorm(x_ref[0], g_ref[...]).astype(BF16)
    fq = _dot_nt(h, w_ref[0:LANES + a, :])
    q = fq[:, LANES:].astype(BF16)
    qT_ref[0, 0] = fq[:, LANES:].T.astype(BF16)

    f = fq[:, 0:LANES] + bf_ref[...]
    logf = (jnp.minimum(f, 0.0) - jnp.log1p(jnp.exp(-jnp.abs(f)))) * LOG2E
    head_lane = lax.broadcasted_iota(jnp.int32, (tm, LANES), 1) < N_HEADS
    k = _dot_nt(h, w_ref[LANES + a:LANES + 2 * a, :]).astype(BF16)
    qf = q.astype(F32)
    kf = k.astype(F32)
    sq = jnp.concatenate([qf * qf, kf * kf], axis=1).astype(BF16)
    norm2 = jnp.max(_dot(sq, ind_ref[...]), axis=0, keepdims=True)

    def pack3(v):
        parts = _split3(jnp.where(head_lane, v, 0.0))
        out = parts[0].astype(F32)
        for j in range(1, N_SPLIT):
            out = out + pltpu.roll(parts[j].astype(F32), j * N_HEADS, 1)
        return out.astype(BF16)

    cs = _dot(tri_ref[...], pack3(logf))
    c_prev = carry_ref[...]
    c = cs + c_prev
    for j in range(1, N_SPLIT):
        c = c + pltpu.roll(cs, LANES - j * N_HEADS, 1)
    c_end = c[tm - 1:tm]
    carry_ref[...] = c_end

    kmax = jnp.maximum(kmax_ref[...], norm2)
    kmax_ref[...] = kmax
    qk2 = NORM_SLACK * norm2 * pltpu.roll(kmax, LANES - N_HEADS, 1)
    gap = hist_ref[...] - c_prev - PRUNE_GAP
    tile = lax.broadcasted_iota(jnp.int32, gap.shape, 0)
    skip = (tile < i) & (gap > 0.0) & (qk2 < gap * gap)
    i_f = i.astype(F32)
    kept = jnp.where(skip, i_f, jnp.minimum(tile.astype(F32), i_f))
    first_ref[0, 0] = jnp.min(kept, axis=0, keepdims=True).astype(jnp.int32)
    hist_ref[pl.ds(i, 1), :] = c_end
    vT = _dot_nt(h, w_ref[LANES + 2 * a:LANES + 3 * a, :]).T
    for cb in range(tm // tkv):
        vT_ref[0, cb] = vT[:, cb * tkv:(cb + 1) * tkv].astype(BF16)
    ext = _dot(pack3(-c), sel_ref[...])
    for p in range(N_HEADS // HEADS_PER_STEP):
        kx_ref[0, :, p * PAIR_W:p * PAIR_W + LANES] = k[:, p * LANES:(p + 1) * LANES]
        kx_ref[0, :, p * PAIR_W + LANES:(p + 1) * PAIR_W] = (
            ext[:, p * LANES:(p + 1) * LANES].astype(BF16))


def _cast_rider_spec(shape, steps, n_tiles):
    rows, cols = shape
    share = 1
    while (rows * share) % (steps * BF16_ROWS) or steps % share:
        share += 1
    return pl.BlockSpec((rows * share // steps, cols),
                        lambda bi, i: ((bi * n_tiles + i) // share, 0))


def _qkv_call(x, g, w, bfp, tri, sel, ind, casts):
    b, s, d = x.shape
    tm, tkv = TM_QKV, TKV
    n_pairs = N_HEADS // HEADS_PER_STEP
    n_tiles = s // tm
    kern = functools.partial(_qkv_kernel, tm=tm, tkv=tkv, n_cast=len(casts))
    rider_specs = [_cast_rider_spec(c.shape, b * n_tiles, n_tiles) for c in casts]
    return pl.pallas_call(
        kern,
        grid=(b, n_tiles),
        in_specs=[
            pl.BlockSpec((1, tm, d), lambda bi, i: (bi, i, 0)),
            _const_spec(g.shape), _const_spec(w.shape), _const_spec(bfp.shape),
            _const_spec(tri.shape), _const_spec(sel.shape), _const_spec(ind.shape),
        ] + rider_specs,
        out_specs=[
            pl.BlockSpec((1, 1, ATTN_WIDTH, tm), lambda bi, i: (bi, i, 0, 0)),
            pl.BlockSpec((1, tm, n_pairs * PAIR_W), lambda bi, i: (bi, i, 0)),
            pl.BlockSpec((1, tm // tkv, ATTN_WIDTH, tkv), lambda bi, i: (bi, i, 0, 0)),
            pl.BlockSpec((1, 1, 1, LANES), lambda bi, i: (bi, i, 0, 0)),
        ] + rider_specs,
        out_shape=[
            jax.ShapeDtypeStruct((b, n_tiles, ATTN_WIDTH, tm), BF16),
            jax.ShapeDtypeStruct((b, s, n_pairs * PAIR_W), BF16),
            jax.ShapeDtypeStruct((b, s // tkv, ATTN_WIDTH, tkv), BF16),
            jax.ShapeDtypeStruct((b, n_tiles, 1, LANES), jnp.int32),
        ] + [jax.ShapeDtypeStruct(c.shape, BF16) for c in casts],
        scratch_shapes=[
            pltpu.VMEM((1, LANES), F32),
            pltpu.VMEM((1, LANES), F32),
            pltpu.VMEM((n_tiles, LANES), F32),
        ],
        compiler_params=pltpu.CompilerParams(
            dimension_semantics=("arbitrary", "arbitrary"),
            vmem_limit_bytes=VMEM_LIMIT),
        name="qkv",
    )(x, g, w, bfp, tri, sel, ind, *casts)


def _attn_kernel(first_ref, qT_ref, kx_ref, vT_ref, o_ref, qx_ref, s0_ref, s1_ref, mx_ref,
                 m_ref, acc_ref, *, tq, tkv, cw, tile, n_tiles, n_q):
    bi, pair = pl.program_id(0), pl.program_id(1)
    heads = range(HEADS_PER_STEP)
    n_diag = tq // tkv
    sub_q = tq // tile
    s_bufs = (s0_ref, s1_ref)
    ones_rows = jnp.ones((BF16_ROWS, tkv), BF16)
    krow = lax.broadcasted_iota(jnp.int32, (tkv, cw), 0)
    qcol = lax.broadcasted_iota(jnp.int32, (tkv, cw), 1)
    row = lax.broadcasted_iota(jnp.int32, (LANES, tile), 0)

    def build_qx(qi):
        for d in range(sub_q):
            qT = qT_ref[0, qi * sub_q + d].astype(F32)
            for h in heads:
                mine = (row >= h * HEAD_DIM) & (row < (h + 1) * HEAD_DIM)
                qx_ref[h, 0:LANES, d * tile:(d + 1) * tile] = (
                    jnp.where(mine, qT, 0.0).astype(BF16))

    def reset_state():
        m_ref[...] = jnp.full_like(m_ref, MASKED)
        acc_ref[...] = jnp.zeros_like(acc_ref)

    def first_block(qi):
        t0 = n_tiles
        for d in range(sub_q):
            for h in heads:
                t0 = jnp.minimum(t0, first_ref[(bi * n_tiles + qi * sub_q + d) * N_HEADS
                                               + pair * HEADS_PER_STEP + h])
        j0 = jnp.minimum(t0 * (tile // tkv), qi * n_diag - 2)
        return 2 * lax.shift_right_logical(j0, 1)

    def units_for(d):
        out = []
        for h in heads:
            for c in range(tq // cw):
                if d is not None and (c + 1) * cw <= d * tkv:
                    continue
                masked = d is not None and c * cw < (d + 1) * tkv - 1
                out.append((h, c, krow + (d * tkv - c * cw) <= qcol if masked else None))
        return out

    def qk_unit(j, buf, unit):
        h, c, mask = unit
        cols = slice(c * cw, (c + 1) * cw)
        start = pl.multiple_of(j * tkv, tkv)
        s = _dot(kx_ref[0, pl.ds(start, tkv), :], qx_ref[h, :, cols])
        if mask is not None:
            s = jnp.where(mask, s, MASKED)
        s_bufs[buf][h, :, cols] = s
        mx_ref[buf, h, :, cols] = jnp.max(s, axis=0, keepdims=True)

    def softmax_unit(buf, unit):
        h, c, _ = unit
        cols = slice(c * cw, (c + 1) * cw)
        m_old = m_ref[h, :, cols]
        m_new = jnp.maximum(m_old, mx_ref[buf, h, :, cols])
        m_ref[h, :, cols] = m_new
        p = jnp.exp2(s_bufs[buf][h, :, cols] - m_new).astype(BF16)
        return p, jnp.exp2(m_old - m_new)

    def pv_unit(j, unit, p, alpha):
        h, c, _ = unit
        cols = slice(c * cw, (c + 1) * cw)
        vx = jnp.concatenate(
            [vT_ref[0, j, h * HEAD_DIM:(h + 1) * HEAD_DIM, :], ones_rows], axis=0)
        acc_ref[h, :, cols] = alpha * acc_ref[h, :, cols] + _dot(vx, p)

    def step(j, buf, cur_d=None, next_d=None, next_j=None, before_next=None):
        cur = units_for(cur_d)
        nxt = units_for(next_d)
        next_j = j + 1 if next_j is None else next_j
        if before_next is not None:
            before_next()
        pending = None
        for i in range(max(len(cur), len(nxt))):
            if i < len(nxt):
                qk_unit(next_j, 1 - buf, nxt[i])
            if pending is not None:
                pv_unit(j, *pending)
                pending = None
            if i < len(cur):
                pending = (cur[i],) + softmax_unit(buf, cur[i])
        if pending is not None:
            pv_unit(j, *pending)

    def diagonal_and_finish(qi):
        n_full = qi * n_diag
        for d in range(n_diag - 1):
            step(n_full + d, d % 2, cur_d=d, next_d=d + 1)
        nq = jnp.minimum(qi + 1, n_q - 1)
        step(n_full + n_diag - 1, (n_diag - 1) % 2, cur_d=n_diag - 1, next_d=None,
             next_j=first_block(nq), before_next=lambda: build_qx(nq))
        outs = []
        for h in heads:
            a = acc_ref[h]
            outs.append(a[0:HEAD_DIM] / a[HEAD_DIM:HEAD_DIM + 1])
        o = jnp.concatenate(outs, axis=0)
        start = qi * tq if isinstance(qi, int) else pl.multiple_of(qi * tq, tq)
        o_ref[0, pl.ds(start, tq), :] = o.T.astype(BF16)
        reset_state()

    def full_pair(i, carry):
        step(2 * i, 0)
        step(2 * i + 1, 1)
        return carry

    def later_tile(qi, carry):
        n_full = qi * n_diag
        lax.fori_loop(lax.shift_right_logical(first_block(qi), 1),
                      qi * (n_diag // 2) - 1, full_pair, 0)
        step(n_full - 2, 0)
        step(n_full - 1, 1, next_d=0)
        diagonal_and_finish(qi)
        return carry

    rowx = lax.broadcasted_iota(jnp.int32, (LANES, tq), 0)
    for h in heads:
        ones = (rowx >= h * N_SPLIT) & (rowx < (h + 1) * N_SPLIT)
        qx_ref[h, LANES:PAIR_W, :] = jnp.where(ones, 1.0, 0.0).astype(BF16)
    reset_state()
    build_qx(0)
    for unit in units_for(0):
        qk_unit(0, 0, unit)
    diagonal_and_finish(0)
    lax.fori_loop(1, n_q, later_tile, 0)


def _attn_call(first, qT, kx, vT):
    b, s = kx.shape[0], kx.shape[1]
    tq, tkv = TQ, TKV
    n_pairs = N_HEADS // HEADS_PER_STEP
    n_tiles = s // TM_QKV
    kern = functools.partial(_attn_kernel, tq=tq, tkv=tkv, cw=CW, tile=TM_QKV,
                             n_tiles=n_tiles, n_q=s // tq)
    grid_spec = pltpu.PrefetchScalarGridSpec(
        num_scalar_prefetch=1,
        grid=(b, n_pairs),
        in_specs=[
            pl.BlockSpec((1, n_tiles, LANES, TM_QKV), lambda bi, p, first: (bi, 0, p, 0)),
            pl.BlockSpec((1, s, PAIR_W), lambda bi, p, first: (bi, 0, p)),
            pl.BlockSpec((1, s // tkv, LANES, tkv), lambda bi, p, first: (bi, 0, p, 0)),
        ],
        out_specs=pl.BlockSpec((1, s, LANES), lambda bi, p, first: (bi, 0, p)),
        scratch_shapes=[
            pltpu.VMEM((HEADS_PER_STEP, PAIR_W, tq), BF16),
            pltpu.VMEM((HEADS_PER_STEP, tkv, tq), F32),
            pltpu.VMEM((HEADS_PER_STEP, tkv, tq), F32),
            pltpu.VMEM((2, HEADS_PER_STEP, 1, tq), F32),
            pltpu.VMEM((HEADS_PER_STEP, 1, tq), F32),
            pltpu.VMEM((HEADS_PER_STEP, HEAD_DIM + BF16_ROWS, tq), F32),
        ],
    )
    return pl.pallas_call(
        kern,
        grid_spec=grid_spec,
        out_shape=jax.ShapeDtypeStruct((b, s, ATTN_WIDTH), BF16),
        compiler_params=pltpu.CompilerParams(
            dimension_semantics=("arbitrary", "arbitrary"),
            vmem_limit_bytes=VMEM_LIMIT),
        name="attn",
    )(first, qT, kx, vT)


def _mix_kernel(x_ref, yb_ref, gpre_ref, gpost_ref, wcg_ref, cw_ref, wb_ref, wo_ref,
                o_ref, ztail_ref, *, tm, sub, tiles_per_seq, d_model):
    parts = [slice(r * sub, (r + 1) * sub) for r in range(tm // sub)]
    xs = [x_ref[rows, :] for rows in parts]
    projs = []
    for x in xs:
        h = _rmsnorm(x, gpre_ref[...]).astype(BF16)
        projs.append(_dot_nt(h, wcg_ref[...]))

    first = pl.program_id(0) % tiles_per_seq == 0
    zh = jnp.where(first, 0.0, ztail_ref[...])
    branches = []
    for rows, proj in zip(parts, projs):
        gate_b = proj[:, 0:CONV_CH]
        z = proj[:, CONV_CH:2 * CONV_CH] * proj[:, 2 * CONV_CH:3 * CONV_CH]
        zz = jnp.concatenate([zh, z], axis=0)
        zh = z[sub - HALO:sub]
        conv = cw_ref[CONV_K - 1:CONV_K, :] * z
        for back in range(1, CONV_K):
            shifted = pltpu.roll(zz, back, 0)[HALO:]
            conv = conv + cw_ref[CONV_K - 1 - back:CONV_K - back, :] * shifted
        ya = (gate_b * conv).astype(BF16)
        branches.append((_dot(ya, wb_ref[0:CONV_CH, :]),
                         _dot(yb_ref[rows, :], wb_ref[CONV_CH:CONV_CH + ATTN_WIDTH, :])))
    ztail_ref[...] = zh

    g0 = 3 * CONV_CH
    ms = []
    for proj, (ya_d, yb_d) in zip(projs, branches):
        gc = jax.nn.sigmoid(proj[:, g0:g0 + d_model])
        ga = jax.nn.sigmoid(proj[:, g0 + d_model:g0 + 2 * d_model])
        merged = (gc * ya_d + ga * yb_d).astype(BF16)
        ms.append(_dot(merged, wo_ref[...]))
    for rows, x, m in zip(parts, xs, ms):
        o_ref[rows, :] = x + _rmsnorm(m, gpost_ref[...])


def _mix_call(x2, yb2, gpre, gpost, wcg, cw, wb, wo, seq_len):
    n, d = x2.shape
    tm = TM_MIX
    kern = functools.partial(_mix_kernel, tm=tm, sub=SUB_ROWS, tiles_per_seq=seq_len // tm,
                             d_model=d)
    return pl.pallas_call(
        kern,
        grid=(n // tm,),
        in_specs=[
            pl.BlockSpec((tm, d), lambda i: (i, 0)),
            pl.BlockSpec((tm, ATTN_WIDTH), lambda i: (i, 0)),
            _const_spec(gpre.shape), _const_spec(gpost.shape), _const_spec(wcg.shape),
            _const_spec(cw.shape), _const_spec(wb.shape), _const_spec(wo.shape),
        ],
        out_specs=pl.BlockSpec((tm, d), lambda i: (i, 0)),
        out_shape=jax.ShapeDtypeStruct((n, d), F32),
        scratch_shapes=[pltpu.VMEM((HALO, CONV_CH), F32)],
        compiler_params=pltpu.CompilerParams(
            dimension_semantics=("arbitrary",), vmem_limit_bytes=VMEM_LIMIT),
        name="mix",
    )(x2, yb2, gpre, gpost, wcg, cw, wb, wo)


def _ffn_kernel(x_ref, gpre_ref, gpost_ref, wgu_ref, wd_ref, o_ref, *, hidden, sub):
    parts = [slice(r * sub, (r + 1) * sub) for r in range(x_ref.shape[0] // sub)]
    xs = [x_ref[rows, :] for rows in parts]
    gus = []
    for x in xs:
        h = _rmsnorm(x, gpre_ref[...]).astype(BF16)
        gus.append(_dot(h, wgu_ref[...]))
    fs = []
    for gu in gus:
        g = gu[:, 0:hidden]
        u = gu[:, hidden:2 * hidden]
        a = (g * jax.nn.sigmoid(g) * u).astype(BF16)
        fs.append(_dot(a, wd_ref[...]))
    for rows, x, f in zip(parts, xs, fs):
        o_ref[rows, :] = x + _rmsnorm(f, gpost_ref[...])


def _ffn_call(x2, gpre, gpost, wgu, wd):
    n, d = x2.shape
    tm = TM_FFN
    hidden = wd.shape[0]
    kern = functools.partial(_ffn_kernel, hidden=hidden, sub=SUB_ROWS)
    return pl.pallas_call(
        kern,
        grid=(n // tm,),
        in_specs=[
            pl.BlockSpec((tm, d), lambda i: (i, 0)),
            _const_spec(gpre.shape), _const_spec(gpost.shape),
            _const_spec(wgu.shape), _const_spec(wd.shape),
        ],
        out_specs=pl.BlockSpec((tm, d), lambda i: (i, 0)),
        out_shape=jax.ShapeDtypeStruct((n, d), F32),
        compiler_params=pltpu.CompilerParams(
            dimension_semantics=("arbitrary",), vmem_limit_bytes=VMEM_LIMIT),
        name="ffn",
    )(x2, gpre, gpost, wgu, wd)


def _selection_matrix():
    n_pairs = N_HEADS // HEADS_PER_STEP
    sel = np.zeros((LANES, n_pairs * LANES), np.float32)
    for head in range(N_HEADS):
        for j in range(N_SPLIT):
            sel[j * N_HEADS + head, (head // HEADS_PER_STEP) * LANES
                + (head % HEADS_PER_STEP) * N_SPLIT + j] = 1.0
    return jnp.asarray(sel, BF16)


def _norm_indicator():
    ind = np.zeros((2 * ATTN_WIDTH, LANES), np.float32)
    for which in range(2):
        for col in range(ATTN_WIDTH):
            ind[which * ATTN_WIDTH + col, which * N_HEADS + col // HEAD_DIM] = 1.0
    return jnp.asarray(ind, BF16)


def kernel(x, norm_mix_pre, norm_mix_post, w_in, b_f, conv_w, w_branch, w_out,
           norm_ffn_pre, norm_ffn_post, w_gate_up, w_down):
    b, s, d = x.shape
    depth = w_in.shape[0]
    a = ATTN_WIDTH
    c0 = 3 * a + N_HEADS
    tri = jnp.asarray(np.tril(np.ones((TM_QKV, TM_QKV), np.float32)), BF16)
    sel = _selection_matrix()
    ind = _norm_indicator()
    scale = LOG2E / math.sqrt(HEAD_DIM)

    for l in range(depth):
        wT = w_in[l].T
        wfqkv = jnp.concatenate(
            [jnp.pad(wT[3 * a:c0], ((0, LANES - N_HEADS), (0, 0))),
             wT[0:a] * scale, wT[a:3 * a]], axis=0).astype(BF16)
        bfp = jnp.pad(b_f[l], (0, LANES - N_HEADS)).reshape(1, LANES)
        wcg = wT[c0:].astype(BF16)

        qT, kx, vT, first, wb, wo, wgu, wd = _qkv_call(
            x, norm_mix_pre[l].reshape(1, d), wfqkv, bfp, tri, sel, ind,
            [w_branch[l].reshape(CONV_CH + a, d), w_out[l], w_gate_up[l], w_down[l]])
        yb = _attn_call(first[:, :, 0, 0:N_HEADS].reshape(-1), qT, kx, vT)
        x2 = _mix_call(
            x.reshape(b * s, d), yb.reshape(b * s, a),
            norm_mix_pre[l].reshape(1, d), norm_mix_post[l].reshape(1, d),
            wcg, conv_w[l], wb, wo, s)
        x2 = _ffn_call(x2, norm_ffn_pre[l].reshape(1, d), norm_ffn_post[l].reshape(1, d),
                       wgu, wd)
        x = x2.reshape(b, s, d)
    return x
```

```python
import functools
import math

import jax
import jax.numpy as jnp
import numpy as np
from jax import lax
from jax.experimental import pallas as pl
from jax.experimental.pallas import tpu as pltpu

F32 = jnp.float32
BF16 = jnp.bfloat16

RMS_EPS = 1e-6
N_HEADS = 8
HEAD_DIM = 64
ATTN_WIDTH = N_HEADS * HEAD_DIM
CONV_CH = 512
CONV_K = 3
LANES = 128
BF16_ROWS = 16
HEADS_PER_STEP = 2
PAIR_W = 2 * LANES
N_SPLIT = 3
MASKED = -1e30
LOG2E = math.log2(math.e)
PRUNE_GAP = 106.0 * LOG2E
NORM_SLACK = 4.0 * 1.04
VMEM_LIMIT = 60 * 1024 * 1024

TM_QKV = 512
TQ = 1024
TKV = 512
CW = 256
TM_MIX = 1024
TM_FFN = 1024
SUB_ROWS = 256
HALO = 8

NT_DIMS = (((1,), (1,)), ((), ()))


def _rmsnorm(x, g):
    ms = jnp.mean(x * x, axis=-1, keepdims=True)
    return (x * lax.rsqrt(ms + RMS_EPS)) * g


def _split3(v):
    parts = []
    r = v
    for _ in range(N_SPLIT):
        p = r.astype(BF16)
        parts.append(p)
        r = r - p.astype(F32)
    return parts


def _dot(a, b):
    return jnp.dot(a, b, preferred_element_type=F32)


def _dot_nt(a, b):
    return lax.dot_general(a, b, NT_DIMS, preferred_element_type=F32)


def _const_spec(shape):
    nd = len(shape)
    return pl.BlockSpec(shape, lambda *_: (0,) * nd, pipeline_mode=pl.Buffered(1))


def _qkv_kernel(*refs, tm, tkv, n_cast, scale):
    (x_ref, g_ref, w_ref, bf_ref, tri_ref, sel_ref, ind_ref), refs = refs[:7], refs[7:]
    cast_in, refs = refs[:n_cast], refs[n_cast:]
    (qT_ref, kx_ref, vT_ref, first_ref), refs = refs[:4], refs[4:]
    cast_out, (w16_ref, carry_ref, kmax_ref, hist_ref) = refs[:n_cast], refs[n_cast:]
    i = pl.program_id(1)

    for src, dst in zip(cast_in, cast_out):
        dst[...] = src[...].astype(BF16)

    a = ATTN_WIDTH

    @pl.when((pl.program_id(0) == 0) & (i == 0))
    def _():
        w16_ref[0:a, :] = (w_ref[0:a, :] * scale).astype(BF16)
        w16_ref[a:, :] = w_ref[a:, :].astype(BF16)

    @pl.when(i == 0)
    def _():
        carry_ref[...] = jnp.zeros_like(carry_ref)
        kmax_ref[...] = jnp.zeros_like(kmax_ref)
        hist_ref[...] = jnp.zeros_like(hist_ref)

    h = _rmsnorm(x_ref[0], g_ref[...]).astype(BF16)
    vf = _dot_nt(h, w16_ref[2 * a:3 * a + LANES, :])
    vT = vf[:, 0:a].T
    for cb in range(tm // tkv):
        vT_ref[0, cb] = vT[:, cb * tkv:(cb + 1) * tkv].astype(BF16)

    f = vf[:, a:a + LANES] + bf_ref[...]
    logf = (jnp.minimum(f, 0.0) - jnp.log1p(jnp.exp(-jnp.abs(f)))) * LOG2E
    head_lane = lax.broadcasted_iota(jnp.int32, (tm, LANES), 1) < N_HEADS
    qf32 = _dot_nt(h, w16_ref[0:a, :])
    q = qf32.astype(BF16)
    qT_ref[0, 0] = qf32.T.astype(BF16)

    def pack3(v):
        parts = _split3(jnp.where(head_lane, v, 0.0))
        out = parts[0].astype(F32)
        for j in range(1, N_SPLIT):
            out = out + pltpu.roll(parts[j].astype(F32), j * N_HEADS, 1)
        return out.astype(BF16)

    cs = _dot(tri_ref[...], pack3(logf))
    k = _dot_nt(h, w16_ref[a:2 * a, :]).astype(BF16)
    qf = q.astype(F32)
    kf = k.astype(F32)
    sq = jnp.concatenate([qf * qf, kf * kf], axis=1).astype(BF16)
    norm2 = jnp.max(_dot(sq, ind_ref[...]), axis=0, keepdims=True)
    c_prev = carry_ref[...]
    c = cs + c_prev
    for j in range(1, N_SPLIT):
        c = c + pltpu.roll(cs, LANES - j * N_HEADS, 1)
    c_end = c[tm - 1:tm]
    carry_ref[...] = c_end

    kmax = jnp.maximum(kmax_ref[...], norm2)
    kmax_ref[...] = kmax
    qk2 = NORM_SLACK * norm2 * pltpu.roll(kmax, LANES - N_HEADS, 1)
    gap = hist_ref[...] - c_prev - PRUNE_GAP
    tile = lax.broadcasted_iota(jnp.int32, gap.shape, 0)
    skip = (tile < i) & (gap > 0.0) & (qk2 < gap * gap)
    i_f = i.astype(F32)
    kept = jnp.where(skip, i_f, jnp.minimum(tile.astype(F32), i_f))
    first_ref[0, 0] = jnp.min(kept, axis=0, keepdims=True).astype(jnp.int32)
    hist_ref[pl.ds(i, 1), :] = c_end
    ext = _dot(pack3(-c), sel_ref[...])
    for p in range(N_HEADS // HEADS_PER_STEP):
        kx_ref[0, :, p * PAIR_W:p * PAIR_W + LANES] = k[:, p * LANES:(p + 1) * LANES]
        kx_ref[0, :, p * PAIR_W + LANES:(p + 1) * PAIR_W] = (
            ext[:, p * LANES:(p + 1) * LANES].astype(BF16))


def _cast_rider_spec(shape, steps, n_tiles):
    rows, cols = shape
    share = 1
    while (rows * share) % (steps * BF16_ROWS) or steps % share:
        share += 1
    return pl.BlockSpec((rows * share // steps, cols),
                        lambda bi, i: ((bi * n_tiles + i) // share, 0))


def _qkv_call(x, g, wT, bfp, tri, sel, ind, casts, scale):
    b, s, d = x.shape
    tm, tkv = TM_QKV, TKV
    n_pairs = N_HEADS // HEADS_PER_STEP
    n_tiles = s // tm
    steps = b * n_tiles
    own_rows = 3 * ATTN_WIDTH + LANES
    rest0 = 3 * ATTN_WIDTH + N_HEADS
    rest_rows = wT.shape[0] - rest0
    rest_blk = rest_rows // steps
    assert rest_rows % (steps * BF16_ROWS) == 0 and rest0 % 8 == 0
    kern = functools.partial(_qkv_kernel, tm=tm, tkv=tkv, n_cast=len(casts) + 1, scale=scale)
    rider_specs = [_cast_rider_spec(c.shape, steps, n_tiles) for c in casts]
    rest_in = pl.BlockSpec((pl.Element(rest_blk), pl.Element(d)),
                           lambda bi, i: (pl.multiple_of(
                               rest0 + (bi * n_tiles + i) * rest_blk, 8), 0))
    rest_out = pl.BlockSpec((rest_blk, d), lambda bi, i: (bi * n_tiles + i, 0))
    return pl.pallas_call(
        kern,
        grid=(b, n_tiles),
        in_specs=[
            pl.BlockSpec((1, tm, d), lambda bi, i: (bi, i, 0)),
            _const_spec(g.shape),
            pl.BlockSpec((own_rows, d), lambda bi, i: (0, 0), pipeline_mode=pl.Buffered(1)),
            _const_spec(bfp.shape),
            _const_spec(tri.shape), _const_spec(sel.shape), _const_spec(ind.shape),
        ] + rider_specs + [rest_in],
        out_specs=[
            pl.BlockSpec((1, 1, ATTN_WIDTH, tm), lambda bi, i: (bi, i, 0, 0)),
            pl.BlockSpec((1, tm, n_pairs * PAIR_W), lambda bi, i: (bi, i, 0)),
            pl.BlockSpec((1, tm // tkv, ATTN_WIDTH, tkv), lambda bi, i: (bi, i, 0, 0)),
            pl.BlockSpec((1, 1, 1, LANES), lambda bi, i: (bi, i, 0, 0)),
        ] + rider_specs + [rest_out],
        out_shape=[
            jax.ShapeDtypeStruct((b, n_tiles, ATTN_WIDTH, tm), BF16),
            jax.ShapeDtypeStruct((b, s, n_pairs * PAIR_W), BF16),
            jax.ShapeDtypeStruct((b, s // tkv, ATTN_WIDTH, tkv), BF16),
            jax.ShapeDtypeStruct((b, n_tiles, 1, LANES), jnp.int32),
        ] + [jax.ShapeDtypeStruct(c.shape, BF16) for c in casts]
        + [jax.ShapeDtypeStruct((rest_rows, d), BF16)],
        scratch_shapes=[
            pltpu.VMEM((own_rows, d), BF16),
            pltpu.VMEM((1, LANES), F32),
            pltpu.VMEM((1, LANES), F32),
            pltpu.VMEM((n_tiles, LANES), F32),
        ],
        compiler_params=pltpu.CompilerParams(
            dimension_semantics=("arbitrary", "arbitrary"),
            vmem_limit_bytes=VMEM_LIMIT),
        name="qkv",
    )(x, g, wT, bfp, tri, sel, ind, *casts, wT)


def _attn_kernel(first_ref, qT_ref, kx_ref, vT_ref, o_ref, qx_ref, s0_ref, s1_ref, mx_ref,
                 m_ref, acc_ref, *, tq, tkv, cw, tile, n_tiles, n_q):
    bi, pair = pl.program_id(0), pl.program_id(1)
    heads = range(HEADS_PER_STEP)
    n_diag = tq // tkv
    sub_q = tq // tile
    s_bufs = (s0_ref, s1_ref)
    ones_rows = jnp.ones((BF16_ROWS, tkv), BF16)
    krow = lax.broadcasted_iota(jnp.int32, (tkv, cw), 0)
    qcol = lax.broadcasted_iota(jnp.int32, (tkv, cw), 1)
    row = lax.broadcasted_iota(jnp.int32, (LANES, tile), 0)

    def build_qx(qi):
        for d in range(sub_q):
            qT = qT_ref[0, qi * sub_q + d].astype(F32)
            for h in heads:
                mine = (row >= h * HEAD_DIM) & (row < (h + 1) * HEAD_DIM)
                qx_ref[h, 0:LANES, d * tile:(d + 1) * tile] = (
                    jnp.where(mine, qT, 0.0).astype(BF16))

    def reset_state():
        m_ref[...] = jnp.full_like(m_ref, MASKED)
        acc_ref[...] = jnp.zeros_like(acc_ref)

    def first_block(qi):
        t0 = n_tiles
        for d in range(sub_q):
            for h in heads:
                t0 = jnp.minimum(t0, first_ref[(bi * n_tiles + qi * sub_q + d) * N_HEADS
                                               + pair * HEADS_PER_STEP + h])
        j0 = jnp.minimum(t0 * (tile // tkv), qi * n_diag - 2)
        return 2 * lax.shift_right_logical(j0, 1)

    def units_for(d):
        out = []
        for h in heads:
            for c in range(tq // cw):
                if d is not None and (c + 1) * cw <= d * tkv:
                    continue
                masked = d is not None and c * cw < (d + 1) * tkv - 1
                out.append((h, c, krow + (d * tkv - c * cw) <= qcol if masked else None))
        return out

    def qk_unit(j, buf, unit):
        h, c, mask = unit
        cols = slice(c * cw, (c + 1) * cw)
        start = pl.multiple_of(j * tkv, tkv)
        s = _dot(kx_ref[0, pl.ds(start, tkv), :], qx_ref[h, :, cols])
        if mask is not None:
            s = jnp.where(mask, s, MASKED)
        s_bufs[buf][h, :, cols] = s
        mx_ref[buf, h, :, cols] = jnp.max(s, axis=0, keepdims=True)

    def softmax_unit(buf, unit):
        h, c, _ = unit
        cols = slice(c * cw, (c + 1) * cw)
        m_old = m_ref[h, :, cols]
        m_new = jnp.maximum(m_old, mx_ref[buf, h, :, cols])
        m_ref[h, :, cols] = m_new
        p = jnp.exp2(s_bufs[buf][h, :, cols] - m_new).astype(BF16)
        return p, jnp.exp2(m_old - m_new)

    def pv_unit(j, unit, p, alpha):
        h, c, _ = unit
        cols = slice(c * cw, (c + 1) * cw)
        vx = jnp.concatenate(
            [vT_ref[0, j, h * HEAD_DIM:(h + 1) * HEAD_DIM, :], ones_rows], axis=0)
        acc_ref[h, :, cols] = alpha * acc_ref[h, :, cols] + _dot(vx, p)

    def step(j, buf, cur_d=None, next_d=None, next_j=None, before_next=None):
        cur = units_for(cur_d)
        nxt = units_for(next_d)
        next_j = j + 1 if next_j is None else next_j
        if before_next is not None:
            before_next()
        pending = None
        for i in range(max(len(cur), len(nxt))):
            if i < len(nxt):
                qk_unit(next_j, 1 - buf, nxt[i])
            if pending is not None:
                pv_unit(j, *pending)
                pending = None
            if i < len(cur):
                pending = (cur[i],) + softmax_unit(buf, cur[i])
        if pending is not None:
            pv_unit(j, *pending)

    def diagonal_and_finish(qi):
        n_full = qi * n_diag
        for d in range(n_diag - 1):
            step(n_full + d, d % 2, cur_d=d, next_d=d + 1)
        nq = jnp.minimum(qi + 1, n_q - 1)
        step(n_full + n_diag - 1, (n_diag - 1) % 2, cur_d=n_diag - 1, next_d=None,
             next_j=first_block(nq), before_next=lambda: build_qx(nq))
        outs = []
        for h in heads:
            a = acc_ref[h]
            outs.append(a[0:HEAD_DIM] / a[HEAD_DIM:HEAD_DIM + 1])
        o = jnp.concatenate(outs, axis=0)
        start = qi * tq if isinstance(qi, int) else pl.multiple_of(qi * tq, tq)
        o_ref[0, pl.ds(start, tq), :] = o.T.astype(BF16)
        reset_state()

    def full_pair(i, carry):
        step(2 * i, 0)
        step(2 * i + 1, 1)
        return carry

    def later_tile(qi, carry):
        n_full = qi * n_diag
        lax.fori_loop(lax.shift_right_logical(first_block(qi), 1),
                      qi * (n_diag // 2) - 1, full_pair, 0)
        step(n_full - 2, 0)
        step(n_full - 1, 1, next_d=0)
        diagonal_and_finish(qi)
        return carry

    rowx = lax.broadcasted_iota(jnp.int32, (LANES, tq), 0)
    for h in heads:
        ones = (rowx >= h * N_SPLIT) & (rowx < (h + 1) * N_SPLIT)
        qx_ref[h, LANES:PAIR_W, :] = jnp.where(ones, 1.0, 0.0).astype(BF16)
    reset_state()
    build_qx(0)
    for unit in units_for(0):
        qk_unit(0, 0, unit)
    diagonal_and_finish(0)
    lax.fori_loop(1, n_q, later_tile, 0)


def _attn_call(first, qT, kx, vT):
    b, s = kx.shape[0], kx.shape[1]
    tq, tkv = TQ, TKV
    n_pairs = N_HEADS // HEADS_PER_STEP
    n_tiles = s // TM_QKV
    kern = functools.partial(_attn_kernel, tq=tq, tkv=tkv, cw=CW, tile=TM_QKV,
                             n_tiles=n_tiles, n_q=s // tq)
    grid_spec = pltpu.PrefetchScalarGridSpec(
        num_scalar_prefetch=1,
        grid=(b, n_pairs),
        in_specs=[
            pl.BlockSpec((1, n_tiles, LANES, TM_QKV), lambda bi, p, first: (bi, 0, p, 0)),
            pl.BlockSpec((1, s, PAIR_W), lambda bi, p, first: (bi, 0, p)),
            pl.BlockSpec((1, s // tkv, LANES, tkv), lambda bi, p, first: (bi, 0, p, 0)),
        ],
        out_specs=pl.BlockSpec((1, s, LANES), lambda bi, p, first: (bi, 0, p)),
        scratch_shapes=[
            pltpu.VMEM((HEADS_PER_STEP, PAIR_W, tq), BF16),
            pltpu.VMEM((HEADS_PER_STEP, tkv, tq), F32),
            pltpu.VMEM((HEADS_PER_STEP, tkv, tq), F32),
            pltpu.VMEM((2, HEADS_PER_STEP, 1, tq), F32),
            pltpu.VMEM((HEADS_PER_STEP, 1, tq), F32),
            pltpu.VMEM((HEADS_PER_STEP, HEAD_DIM + BF16_ROWS, tq), F32),
        ],
    )
    return pl.pallas_call(
        kern,
        grid_spec=grid_spec,
        out_shape=jax.ShapeDtypeStruct((b, s, ATTN_WIDTH), BF16),
        compiler_params=pltpu.CompilerParams(
            dimension_semantics=("arbitrary", "arbitrary"),
            vmem_limit_bytes=VMEM_LIMIT),
        name="attn",
    )(first, qT, kx, vT)


def _mix_kernel(x_ref, yb_ref, gpre_ref, gpost_ref, wcg_ref, cw_ref, wb_ref, wo_ref,
                o_ref, ztail_ref, *, tm, sub, tiles_per_seq, d_model):
    parts = [slice(r * sub, (r + 1) * sub) for r in range(tm // sub)]
    xs = [x_ref[rows, :] for rows in parts]
    projs = []
    for x in xs:
        h = _rmsnorm(x, gpre_ref[...]).astype(BF16)
        projs.append(_dot_nt(h, wcg_ref[...]))

    first = pl.program_id(0) % tiles_per_seq == 0
    zh = jnp.where(first, 0.0, ztail_ref[...])
    branches = []
    for rows, proj in zip(parts, projs):
        gate_b = proj[:, 0:CONV_CH]
        z = proj[:, CONV_CH:2 * CONV_CH] * proj[:, 2 * CONV_CH:3 * CONV_CH]
        zz = jnp.concatenate([zh, z], axis=0)
        zh = z[sub - HALO:sub]
        conv = cw_ref[CONV_K - 1:CONV_K, :] * z
        for back in range(1, CONV_K):
            shifted = pltpu.roll(zz, back, 0)[HALO:]
            conv = conv + cw_ref[CONV_K - 1 - back:CONV_K - back, :] * shifted
        ya = (gate_b * conv).astype(BF16)
        branches.append((_dot(ya, wb_ref[0:CONV_CH, :]),
                         _dot(yb_ref[rows, :], wb_ref[CONV_CH:CONV_CH + ATTN_WIDTH, :])))
    ztail_ref[...] = zh

    g0 = 3 * CONV_CH
    ms = []
    for proj, (ya_d, yb_d) in zip(projs, branches):
        gc = jax.nn.sigmoid(proj[:, g0:g0 + d_model])
        ga = jax.nn.sigmoid(proj[:, g0 + d_model:g0 + 2 * d_model])
        merged = (gc * ya_d + ga * yb_d).astype(BF16)
        ms.append(_dot(merged, wo_ref[...]))
    for rows, x, m in zip(parts, xs, ms):
        o_ref[rows, :] = x + _rmsnorm(m, gpost_ref[...])


def _mix_call(x2, yb2, gpre, gpost, wcg, cw, wb, wo, seq_len):
    n, d = x2.shape
    tm = TM_MIX
    kern = functools.partial(_mix_kernel, tm=tm, sub=SUB_ROWS, tiles_per_seq=seq_len // tm,
                             d_model=d)
    return pl.pallas_call(
        kern,
        grid=(n // tm,),
        in_specs=[
            pl.BlockSpec((tm, d), lambda i: (i, 0)),
            pl.BlockSpec((tm, ATTN_WIDTH), lambda i: (i, 0)),
            _const_spec(gpre.shape), _const_spec(gpost.shape), _const_spec(wcg.shape),
            _const_spec(cw.shape), _const_spec(wb.shape), _const_spec(wo.shape),
        ],
        out_specs=pl.BlockSpec((tm, d), lambda i: (i, 0)),
        out_shape=jax.ShapeDtypeStruct((n, d), F32),
        scratch_shapes=[pltpu.VMEM((HALO, CONV_CH), F32)],
        compiler_params=pltpu.CompilerParams(
            dimension_semantics=("arbitrary",), vmem_limit_bytes=VMEM_LIMIT),
        name="mix",
    )(x2, yb2, gpre, gpost, wcg, cw, wb, wo)


def _ffn_kernel(x_ref, gpre_ref, gpost_ref, wgu_ref, wd_ref, o_ref, *, hidden, sub):
    parts = [slice(r * sub, (r + 1) * sub) for r in range(x_ref.shape[0] // sub)]
    xs = [x_ref[rows, :] for rows in parts]
    gus = []
    for x in xs:
        h = _rmsnorm(x, gpre_ref[...]).astype(BF16)
        gus.append(_dot(h, wgu_ref[...]))
    fs = []
    for gu in gus:
        g = gu[:, 0:hidden]
        u = gu[:, hidden:2 * hidden]
        a = (g * jax.nn.sigmoid(g) * u).astype(BF16)
        fs.append(_dot(a, wd_ref[...]))
    for rows, x, f in zip(parts, xs, fs):
        o_ref[rows, :] = x + _rmsnorm(f, gpost_ref[...])


def _ffn_call(x2, gpre, gpost, wgu, wd):
    n, d = x2.shape
    tm = TM_FFN
    hidden = wd.shape[0]
    kern = functools.partial(_ffn_kernel, hidden=hidden, sub=SUB_ROWS)
    return pl.pallas_call(
        kern,
        grid=(n // tm,),
        in_specs=[
            pl.BlockSpec((tm, d), lambda i: (i, 0)),
            _const_spec(gpre.shape), _const_spec(gpost.shape),
            _const_spec(wgu.shape), _const_spec(wd.shape),
        ],
        out_specs=pl.BlockSpec((tm, d), lambda i: (i, 0)),
        out_shape=jax.ShapeDtypeStruct((n, d), F32),
        compiler_params=pltpu.CompilerParams(
            dimension_semantics=("arbitrary",), vmem_limit_bytes=VMEM_LIMIT),
        name="ffn",
    )(x2, gpre, gpost, wgu, wd)


def _selection_matrix():
    n_pairs = N_HEADS // HEADS_PER_STEP
    sel = np.zeros((LANES, n_pairs * LANES), np.float32)
    for head in range(N_HEADS):
        for j in range(N_SPLIT):
            sel[j * N_HEADS + head, (head // HEADS_PER_STEP) * LANES
                + (head % HEADS_PER_STEP) * N_SPLIT + j] = 1.0
    return jnp.asarray(sel, BF16)


def _norm_indicator():
    ind = np.zeros((2 * ATTN_WIDTH, LANES), np.float32)
    for which in range(2):
        for col in range(ATTN_WIDTH):
            ind[which * ATTN_WIDTH + col, which * N_HEADS + col // HEAD_DIM] = 1.0
    return jnp.asarray(ind, BF16)


def kernel(x, norm_mix_pre, norm_mix_post, w_in, b_f, conv_w, w_branch, w_out,
           norm_ffn_pre, norm_ffn_post, w_gate_up, w_down):
    b, s, d = x.shape
    depth = w_in.shape[0]
    a = ATTN_WIDTH
    tri = jnp.asarray(np.tril(np.ones((TM_QKV, TM_QKV), np.float32)), BF16)
    sel = _selection_matrix()
    ind = _norm_indicator()
    scale = LOG2E / math.sqrt(HEAD_DIM)

    for l in range(depth):
        bfp = jnp.pad(b_f[l], (0, LANES - N_HEADS)).reshape(1, LANES)
        qT, kx, vT, first, wb, wo, wgu, wd, wcg = _qkv_call(
            x, norm_mix_pre[l].reshape(1, d), w_in[l].T, bfp, tri, sel, ind,
            [w_branch[l].reshape(CONV_CH + a, d), w_out[l], w_gate_up[l], w_down[l]], scale)
        yb = _attn_call(first[:, :, 0, 0:N_HEADS].reshape(-1), qT, kx, vT)
        x2 = _mix_call(
            x.reshape(b * s, d), yb.reshape(b * s, a),
            norm_mix_pre[l].reshape(1, d), norm_mix_post[l].reshape(1, d),
            wcg, conv_w[l], wb, wo, s)
        x2 = _ffn_call(x2, norm_ffn_pre[l].reshape(1, d), norm_ffn_post[l].reshape(1, d),
                       wgu, wd)
        x = x2.reshape(b, s, d)
    return x
```

```python
import functools
import math

import jax
import jax.numpy as jnp
import numpy as np
from jax import lax
from jax.experimental import pallas as pl
from jax.experimental.pallas import tpu as pltpu

F32 = jnp.float32
BF16 = jnp.bfloat16

RMS_EPS = 1e-6
N_HEADS = 8
HEAD_DIM = 64
ATTN_WIDTH = N_HEADS * HEAD_DIM
CONV_CH = 512
CONV_K = 3
LANES = 128
BF16_ROWS = 16
HEADS_PER_STEP = 2
PAIR_W = 2 * LANES
N_SPLIT = 3
MASKED = -1e30
LOG2E = math.log2(math.e)
PRUNE_GAP = 106.0 * LOG2E
NORM_SLACK = 4.0 * 1.04
VMEM_LIMIT = 60 * 1024 * 1024

TM_QKV = 512
TQ = 1024
TKV = 512
CW = 256
TM_MIX = 1024
TM_FFN = 1024
SUB_ROWS = 256
HALO = 8

NT_DIMS = (((1,), (1,)), ((), ()))


def _rmsnorm(x, g):
    ms = jnp.mean(x * x, axis=-1, keepdims=True)
    return (x * lax.rsqrt(ms + RMS_EPS)) * g


def _split3(v):
    parts = []
    r = v
    for _ in range(N_SPLIT):
        p = r.astype(BF16)
        parts.append(p)
        r = r - p.astype(F32)
    return parts


def _dot(a, b):
    return jnp.dot(a, b, preferred_element_type=F32)


def _dot_nt(a, b):
    return lax.dot_general(a, b, NT_DIMS, preferred_element_type=F32)


def _const_spec(shape):
    nd = len(shape)
    return pl.BlockSpec(shape, lambda *_: (0,) * nd, pipeline_mode=pl.Buffered(1))


def _qkv_kernel(*refs, tm, sub, tkv, n_cast, scale):
    (x_ref, g_ref, w_ref, bf_ref, tri_ref, ind_ref), refs = refs[:6], refs[6:]
    cast_in, refs = refs[:n_cast], refs[n_cast:]
    (qT_ref, kx_ref, vT_ref, first_ref), refs = refs[:4], refs[4:]
    cast_out, (w16_ref, carry_ref, kmax_ref, hist_ref) = refs[:n_cast], refs[n_cast:]
    i = pl.program_id(1)

    for src, dst in zip(cast_in, cast_out):
        dst[...] = src[...].astype(BF16)

    a = ATTN_WIDTH

    @pl.when((pl.program_id(0) == 0) & (i == 0))
    def _():
        w16_ref[0:a, :] = (w_ref[0:a, :] * scale).astype(BF16)
        w16_ref[a:, :] = w_ref[a:, :].astype(BF16)

    @pl.when(i == 0)
    def _():
        carry_ref[...] = jnp.zeros_like(carry_ref)
        kmax_ref[...] = jnp.zeros_like(kmax_ref)
        hist_ref[...] = jnp.zeros_like(hist_ref)

    parts = [slice(r * sub, (r + 1) * sub) for r in range(tm // sub)]
    head_lane = lax.broadcasted_iota(jnp.int32, (sub, LANES), 1) < N_HEADS

    def pack3(v):
        pieces = _split3(jnp.where(head_lane, v, 0.0))
        out = pieces[0].astype(F32)
        for j in range(1, N_SPLIT):
            out = out + pltpu.roll(pieces[j].astype(F32), j * N_HEADS, 1)
        return out.astype(BF16)

    hs, vfs = [], []
    for rows in parts:
        hs.append(_rmsnorm(x_ref[0, rows, :], g_ref[...]).astype(BF16))
        vfs.append(_dot_nt(hs[-1], w16_ref[2 * a:3 * a + LANES, :]))
    qs, css = [], []
    for rows, h, vf in zip(parts, hs, vfs):
        vT_ref[0, rows.start // tkv, :, rows.start % tkv:rows.start % tkv + sub] = (
            vf[:, 0:a].T.astype(BF16))
        f = vf[:, a:a + LANES] + bf_ref[...]
        logf = (jnp.minimum(f, 0.0) - jnp.log1p(jnp.exp(-jnp.abs(f)))) * LOG2E
        qf32 = _dot_nt(h, w16_ref[0:a, :])
        qs.append(qf32.astype(BF16))
        qT_ref[0, 0, :, rows] = qf32.T.astype(BF16)
        css.append(_dot(tri_ref[...], pack3(logf)))
    ks, norm2 = [], None
    for h, q in zip(hs, qs):
        ks.append(_dot_nt(h, w16_ref[a:2 * a, :]).astype(BF16))
        qf = q.astype(F32)
        kf = ks[-1].astype(F32)
        sq = jnp.concatenate([qf * qf, kf * kf], axis=1).astype(BF16)
        n2 = jnp.max(_dot(sq, ind_ref[...]), axis=0, keepdims=True)
        norm2 = n2 if norm2 is None else jnp.maximum(norm2, n2)

    c_prev = carry_ref[...]
    c_end = c_prev
    for rows, cs, k in zip(parts, css, ks):
        c = cs + c_end
        for j in range(1, N_SPLIT):
            c = c + pltpu.roll(cs, LANES - j * N_HEADS, 1)
        c_end = c[sub - 1:sub]
        bias = pack3(-c)
        for p in range(N_HEADS // HEADS_PER_STEP):
            kx_ref[0, rows, p * PAIR_W:p * PAIR_W + LANES] = k[:, p * LANES:(p + 1) * LANES]
            kx_ref[0, rows, p * PAIR_W + LANES:(p + 1) * PAIR_W] = bias
    carry_ref[...] = c_end

    kmax = jnp.maximum(kmax_ref[...], norm2)
    kmax_ref[...] = kmax
    qk2 = NORM_SLACK * norm2 * pltpu.roll(kmax, LANES - N_HEADS, 1)
    gap = hist_ref[...] - c_prev - PRUNE_GAP
    tile = lax.broadcasted_iota(jnp.int32, gap.shape, 0)
    skip = (tile < i) & (gap > 0.0) & (qk2 < gap * gap)
    i_f = i.astype(F32)
    kept = jnp.where(skip, i_f, jnp.minimum(tile.astype(F32), i_f))
    first_ref[0, 0] = jnp.min(kept, axis=0, keepdims=True).astype(jnp.int32)
    hist_ref[pl.ds(i, 1), :] = c_end


def _cast_rider_spec(shape, steps, n_tiles):
    rows, cols = shape
    share = 1
    while (rows * share) % (steps * BF16_ROWS) or steps % share:
        share += 1
    return pl.BlockSpec((rows * share // steps, cols),
                        lambda bi, i: ((bi * n_tiles + i) // share, 0))


def _qkv_call(x, g, wT, bfp, tri, ind, casts, scale):
    b, s, d = x.shape
    tm, tkv = TM_QKV, TKV
    n_pairs = N_HEADS // HEADS_PER_STEP
    n_tiles = s // tm
    steps = b * n_tiles
    own_rows = 3 * ATTN_WIDTH + LANES
    rest0 = 3 * ATTN_WIDTH + N_HEADS
    rest_rows = wT.shape[0] - rest0
    rest_blk = rest_rows // steps
    assert rest_rows % (steps * BF16_ROWS) == 0 and rest0 % 8 == 0
    kern = functools.partial(_qkv_kernel, tm=tm, sub=SUB_ROWS, tkv=tkv, n_cast=len(casts) + 1,
                             scale=scale)
    rider_specs = [_cast_rider_spec(c.shape, steps, n_tiles) for c in casts]
    rest_in = pl.BlockSpec((pl.Element(rest_blk), pl.Element(d)),
                           lambda bi, i: (pl.multiple_of(
                               rest0 + (bi * n_tiles + i) * rest_blk, 8), 0))
    rest_out = pl.BlockSpec((rest_blk, d), lambda bi, i: (bi * n_tiles + i, 0))
    return pl.pallas_call(
        kern,
        grid=(b, n_tiles),
        in_specs=[
            pl.BlockSpec((1, tm, d), lambda bi, i: (bi, i, 0)),
            _const_spec(g.shape),
            pl.BlockSpec((own_rows, d), lambda bi, i: (0, 0), pipeline_mode=pl.Buffered(1)),
            _const_spec(bfp.shape),
            _const_spec(tri.shape), _const_spec(ind.shape),
        ] + rider_specs + [rest_in],
        out_specs=[
            pl.BlockSpec((1, 1, ATTN_WIDTH, tm), lambda bi, i: (bi, i, 0, 0)),
            pl.BlockSpec((1, tm, n_pairs * PAIR_W), lambda bi, i: (bi, i, 0)),
            pl.BlockSpec((1, tm // tkv, ATTN_WIDTH, tkv), lambda bi, i: (bi, i, 0, 0)),
            pl.BlockSpec((1, 1, 1, LANES), lambda bi, i: (bi, i, 0, 0)),
        ] + rider_specs + [rest_out],
        out_shape=[
            jax.ShapeDtypeStruct((b, n_tiles, ATTN_WIDTH, tm), BF16),
            jax.ShapeDtypeStruct((b, s, n_pairs * PAIR_W), BF16),
            jax.ShapeDtypeStruct((b, s // tkv, ATTN_WIDTH, tkv), BF16),
            jax.ShapeDtypeStruct((b, n_tiles, 1, LANES), jnp.int32),
        ] + [jax.ShapeDtypeStruct(c.shape, BF16) for c in casts]
        + [jax.ShapeDtypeStruct((rest_rows, d), BF16)],
        scratch_shapes=[
            pltpu.VMEM((own_rows, d), BF16),
            pltpu.VMEM((1, LANES), F32),
            pltpu.VMEM((1, LANES), F32),
            pltpu.VMEM((n_tiles, LANES), F32),
        ],
        compiler_params=pltpu.CompilerParams(
            dimension_semantics=("arbitrary", "arbitrary"),
            vmem_limit_bytes=VMEM_LIMIT),
        name="qkv",
    )(x, g, wT, bfp, tri, ind, *casts, wT)


def _attn_kernel(first_ref, qT_ref, kx_ref, vT_ref, o_ref, qx_ref, s0_ref, s1_ref, mx_ref,
                 m_ref, acc_ref, *, tq, tkv, cw, tile, n_tiles, n_q):
    bi, pair = pl.program_id(0), pl.program_id(1)
    heads = range(HEADS_PER_STEP)
    n_diag = tq // tkv
    sub_q = tq // tile
    s_bufs = (s0_ref, s1_ref)
    ones_rows = jnp.ones((BF16_ROWS, tkv), BF16)
    krow = lax.broadcasted_iota(jnp.int32, (tkv, cw), 0)
    qcol = lax.broadcasted_iota(jnp.int32, (tkv, cw), 1)
    row = lax.broadcasted_iota(jnp.int32, (LANES, tile), 0)

    def build_qx(qi):
        for d in range(sub_q):
            qT = qT_ref[0, qi * sub_q + d].astype(F32)
            for h in heads:
                mine = (row >= h * HEAD_DIM) & (row < (h + 1) * HEAD_DIM)
                qx_ref[h, 0:LANES, d * tile:(d + 1) * tile] = (
                    jnp.where(mine, qT, 0.0).astype(BF16))

    def reset_state():
        m_ref[...] = jnp.full_like(m_ref, MASKED)
        acc_ref[...] = jnp.zeros_like(acc_ref)

    def first_block(qi):
        t0 = n_tiles
        for d in range(sub_q):
            for h in heads:
                t0 = jnp.minimum(t0, first_ref[(bi * n_tiles + qi * sub_q + d) * N_HEADS
                                               + pair * HEADS_PER_STEP + h])
        j0 = jnp.minimum(t0 * (tile // tkv), qi * n_diag - 2)
        return 2 * lax.shift_right_logical(j0, 1)

    def units_for(d):
        out = []
        for h in heads:
            for c in range(tq // cw):
                if d is not None and (c + 1) * cw <= d * tkv:
                    continue
                masked = d is not None and c * cw < (d + 1) * tkv - 1
                out.append((h, c, krow + (d * tkv - c * cw) <= qcol if masked else None))
        return out

    def qk_unit(j, buf, unit):
        h, c, mask = unit
        cols = slice(c * cw, (c + 1) * cw)
        start = pl.multiple_of(j * tkv, tkv)
        s = _dot(kx_ref[0, pl.ds(start, tkv), :], qx_ref[h, :, cols])
        if mask is not None:
            s = jnp.where(mask, s, MASKED)
        s_bufs[buf][h, :, cols] = s
        mx_ref[buf, h, :, cols] = jnp.max(s, axis=0, keepdims=True)

    def softmax_unit(buf, unit):
        h, c, _ = unit
        cols = slice(c * cw, (c + 1) * cw)
        m_old = m_ref[h, :, cols]
        m_new = jnp.maximum(m_old, mx_ref[buf, h, :, cols])
        m_ref[h, :, cols] = m_new
        p = jnp.exp2(s_bufs[buf][h, :, cols] - m_new).astype(BF16)
        return p, jnp.exp2(m_old - m_new)

    def pv_unit(j, unit, p, alpha):
        h, c, _ = unit
        cols = slice(c * cw, (c + 1) * cw)
        vx = jnp.concatenate(
            [vT_ref[0, j, h * HEAD_DIM:(h + 1) * HEAD_DIM, :], ones_rows], axis=0)
        acc_ref[h, :, cols] = alpha * acc_ref[h, :, cols] + _dot(vx, p)

    def step(j, buf, cur_d=None, next_d=None, next_j=None, before_next=None):
        cur = units_for(cur_d)
        nxt = units_for(next_d)
        next_j = j + 1 if next_j is None else next_j
        if before_next is not None:
            before_next()
        pending = None
        for i in range(max(len(cur), len(nxt))):
            if i < len(nxt):
                qk_unit(next_j, 1 - buf, nxt[i])
            if pending is not None:
                pv_unit(j, *pending)
                pending = None
            if i < len(cur):
                pending = (cur[i],) + softmax_unit(buf, cur[i])
        if pending is not None:
            pv_unit(j, *pending)

    def diagonal_and_finish(qi):
        n_full = qi * n_diag
        for d in range(n_diag - 1):
            step(n_full + d, d % 2, cur_d=d, next_d=d + 1)
        nq = jnp.minimum(qi + 1, n_q - 1)
        step(n_full + n_diag - 1, (n_diag - 1) % 2, cur_d=n_diag - 1, next_d=None,
             next_j=first_block(nq), before_next=lambda: build_qx(nq))
        outs = []
        for h in heads:
            a = acc_ref[h]
            outs.append(a[0:HEAD_DIM] / a[HEAD_DIM:HEAD_DIM + 1])
        o = jnp.concatenate(outs, axis=0)
        start = qi * tq if isinstance(qi, int) else pl.multiple_of(qi * tq, tq)
        o_ref[0, pl.ds(start, tq), :] = o.T.astype(BF16)
        reset_state()

    def full_pair(i, carry):
        step(2 * i, 0)
        step(2 * i + 1, 1)
        return carry

    def later_tile(qi, carry):
        n_full = qi * n_diag
        lax.fori_loop(lax.shift_right_logical(first_block(qi), 1),
                      qi * (n_diag // 2) - 1, full_pair, 0)
        step(n_full - 2, 0)
        step(n_full - 1, 1, next_d=0)
        diagonal_and_finish(qi)
        return carry

    rowx = lax.broadcasted_iota(jnp.int32, (LANES, tq), 0)
    for h in heads:
        head = pair * HEADS_PER_STEP + h
        ones = rowx == head
        for j in range(1, N_SPLIT):
            ones = ones | (rowx == head + j * N_HEADS)
        qx_ref[h, LANES:PAIR_W, :] = jnp.where(ones, 1.0, 0.0).astype(BF16)
    reset_state()
    build_qx(0)
    for unit in units_for(0):
        qk_unit(0, 0, unit)
    diagonal_and_finish(0)
    lax.fori_loop(1, n_q, later_tile, 0)


def _attn_call(first, qT, kx, vT):
    b, s = kx.shape[0], kx.shape[1]
    tq, tkv = TQ, TKV
    n_pairs = N_HEADS // HEADS_PER_STEP
    n_tiles = s // TM_QKV
    kern = functools.partial(_attn_kernel, tq=tq, tkv=tkv, cw=CW, tile=TM_QKV,
                             n_tiles=n_tiles, n_q=s // tq)
    grid_spec = pltpu.PrefetchScalarGridSpec(
        num_scalar_prefetch=1,
        grid=(b, n_pairs),
        in_specs=[
            pl.BlockSpec((1, n_tiles, LANES, TM_QKV), lambda bi, p, first: (bi, 0, p, 0)),
            pl.BlockSpec((1, s, PAIR_W), lambda bi, p, first: (bi, 0, p)),
            pl.BlockSpec((1, s // tkv, LANES, tkv), lambda bi, p, first: (bi, 0, p, 0)),
        ],
        out_specs=pl.BlockSpec((1, s, LANES), lambda bi, p, first: (bi, 0, p)),
        scratch_shapes=[
            pltpu.VMEM((HEADS_PER_STEP, PAIR_W, tq), BF16),
            pltpu.VMEM((HEADS_PER_STEP, tkv, tq), F32),
            pltpu.VMEM((HEADS_PER_STEP, tkv, tq), F32),
            pltpu.VMEM((2, HEADS_PER_STEP, 1, tq), F32),
            pltpu.VMEM((HEADS_PER_STEP, 1, tq), F32),
            pltpu.VMEM((HEADS_PER_STEP, HEAD_DIM + BF16_ROWS, tq), F32),
        ],
    )
    return pl.pallas_call(
        kern,
        grid_spec=grid_spec,
        out_shape=jax.ShapeDtypeStruct((b, s, ATTN_WIDTH), BF16),
        compiler_params=pltpu.CompilerParams(
            dimension_semantics=("arbitrary", "arbitrary"),
            vmem_limit_bytes=VMEM_LIMIT),
        name="attn",
    )(first, qT, kx, vT)


def _mix_kernel(x_ref, yb_ref, gpre_ref, gpost_ref, wcg_ref, cw_ref, wb_ref, wo_ref,
                o_ref, ztail_ref, *, tm, sub, tiles_per_seq, d_model):
    parts = [slice(r * sub, (r + 1) * sub) for r in range(tm // sub)]
    xs = [x_ref[rows, :] for rows in parts]
    projs = []
    for x in xs:
        h = _rmsnorm(x, gpre_ref[...]).astype(BF16)
        projs.append(_dot_nt(h, wcg_ref[...]))

    first = pl.program_id(0) % tiles_per_seq == 0
    zh = jnp.where(first, 0.0, ztail_ref[...])
    branches = []
    for rows, proj in zip(parts, projs):
        gate_b = proj[:, 0:CONV_CH]
        z = proj[:, CONV_CH:2 * CONV_CH] * proj[:, 2 * CONV_CH:3 * CONV_CH]
        zz = jnp.concatenate([zh, z], axis=0)
        zh = z[sub - HALO:sub]
        conv = cw_ref[CONV_K - 1:CONV_K, :] * z
        for back in range(1, CONV_K):
            shifted = pltpu.roll(zz, back, 0)[HALO:]
            conv = conv + cw_ref[CONV_K - 1 - back:CONV_K - back, :] * shifted
        ya = (gate_b * conv).astype(BF16)
        branches.append((_dot(ya, wb_ref[0:CONV_CH, :]),
                         _dot(yb_ref[rows, :], wb_ref[CONV_CH:CONV_CH + ATTN_WIDTH, :])))
    ztail_ref[...] = zh

    g0 = 3 * CONV_CH
    ms = []
    for proj, (ya_d, yb_d) in zip(projs, branches):
        gc = jax.nn.sigmoid(proj[:, g0:g0 + d_model])
        ga = jax.nn.sigmoid(proj[:, g0 + d_model:g0 + 2 * d_model])
        merged = (gc * ya_d + ga * yb_d).astype(BF16)
        ms.append(_dot(merged, wo_ref[...]))
    for rows, x, m in zip(parts, xs, ms):
        o_ref[rows, :] = x + _rmsnorm(m, gpost_ref[...])


def _mix_call(x2, yb2, gpre, gpost, wcg, cw, wb, wo, seq_len):
    n, d = x2.shape
    tm = TM_MIX
    kern = functools.partial(_mix_kernel, tm=tm, sub=SUB_ROWS, tiles_per_seq=seq_len // tm,
                             d_model=d)
    return pl.pallas_call(
        kern,
        grid=(n // tm,),
        in_specs=[
            pl.BlockSpec((tm, d), lambda i: (i, 0)),
            pl.BlockSpec((tm, ATTN_WIDTH), lambda i: (i, 0)),
            _const_spec(gpre.shape), _const_spec(gpost.shape), _const_spec(wcg.shape),
            _const_spec(cw.shape), _const_spec(wb.shape), _const_spec(wo.shape),
        ],
        out_specs=pl.BlockSpec((tm, d), lambda i: (i, 0)),
        out_shape=jax.ShapeDtypeStruct((n, d), F32),
        scratch_shapes=[pltpu.VMEM((HALO, CONV_CH), F32)],
        compiler_params=pltpu.CompilerParams(
            dimension_semantics=("arbitrary",), vmem_limit_bytes=VMEM_LIMIT),
        name="mix",
    )(x2, yb2, gpre, gpost, wcg, cw, wb, wo)


def _ffn_kernel(x_ref, gpre_ref, gpost_ref, wgu_ref, wd_ref, o_ref, *, hidden, sub):
    parts = [slice(r * sub, (r + 1) * sub) for r in range(x_ref.shape[0] // sub)]
    xs = [x_ref[rows, :] for rows in parts]
    gus = []
    for x in xs:
        h = _rmsnorm(x, gpre_ref[...]).astype(BF16)
        gus.append(_dot(h, wgu_ref[...]))
    fs = []
    for gu in gus:
        g = gu[:, 0:hidden]
        u = gu[:, hidden:2 * hidden]
        a = (g * jax.nn.sigmoid(g) * u).astype(BF16)
        fs.append(_dot(a, wd_ref[...]))
    for rows, x, f in zip(parts, xs, fs):
        o_ref[rows, :] = x + _rmsnorm(f, gpost_ref[...])


def _ffn_call(x2, gpre, gpost, wgu, wd):
    n, d = x2.shape
    tm = TM_FFN
    hidden = wd.shape[0]
    kern = functools.partial(_ffn_kernel, hidden=hidden, sub=SUB_ROWS)
    return pl.pallas_call(
        kern,
        grid=(n // tm,),
        in_specs=[
            pl.BlockSpec((tm, d), lambda i: (i, 0)),
            _const_spec(gpre.shape), _const_spec(gpost.shape),
            _const_spec(wgu.shape), _const_spec(wd.shape),
        ],
        out_specs=pl.BlockSpec((tm, d), lambda i: (i, 0)),
        out_shape=jax.ShapeDtypeStruct((n, d), F32),
        compiler_params=pltpu.CompilerParams(
            dimension_semantics=("arbitrary",), vmem_limit_bytes=VMEM_LIMIT),
        name="ffn",
    )(x2, gpre, gpost, wgu, wd)


def _norm_indicator():
    ind = np.zeros((2 * ATTN_WIDTH, LANES), np.float32)
    for which in range(2):
        for col in range(ATTN_WIDTH):
            ind[which * ATTN_WIDTH + col, which * N_HEADS + col // HEAD_DIM] = 1.0
    return jnp.asarray(ind, BF16)


def kernel(x, norm_mix_pre, norm_mix_post, w_in, b_f, conv_w, w_branch, w_out,
           norm_ffn_pre, norm_ffn_post, w_gate_up, w_down):
    b, s, d = x.shape
    depth = w_in.shape[0]
    a = ATTN_WIDTH
    tri = jnp.asarray(np.tril(np.ones((SUB_ROWS, SUB_ROWS), np.float32)), BF16)
    ind = _norm_indicator()
    scale = LOG2E / math.sqrt(HEAD_DIM)

    for l in range(depth):
        bfp = jnp.pad(b_f[l], (0, LANES - N_HEADS)).reshape(1, LANES)
        qT, kx, vT, first, wb, wo, wgu, wd, wcg = _qkv_call(
            x, norm_mix_pre[l].reshape(1, d), w_in[l].T, bfp, tri, ind,
            [w_branch[l].reshape(CONV_CH + a, d), w_out[l], w_gate_up[l], w_down[l]], scale)
        yb = _attn_call(first[:, :, 0, 0:N_HEADS].reshape(-1), qT, kx, vT)
        x2 = _mix_call(
            x.reshape(b * s, d), yb.reshape(b * s, a),
            norm_mix_pre[l].reshape(1, d), norm_mix_post[l].reshape(1, d),
            wcg, conv_w[l], wb, wo, s)
        x2 = _ffn_call(x2, norm_ffn_pre[l].reshape(1, d), norm_ffn_post[l].reshape(1, d),
                       wgu, wd)
        x = x2.reshape(b, s, d)
    return x
```

```python
import functools
import math

import jax
import jax.numpy as jnp
import numpy as np
from jax import lax
from jax.experimental import pallas as pl
from jax.experimental.pallas import tpu as pltpu

F32 = jnp.float32
BF16 = jnp.bfloat16

RMS_EPS = 1e-6
N_HEADS = 8
HEAD_DIM = 64
ATTN_WIDTH = N_HEADS * HEAD_DIM
CONV_CH = 512
CONV_K = 3
LANES = 128
BF16_ROWS = 16
HEADS_PER_STEP = 2
PAIR_W = 2 * LANES
N_SPLIT = 3
MASKED = -1e30
LOG2E = math.log2(math.e)
PRUNE_GAP = 106.0 * LOG2E
NORM_SLACK = 4.0 * 1.04
VMEM_LIMIT = 60 * 1024 * 1024

TM_QKV = 512
TQ = 1024
TKV = 512
CW = 256
TM_MIX = 1024
TM_FFN = 1024
SUB_ROWS = 256
RIDER_SHARE = 4
HALO = 8

NT_DIMS = (((1,), (1,)), ((), ()))


def _rmsnorm(x, g):
    ms = jnp.mean(x * x, axis=-1, keepdims=True)
    return (x * lax.rsqrt(ms + RMS_EPS)) * g


def _split3(v):
    parts = []
    r = v
    for _ in range(N_SPLIT):
        p = r.astype(BF16)
        parts.append(p)
        r = r - p.astype(F32)
    return parts


def _dot(a, b):
    return jnp.dot(a, b, preferred_element_type=F32)


def _dot_nt(a, b):
    return lax.dot_general(a, b, NT_DIMS, preferred_element_type=F32)


def _const_spec(shape):
    nd = len(shape)
    return pl.BlockSpec(shape, lambda *_: (0,) * nd, pipeline_mode=pl.Buffered(1))


def _qkv_kernel(*refs, tm, sub, tkv, n_cast, scale):
    (x_ref, g_ref, w_ref, bf_ref, tri_ref, ind_ref), refs = refs[:6], refs[6:]
    cast_in, refs = refs[:n_cast], refs[n_cast:]
    (qT_ref, kx_ref, vT_ref, first_ref), refs = refs[:4], refs[4:]
    cast_out, (w16_ref, carry_ref, kmax_ref, hist_ref) = refs[:n_cast], refs[n_cast:]
    i = pl.program_id(1)

    @pl.when((pl.program_id(0) * pl.num_programs(1) + i) % RIDER_SHARE == 0)
    def _():
        for src, dst in zip(cast_in, cast_out):
            dst[...] = src[...].astype(BF16)

    a = ATTN_WIDTH

    @pl.when((pl.program_id(0) == 0) & (i == 0))
    def _():
        w16_ref[0:a, :] = (w_ref[0:a, :] * scale).astype(BF16)
        w16_ref[a:, :] = w_ref[a:, :].astype(BF16)

    @pl.when(i == 0)
    def _():
        carry_ref[...] = jnp.zeros_like(carry_ref)
        kmax_ref[...] = jnp.zeros_like(kmax_ref)
        hist_ref[...] = jnp.zeros_like(hist_ref)

    parts = [slice(r * sub, (r + 1) * sub) for r in range(tm // sub)]
    head_lane = lax.broadcasted_iota(jnp.int32, (sub, LANES), 1) < N_HEADS

    def pack3(v):
        pieces = _split3(jnp.where(head_lane, v, 0.0))
        out = pieces[0].astype(F32)
        for j in range(1, N_SPLIT):
            out = out + pltpu.roll(pieces[j].astype(F32), j * N_HEADS, 1)
        return out.astype(BF16)

    hs, vfs = [], []
    for rows in parts:
        hs.append(_rmsnorm(x_ref[0, rows, :], g_ref[...]).astype(BF16))
        vfs.append(_dot_nt(hs[-1], w16_ref[2 * a:3 * a + LANES, :]))
    qs, css = [], []
    for rows, h, vf in zip(parts, hs, vfs):
        vT_ref[0, rows.start // tkv, :, rows.start % tkv:rows.start % tkv + sub] = (
            vf[:, 0:a].T.astype(BF16))
        f = vf[:, a:a + LANES] + bf_ref[...]
        logf = (jnp.minimum(f, 0.0) - jnp.log1p(jnp.exp(-jnp.abs(f)))) * LOG2E
        qf32 = _dot_nt(h, w16_ref[0:a, :])
        qs.append(qf32.astype(BF16))
        qT_ref[0, 0, :, rows] = qf32.T.astype(BF16)
        css.append(_dot(tri_ref[...], pack3(logf)))
    ks, norm2 = [], None
    for h, q in zip(hs, qs):
        ks.append(_dot_nt(h, w16_ref[a:2 * a, :]).astype(BF16))
        qf = q.astype(F32)
        kf = ks[-1].astype(F32)
        sq = jnp.concatenate([qf * qf, kf * kf], axis=1).astype(BF16)
        n2 = jnp.max(_dot(sq, ind_ref[...]), axis=0, keepdims=True)
        norm2 = n2 if norm2 is None else jnp.maximum(norm2, n2)

    c_prev = carry_ref[...]
    c_end = c_prev
    for rows, cs, k in zip(parts, css, ks):
        c = cs + c_end
        for j in range(1, N_SPLIT):
            c = c + pltpu.roll(cs, LANES - j * N_HEADS, 1)
        c_end = c[sub - 1:sub]
        bias = pack3(-c)
        for p in range(N_HEADS // HEADS_PER_STEP):
            kx_ref[0, rows, p * PAIR_W:p * PAIR_W + LANES] = k[:, p * LANES:(p + 1) * LANES]
            kx_ref[0, rows, p * PAIR_W + LANES:(p + 1) * PAIR_W] = bias
    carry_ref[...] = c_end

    kmax = jnp.maximum(kmax_ref[...], norm2)
    kmax_ref[...] = kmax
    qk2 = NORM_SLACK * norm2 * pltpu.roll(kmax, LANES - N_HEADS, 1)
    gap = hist_ref[...] - c_prev - PRUNE_GAP
    tile = lax.broadcasted_iota(jnp.int32, gap.shape, 0)
    skip = (tile < i) & (gap > 0.0) & (qk2 < gap * gap)
    i_f = i.astype(F32)
    kept = jnp.where(skip, i_f, jnp.minimum(tile.astype(F32), i_f))
    first_ref[0, 0] = jnp.min(kept, axis=0, keepdims=True).astype(jnp.int32)
    hist_ref[pl.ds(i, 1), :] = c_end


def _cast_rider_spec(shape, steps, n_tiles):
    rows, cols = shape
    share = RIDER_SHARE
    assert (rows * share) % (steps * BF16_ROWS) == 0 and steps % share == 0
    return pl.BlockSpec((rows * share // steps, cols),
                        lambda bi, i: ((bi * n_tiles + i) // share, 0))


def _qkv_call(x, g, wT, bfp, tri, ind, casts, scale):
    b, s, d = x.shape
    tm, tkv = TM_QKV, TKV
    n_pairs = N_HEADS // HEADS_PER_STEP
    n_tiles = s // tm
    steps = b * n_tiles
    own_rows = 3 * ATTN_WIDTH + LANES
    rest0 = 3 * ATTN_WIDTH + N_HEADS
    rest_rows = wT.shape[0] - rest0
    rest_blk = rest_rows * RIDER_SHARE // steps
    assert (rest_rows * RIDER_SHARE) % (steps * BF16_ROWS) == 0 and rest0 % 8 == 0
    kern = functools.partial(_qkv_kernel, tm=tm, sub=SUB_ROWS, tkv=tkv, n_cast=len(casts) + 1,
                             scale=scale)
    rider_specs = [_cast_rider_spec(c.shape, steps, n_tiles) for c in casts]
    rest_in = pl.BlockSpec((pl.Element(rest_blk), pl.Element(d)),
                           lambda bi, i: (pl.multiple_of(
                               rest0 + (bi * n_tiles + i) // RIDER_SHARE * rest_blk, 8), 0))
    rest_out = pl.BlockSpec((rest_blk, d), lambda bi, i: ((bi * n_tiles + i) // RIDER_SHARE, 0))
    return pl.pallas_call(
        kern,
        grid=(b, n_tiles),
        in_specs=[
            pl.BlockSpec((1, tm, d), lambda bi, i: (bi, i, 0)),
            _const_spec(g.shape),
            pl.BlockSpec((own_rows, d), lambda bi, i: (0, 0), pipeline_mode=pl.Buffered(1)),
            _const_spec(bfp.shape),
            _const_spec(tri.shape), _const_spec(ind.shape),
        ] + rider_specs + [rest_in],
        out_specs=[
            pl.BlockSpec((1, 1, ATTN_WIDTH, tm), lambda bi, i: (bi, i, 0, 0)),
            pl.BlockSpec((1, tm, n_pairs * PAIR_W), lambda bi, i: (bi, i, 0)),
            pl.BlockSpec((1, tm // tkv, ATTN_WIDTH, tkv), lambda bi, i: (bi, i, 0, 0)),
            pl.BlockSpec((1, 1, 1, LANES), lambda bi, i: (bi, i, 0, 0)),
        ] + rider_specs + [rest_out],
        out_shape=[
            jax.ShapeDtypeStruct((b, n_tiles, ATTN_WIDTH, tm), BF16),
            jax.ShapeDtypeStruct((b, s, n_pairs * PAIR_W), BF16),
            jax.ShapeDtypeStruct((b, s // tkv, ATTN_WIDTH, tkv), BF16),
            jax.ShapeDtypeStruct((b, n_tiles, 1, LANES), jnp.int32),
        ] + [jax.ShapeDtypeStruct(c.shape, BF16) for c in casts]
        + [jax.ShapeDtypeStruct((rest_rows, d), BF16)],
        scratch_shapes=[
            pltpu.VMEM((own_rows, d), BF16),
            pltpu.VMEM((1, LANES), F32),
            pltpu.VMEM((1, LANES), F32),
            pltpu.VMEM((n_tiles, LANES), F32),
        ],
        compiler_params=pltpu.CompilerParams(
            dimension_semantics=("arbitrary", "arbitrary"),
            vmem_limit_bytes=VMEM_LIMIT),
        name="qkv",
    )(x, g, wT, bfp, tri, ind, *casts, wT)


def _attn_kernel(first_ref, qT_ref, kx_ref, vT_ref, o_ref, qx_ref, s0_ref, s1_ref, mx_ref,
                 m_ref, acc_ref, *, tq, tkv, cw, tile, n_tiles, n_q):
    bi, pair = pl.program_id(0), pl.program_id(1)
    heads = range(HEADS_PER_STEP)
    n_diag = tq // tkv
    sub_q = tq // tile
    s_bufs = (s0_ref, s1_ref)
    ones_rows = jnp.ones((BF16_ROWS, tkv), BF16)
    krow = lax.broadcasted_iota(jnp.int32, (tkv, cw), 0)
    qcol = lax.broadcasted_iota(jnp.int32, (tkv, cw), 1)
    row = lax.broadcasted_iota(jnp.int32, (LANES, tile), 0)

    def build_qx(qi):
        for d in range(sub_q):
            qT = qT_ref[0, qi * sub_q + d].astype(F32)
            for h in heads:
                mine = (row >= h * HEAD_DIM) & (row < (h + 1) * HEAD_DIM)
                qx_ref[h, 0:LANES, d * tile:(d + 1) * tile] = (
                    jnp.where(mine, qT, 0.0).astype(BF16))

    def reset_state():
        m_ref[...] = jnp.full_like(m_ref, MASKED)
        acc_ref[...] = jnp.zeros_like(acc_ref)

    def first_block(qi):
        t0 = n_tiles
        for d in range(sub_q):
            for h in heads:
                t0 = jnp.minimum(t0, first_ref[(bi * n_tiles + qi * sub_q + d) * N_HEADS
                                               + pair * HEADS_PER_STEP + h])
        j0 = jnp.minimum(t0 * (tile // tkv), qi * n_diag - 2)
        return 2 * lax.shift_right_logical(j0, 1)

    def units_for(d):
        out = []
        for h in heads:
            for c in range(tq // cw):
                if d is not None and (c + 1) * cw <= d * tkv:
                    continue
                masked = d is not None and c * cw < (d + 1) * tkv - 1
                out.append((h, c, krow + (d * tkv - c * cw) <= qcol if masked else None))
        return out

    def qk_unit(j, buf, unit):
        h, c, mask = unit
        cols = slice(c * cw, (c + 1) * cw)
        start = pl.multiple_of(j * tkv, tkv)
        s = _dot(kx_ref[0, pl.ds(start, tkv), :], qx_ref[h, :, cols])
        if mask is not None:
            s = jnp.where(mask, s, MASKED)
        s_bufs[buf][h, :, cols] = s
        mx_ref[buf, h, :, cols] = jnp.max(s, axis=0, keepdims=True)

    def softmax_unit(buf, unit):
        h, c, _ = unit
        cols = slice(c * cw, (c + 1) * cw)
        m_old = m_ref[h, :, cols]
        m_new = jnp.maximum(m_old, mx_ref[buf, h, :, cols])
        m_ref[h, :, cols] = m_new
        p = jnp.exp2(s_bufs[buf][h, :, cols] - m_new).astype(BF16)
        return p, jnp.exp2(m_old - m_new)

    def pv_unit(j, unit, p, alpha):
        h, c, _ = unit
        cols = slice(c * cw, (c + 1) * cw)
        vx = jnp.concatenate(
            [vT_ref[0, j, h * HEAD_DIM:(h + 1) * HEAD_DIM, :], ones_rows], axis=0)
        acc_ref[h, :, cols] = alpha * acc_ref[h, :, cols] + _dot(vx, p)

    def step(j, buf, cur_d=None, next_d=None, next_j=None, before_next=None):
        cur = units_for(cur_d)
        nxt = units_for(next_d)
        next_j = j + 1 if next_j is None else next_j
        if before_next is not None:
            before_next()
        pending = None
        for i in range(max(len(cur), len(nxt))):
            if i < len(nxt):
                qk_unit(next_j, 1 - buf, nxt[i])
            if pending is not None:
                pv_unit(j, *pending)
                pending = None
            if i < len(cur):
                pending = (cur[i],) + softmax_unit(buf, cur[i])
        if pending is not None:
            pv_unit(j, *pending)

    def diagonal_and_finish(qi):
        n_full = qi * n_diag
        for d in range(n_diag - 1):
            step(n_full + d, d % 2, cur_d=d, next_d=d + 1)
        nq = jnp.minimum(qi + 1, n_q - 1)
        step(n_full + n_diag - 1, (n_diag - 1) % 2, cur_d=n_diag - 1, next_d=None,
             next_j=first_block(nq), before_next=lambda: build_qx(nq))
        outs = []
        for h in heads:
            a = acc_ref[h]
            outs.append(a[0:HEAD_DIM] / a[HEAD_DIM:HEAD_DIM + 1])
        o = jnp.concatenate(outs, axis=0)
        start = qi * tq if isinstance(qi, int) else pl.multiple_of(qi * tq, tq)
        o_ref[0, pl.ds(start, tq), :] = o.T.astype(BF16)
        reset_state()

    def full_pair(i, carry):
        step(2 * i, 0)
        step(2 * i + 1, 1)
        return carry

    def later_tile(qi, carry):
        n_full = qi * n_diag
        lax.fori_loop(lax.shift_right_logical(first_block(qi), 1),
                      qi * (n_diag // 2) - 1, full_pair, 0)
        step(n_full - 2, 0)
        step(n_full - 1, 1, next_d=0)
        diagonal_and_finish(qi)
        return carry

    rowx = lax.broadcasted_iota(jnp.int32, (LANES, tq), 0)
    for h in heads:
        head = pair * HEADS_PER_STEP + h
        ones = rowx == head
        for j in range(1, N_SPLIT):
            ones = ones | (rowx == head + j * N_HEADS)
        qx_ref[h, LANES:PAIR_W, :] = jnp.where(ones, 1.0, 0.0).astype(BF16)
    reset_state()
    build_qx(0)
    for unit in units_for(0):
        qk_unit(0, 0, unit)
    diagonal_and_finish(0)
    lax.fori_loop(1, n_q, later_tile, 0)


def _attn_call(first, qT, kx, vT):
    b, s = kx.shape[0], kx.shape[1]
    tq, tkv = TQ, TKV
    n_pairs = N_HEADS // HEADS_PER_STEP
    n_tiles = s // TM_QKV
    kern = functools.partial(_attn_kernel, tq=tq, tkv=tkv, cw=CW, tile=TM_QKV,
                             n_tiles=n_tiles, n_q=s // tq)
    grid_spec = pltpu.PrefetchScalarGridSpec(
        num_scalar_prefetch=1,
        grid=(b, n_pairs),
        in_specs=[
            pl.BlockSpec((1, n_tiles, LANES, TM_QKV), lambda bi, p, first: (bi, 0, p, 0)),
            pl.BlockSpec((1, s, PAIR_W), lambda bi, p, first: (bi, 0, p)),
            pl.BlockSpec((1, s // tkv, LANES, tkv), lambda bi, p, first: (bi, 0, p, 0)),
        ],
        out_specs=pl.BlockSpec((1, s, LANES), lambda bi, p, first: (bi, 0, p)),
        scratch_shapes=[
            pltpu.VMEM((HEADS_PER_STEP, PAIR_W, tq), BF16),
            pltpu.VMEM((HEADS_PER_STEP, tkv, tq), F32),
            pltpu.VMEM((HEADS_PER_STEP, tkv, tq), F32),
            pltpu.VMEM((2, HEADS_PER_STEP, 1, tq), F32),
            pltpu.VMEM((HEADS_PER_STEP, 1, tq), F32),
            pltpu.VMEM((HEADS_PER_STEP, HEAD_DIM + BF16_ROWS, tq), F32),
        ],
    )
    return pl.pallas_call(
        kern,
        grid_spec=grid_spec,
        out_shape=jax.ShapeDtypeStruct((b, s, ATTN_WIDTH), BF16),
        compiler_params=pltpu.CompilerParams(
            dimension_semantics=("arbitrary", "arbitrary"),
            vmem_limit_bytes=VMEM_LIMIT),
        name="attn",
    )(first, qT, kx, vT)


def _mix_kernel(x_ref, yb_ref, gpre_ref, gpost_ref, wcg_ref, cw_ref, wb_ref, wo_ref,
                o_ref, ztail_ref, *, tm, sub, tiles_per_seq, d_model):
    parts = [slice(r * sub, (r + 1) * sub) for r in range(tm // sub)]
    xs = [x_ref[rows, :] for rows in parts]
    projs = []
    for x in xs:
        h = _rmsnorm(x, gpre_ref[...]).astype(BF16)
        projs.append(_dot_nt(h, wcg_ref[...]))

    first = pl.program_id(0) % tiles_per_seq == 0
    zh = jnp.where(first, 0.0, ztail_ref[...])
    branches = []
    for rows, proj in zip(parts, projs):
        gate_b = proj[:, 0:CONV_CH]
        z = proj[:, CONV_CH:2 * CONV_CH] * proj[:, 2 * CONV_CH:3 * CONV_CH]
        zz = jnp.concatenate([zh, z], axis=0)
        zh = z[sub - HALO:sub]
        conv = cw_ref[CONV_K - 1:CONV_K, :] * z
        for back in range(1, CONV_K):
            shifted = pltpu.roll(zz, back, 0)[HALO:]
            conv = conv + cw_ref[CONV_K - 1 - back:CONV_K - back, :] * shifted
        ya = (gate_b * conv).astype(BF16)
        branches.append((_dot(ya, wb_ref[0:CONV_CH, :]),
                         _dot(yb_ref[rows, :], wb_ref[CONV_CH:CONV_CH + ATTN_WIDTH, :])))
    ztail_ref[...] = zh

    g0 = 3 * CONV_CH
    ms = []
    for proj, (ya_d, yb_d) in zip(projs, branches):
        gc = jax.nn.sigmoid(proj[:, g0:g0 + d_model])
        ga = jax.nn.sigmoid(proj[:, g0 + d_model:g0 + 2 * d_model])
        merged = (gc * ya_d + ga * yb_d).astype(BF16)
        ms.append(_dot(merged, wo_ref[...]))
    for rows, x, m in zip(parts, xs, ms):
        o_ref[rows, :] = x + _rmsnorm(m, gpost_ref[...])


def _mix_call(x2, yb2, gpre, gpost, wcg, cw, wb, wo, seq_len):
    n, d = x2.shape
    tm = TM_MIX
    kern = functools.partial(_mix_kernel, tm=tm, sub=SUB_ROWS, tiles_per_seq=seq_len // tm,
                             d_model=d)
    return pl.pallas_call(
        kern,
        grid=(n // tm,),
        in_specs=[
            pl.BlockSpec((tm, d), lambda i: (i, 0)),
            pl.BlockSpec((tm, ATTN_WIDTH), lambda i: (i, 0)),
            _const_spec(gpre.shape), _const_spec(gpost.shape), _const_spec(wcg.shape),
            _const_spec(cw.shape), _const_spec(wb.shape), _const_spec(wo.shape),
        ],
        out_specs=pl.BlockSpec((tm, d), lambda i: (i, 0)),
        out_shape=jax.ShapeDtypeStruct((n, d), F32),
        scratch_shapes=[pltpu.VMEM((HALO, CONV_CH), F32)],
        compiler_params=pltpu.CompilerParams(
            dimension_semantics=("arbitrary",), vmem_limit_bytes=VMEM_LIMIT),
        name="mix",
    )(x2, yb2, gpre, gpost, wcg, cw, wb, wo)


def _ffn_kernel(x_ref, gpre_ref, gpost_ref, wgu_ref, wd_ref, o_ref, *, hidden, sub):
    parts = [slice(r * sub, (r + 1) * sub) for r in range(x_ref.shape[0] // sub)]
    xs = [x_ref[rows, :] for rows in parts]
    gus = []
    for x in xs:
        h = _rmsnorm(x, gpre_ref[...]).astype(BF16)
        gus.append(_dot(h, wgu_ref[...]))
    fs = []
    for gu in gus:
        g = gu[:, 0:hidden]
        u = gu[:, hidden:2 * hidden]
        a = (g * jax.nn.sigmoid(g) * u).astype(BF16)
        fs.append(_dot(a, wd_ref[...]))
    for rows, x, f in zip(parts, xs, fs):
        o_ref[rows, :] = x + _rmsnorm(f, gpost_ref[...])


def _ffn_call(x2, gpre, gpost, wgu, wd):
    n, d = x2.shape
    tm = TM_FFN
    hidden = wd.shape[0]
    kern = functools.partial(_ffn_kernel, hidden=hidden, sub=SUB_ROWS)
    return pl.pallas_call(
        kern,
        grid=(n // tm,),
        in_specs=[
            pl.BlockSpec((tm, d), lambda i: (i, 0)),
            _const_spec(gpre.shape), _const_spec(gpost.shape),
            _const_spec(wgu.shape), _const_spec(wd.shape),
        ],
        out_specs=pl.BlockSpec((tm, d), lambda i: (i, 0)),
        out_shape=jax.ShapeDtypeStruct((n, d), F32),
        compiler_params=pltpu.CompilerParams(
            dimension_semantics=("arbitrary",), vmem_limit_bytes=VMEM_LIMIT),
        name="ffn",
    )(x2, gpre, gpost, wgu, wd)


def _norm_indicator():
    ind = np.zeros((2 * ATTN_WIDTH, LANES), np.float32)
    for which in range(2):
        for col in range(ATTN_WIDTH):
            ind[which * ATTN_WIDTH + col, which * N_HEADS + col // HEAD_DIM] = 1.0
    return jnp.asarray(ind, BF16)


def kernel(x, norm_mix_pre, norm_mix_post, w_in, b_f, conv_w, w_branch, w_out,
           norm_ffn_pre, norm_ffn_post, w_gate_up, w_down):
    b, s, d = x.shape
    depth = w_in.shape[0]
    a = ATTN_WIDTH
    tri = jnp.asarray(np.tril(np.ones((SUB_ROWS, SUB_ROWS), np.float32)), BF16)
    ind = _norm_indicator()
    scale = LOG2E / math.sqrt(HEAD_DIM)

    for l in range(depth):
        bfp = jnp.pad(b_f[l], (0, LANES - N_HEADS)).reshape(1, LANES)
        qT, kx, vT, first, wb, wo, wgu, wd, wcg = _qkv_call(
            x, norm_mix_pre[l].reshape(1, d), w_in[l].T, bfp, tri, ind,
            [w_branch[l].reshape(CONV_CH + a, d), w_out[l], w_gate_up[l], w_down[l]], scale)
        yb = _attn_call(first[:, :, 0, 0:N_HEADS].reshape(-1), qT, kx, vT)
        x2 = _mix_call(
            x.reshape(b * s, d), yb.reshape(b * s, a),
            norm_mix_pre[l].reshape(1, d), norm_mix_post[l].reshape(1, d),
            wcg, conv_w[l], wb, wo, s)
        x2 = _ffn_call(x2, norm_ffn_pre[l].reshape(1, d), norm_ffn_post[l].reshape(1, d),
                       wgu, wd)
        x = x2.reshape(b, s, d)
    return x
```

```python
import functools
import math

import jax
import jax.numpy as jnp
import numpy as np
from jax import lax
from jax.experimental import pallas as pl
from jax.experimental.pallas import tpu as pltpu

F32 = jnp.float32
BF16 = jnp.bfloat16

RMS_EPS = 1e-6
N_HEADS = 8
HEAD_DIM = 64
ATTN_WIDTH = N_HEADS * HEAD_DIM
CONV_CH = 512
CONV_K = 3
LANES = 128
BF16_ROWS = 16
HEADS_PER_STEP = 2
PAIR_W = 2 * LANES
N_SPLIT = 3
MASKED = -1e30
LOG2E = math.log2(math.e)
PRUNE_GAP = 106.0 * LOG2E
NORM_SLACK = 4.0 * 1.04
VMEM_LIMIT = 60 * 1024 * 1024

TM_QKV = 512
TQ = 1024
TKV = 512
CW = 256
TM_MIX = 1024
TM_FFN = 1024
SUB_ROWS = 256
HALO = 8

NT_DIMS = (((1,), (1,)), ((), ()))


def _rmsnorm(x, g):
    ms = jnp.mean(x * x, axis=-1, keepdims=True)
    return (x * lax.rsqrt(ms + RMS_EPS)) * g


def _split3(v):
    parts = []
    r = v
    for _ in range(N_SPLIT):
        p = r.astype(BF16)
        parts.append(p)
        r = r - p.astype(F32)
    return parts


def _dot(a, b):
    return jnp.dot(a, b, preferred_element_type=F32)


def _dot_nt(a, b):
    return lax.dot_general(a, b, NT_DIMS, preferred_element_type=F32)


def _const_spec(shape):
    nd = len(shape)
    return pl.BlockSpec(shape, lambda *_: (0,) * nd, pipeline_mode=pl.Buffered(1))


def _qkv_kernel(*refs, tm, sub, tkv, n_cast, scale):
    (x_ref, g_ref, w_ref, bf_ref, tri_ref, ind_ref), refs = refs[:6], refs[6:]
    cast_in, refs = refs[:n_cast], refs[n_cast:]
    (qT_ref, kx_ref, vT_ref, first_ref), refs = refs[:4], refs[4:]
    cast_out, (w16_ref, carry_ref, kmax_ref, hist_ref) = refs[:n_cast], refs[n_cast:]
    i = pl.program_id(1)

    for src, dst in zip(cast_in, cast_out):
        dst[...] = src[...].astype(BF16)

    a = ATTN_WIDTH

    @pl.when((pl.program_id(0) == 0) & (i == 0))
    def _():
        w16_ref[0:a, :] = (w_ref[0:a, :] * scale).astype(BF16)
        w16_ref[a:, :] = w_ref[a:, :].astype(BF16)

    @pl.when(i == 0)
    def _():
        carry_ref[...] = jnp.zeros_like(carry_ref)
        kmax_ref[...] = jnp.zeros_like(kmax_ref)
        hist_ref[...] = jnp.zeros_like(hist_ref)

    parts = [slice(r * sub, (r + 1) * sub) for r in range(tm // sub)]
    head_lane = lax.broadcasted_iota(jnp.int32, (sub, LANES), 1) < N_HEADS

    def pack3(v):
        pieces = _split3(jnp.where(head_lane, v, 0.0))
        out = pieces[0].astype(F32)
        for j in range(1, N_SPLIT):
            out = out + pltpu.roll(pieces[j].astype(F32), j * N_HEADS, 1)
        return out.astype(BF16)

    hs, vfs = [], []
    for rows in parts:
        hs.append(_rmsnorm(x_ref[0, rows, :], g_ref[...]).astype(BF16))
        vfs.append(_dot_nt(hs[-1], w16_ref[2 * a:3 * a + LANES, :]))
    qs, css = [], []
    for rows, h, vf in zip(parts, hs, vfs):
        vT_ref[0, rows.start // tkv, :, rows.start % tkv:rows.start % tkv + sub] = (
            vf[:, 0:a].T.astype(BF16))
        f = vf[:, a:a + LANES] + bf_ref[...]
        logf = (jnp.minimum(f, 0.0) - jnp.log1p(jnp.exp(-jnp.abs(f)))) * LOG2E
        qf32 = _dot_nt(h, w16_ref[0:a, :])
        qs.append(qf32.astype(BF16))
        qT_ref[0, 0, :, rows] = qf32.T.astype(BF16)
        css.append(_dot(tri_ref[...], pack3(logf)))
    ks, norm2 = [], None
    for h, q in zip(hs, qs):
        ks.append(_dot_nt(h, w16_ref[a:2 * a, :]).astype(BF16))
        qf = q.astype(F32)
        kf = ks[-1].astype(F32)
        sq = jnp.concatenate([qf * qf, kf * kf], axis=1).astype(BF16)
        n2 = jnp.max(_dot(sq, ind_ref[...]), axis=0, keepdims=True)
        norm2 = n2 if norm2 is None else jnp.maximum(norm2, n2)

    c_prev = carry_ref[...]
    c_end = c_prev
    for rows, cs, k in zip(parts, css, ks):
        c = cs + c_end
        for j in range(1, N_SPLIT):
            c = c + pltpu.roll(cs, LANES - j * N_HEADS, 1)
        c_end = c[sub - 1:sub]
        bias = pack3(-c)
        for p in range(N_HEADS // HEADS_PER_STEP):
            kx_ref[0, rows, p * PAIR_W:p * PAIR_W + LANES] = k[:, p * LANES:(p + 1) * LANES]
            kx_ref[0, rows, p * PAIR_W + LANES:(p + 1) * PAIR_W] = bias
    carry_ref[...] = c_end

    kmax = jnp.maximum(kmax_ref[...], norm2)
    kmax_ref[...] = kmax
    qk2 = NORM_SLACK * norm2 * pltpu.roll(kmax, LANES - N_HEADS, 1)
    gap = hist_ref[...] - c_prev - PRUNE_GAP
    tile = lax.broadcasted_iota(jnp.int32, gap.shape, 0)
    skip = (tile < i) & (gap > 0.0) & (qk2 < gap * gap)
    i_f = i.astype(F32)
    kept = jnp.where(skip, i_f, jnp.minimum(tile.astype(F32), i_f))
    first_ref[0, 0] = jnp.min(kept, axis=0, keepdims=True).astype(jnp.int32)
    hist_ref[pl.ds(i, 1), :] = c_end


def _cast_rider_spec(shape, steps, n_tiles):
    rows, cols = shape
    share = 1
    while (rows * share) % (steps * BF16_ROWS) or steps % share:
        share += 1
    return pl.BlockSpec((rows * share // steps, cols),
                        lambda bi, i: ((bi * n_tiles + i) // share, 0))


def _qkv_call(x, g, wT, bfp, tri, ind, casts, scale):
    b, s, d = x.shape
    tm, tkv = TM_QKV, TKV
    n_pairs = N_HEADS // HEADS_PER_STEP
    n_tiles = s // tm
    steps = b * n_tiles
    own_rows = 3 * ATTN_WIDTH + LANES
    rest0 = 3 * ATTN_WIDTH + N_HEADS
    rest_rows = wT.shape[0] - rest0
    rest_blk = rest_rows // steps
    assert rest_rows % (steps * BF16_ROWS) == 0 and rest0 % 8 == 0
    kern = functools.partial(_qkv_kernel, tm=tm, sub=SUB_ROWS, tkv=tkv, n_cast=len(casts) + 1,
                             scale=scale)
    rider_specs = [_cast_rider_spec(c.shape, steps, n_tiles) for c in casts]
    rest_in = pl.BlockSpec((pl.Element(rest_blk), pl.Element(d)),
                           lambda bi, i: (pl.multiple_of(
                               rest0 + (bi * n_tiles + i) * rest_blk, 8), 0))
    rest_out = pl.BlockSpec((rest_blk, d), lambda bi, i: (bi * n_tiles + i, 0))
    return pl.pallas_call(
        kern,
        grid=(b, n_tiles),
        in_specs=[
            pl.BlockSpec((1, tm, d), lambda bi, i: (bi, i, 0)),
            _const_spec(g.shape),
            pl.BlockSpec((own_rows, d), lambda bi, i: (0, 0), pipeline_mode=pl.Buffered(1)),
            _const_spec(bfp.shape),
            _const_spec(tri.shape), _const_spec(ind.shape),
        ] + rider_specs + [rest_in],
        out_specs=[
            pl.BlockSpec((1, 1, ATTN_WIDTH, tm), lambda bi, i: (bi, i, 0, 0)),
            pl.BlockSpec((1, tm, n_pairs * PAIR_W), lambda bi, i: (bi, i, 0)),
            pl.BlockSpec((1, tm // tkv, ATTN_WIDTH, tkv), lambda bi, i: (bi, i, 0, 0)),
            pl.BlockSpec((1, 1, 1, LANES), lambda bi, i: (bi, i, 0, 0)),
        ] + rider_specs + [rest_out],
        out_shape=[
            jax.ShapeDtypeStruct((b, n_tiles, ATTN_WIDTH, tm), BF16),
            jax.ShapeDtypeStruct((b, s, n_pairs * PAIR_W), BF16),
            jax.ShapeDtypeStruct((b, s // tkv, ATTN_WIDTH, tkv), BF16),
            jax.ShapeDtypeStruct((b, n_tiles, 1, LANES), jnp.int32),
        ] + [jax.ShapeDtypeStruct(c.shape, BF16) for c in casts]
        + [jax.ShapeDtypeStruct((rest_rows, d), BF16)],
        scratch_shapes=[
            pltpu.VMEM((own_rows, d), BF16),
            pltpu.VMEM((1, LANES), F32),
            pltpu.VMEM((1, LANES), F32),
            pltpu.VMEM((n_tiles, LANES), F32),
        ],
        compiler_params=pltpu.CompilerParams(
            dimension_semantics=("arbitrary", "arbitrary"),
            vmem_limit_bytes=VMEM_LIMIT),
        name="qkv",
    )(x, g, wT, bfp, tri, ind, *casts, wT)


def _attn_kernel(first_ref, qT_ref, kx_ref, vT_ref, o_ref, qx_ref, s0_ref, s1_ref, mx_ref,
                 m_ref, acc_ref, *, tq, tkv, cw, tile, n_tiles, n_q):
    bi, pair = pl.program_id(0), pl.program_id(1)
    heads = range(HEADS_PER_STEP)
    n_diag = tq // tkv
    sub_q = tq // tile
    s_bufs = (s0_ref, s1_ref)
    ones_rows = jnp.ones((BF16_ROWS, tkv), BF16)
    row = lax.broadcasted_iota(jnp.int32, (LANES, tile), 0)

    def build_qx(qi):
        for d in range(sub_q):
            qT = qT_ref[0, qi * sub_q + d].astype(F32)
            for h in heads:
                mine = (row >= h * HEAD_DIM) & (row < (h + 1) * HEAD_DIM)
                qx_ref[h, 0:LANES, d * tile:(d + 1) * tile] = (
                    jnp.where(mine, qT, 0.0).astype(BF16))

    def reset_state():
        m_ref[...] = jnp.full_like(m_ref, MASKED)
        acc_ref[...] = jnp.zeros_like(acc_ref)

    def first_block(qi):
        t0 = n_tiles
        for d in range(sub_q):
            for h in heads:
                t0 = jnp.minimum(t0, first_ref[(bi * n_tiles + qi * sub_q + d) * N_HEADS
                                               + pair * HEADS_PER_STEP + h])
        j0 = jnp.minimum(t0 * (tile // tkv), qi * n_diag - 2)
        return 2 * lax.shift_right_logical(j0, 1)

    def units_for(d):
        out = []
        for h in heads:
            for c in range(tq // cw):
                if d is None:
                    out.append((h, c, tkv, None))
                    continue
                if (c + 1) * cw <= d * tkv:
                    continue
                kr = min(tkv, (c + 1) * cw - d * tkv)
                masked = c * cw < d * tkv + kr - 1
                mask = None
                if masked:
                    mask = (lax.broadcasted_iota(jnp.int32, (kr, cw), 0) + (d * tkv - c * cw)
                            <= lax.broadcasted_iota(jnp.int32, (kr, cw), 1))
                out.append((h, c, kr, mask))
        return out

    def qk_unit(j, buf, unit):
        h, c, kr, mask = unit
        cols = slice(c * cw, (c + 1) * cw)
        start = pl.multiple_of(j * tkv, tkv)
        s = _dot(kx_ref[0, pl.ds(start, kr), :], qx_ref[h, :, cols])
        if mask is not None:
            s = jnp.where(mask, s, MASKED)
        s_bufs[buf][h, 0:kr, cols] = s
        mx_ref[buf, h, :, cols] = jnp.max(s, axis=0, keepdims=True)

    def softmax_unit(buf, unit):
        h, c, kr, _ = unit
        cols = slice(c * cw, (c + 1) * cw)
        m_old = m_ref[h, :, cols]
        m_new = jnp.maximum(m_old, mx_ref[buf, h, :, cols])
        m_ref[h, :, cols] = m_new
        p = jnp.exp2(s_bufs[buf][h, 0:kr, cols] - m_new).astype(BF16)
        return p, jnp.exp2(m_old - m_new)

    def pv_unit(j, unit, p, alpha):
        h, c, kr, _ = unit
        cols = slice(c * cw, (c + 1) * cw)
        vx = jnp.concatenate(
            [vT_ref[0, j, h * HEAD_DIM:(h + 1) * HEAD_DIM, :][:, 0:kr],
             jnp.ones((BF16_ROWS, kr), BF16)], axis=0)
        acc_ref[h, :, cols] = alpha * acc_ref[h, :, cols] + _dot(vx, p)

    def step(j, buf, cur_d=None, next_d=None, next_j=None, before_next=None):
        cur = units_for(cur_d)
        nxt = units_for(next_d)
        next_j = j + 1 if next_j is None else next_j
        if before_next is not None:
            before_next()
        pending = None
        for i in range(max(len(cur), len(nxt))):
            if i < len(nxt):
                qk_unit(next_j, 1 - buf, nxt[i])
            if pending is not None:
                pv_unit(j, *pending)
                pending = None
            if i < len(cur):
                pending = (cur[i],) + softmax_unit(buf, cur[i])
        if pending is not None:
            pv_unit(j, *pending)

    def diagonal_and_finish(qi):
        n_full = qi * n_diag
        for d in range(n_diag - 1):
            step(n_full + d, d % 2, cur_d=d, next_d=d + 1)
        nq = jnp.minimum(qi + 1, n_q - 1)
        step(n_full + n_diag - 1, (n_diag - 1) % 2, cur_d=n_diag - 1, next_d=None,
             next_j=first_block(nq), before_next=lambda: build_qx(nq))
        outs = []
        for h in heads:
            a = acc_ref[h]
            outs.append(a[0:HEAD_DIM] / a[HEAD_DIM:HEAD_DIM + 1])
        o = jnp.concatenate(outs, axis=0)
        start = qi * tq if isinstance(qi, int) else pl.multiple_of(qi * tq, tq)
        o_ref[0, pl.ds(start, tq), :] = o.T.astype(BF16)
        reset_state()

    def full_pair(i, carry):
        step(2 * i, 0)
        step(2 * i + 1, 1)
        return carry

    def later_tile(qi, carry):
        n_full = qi * n_diag
        lax.fori_loop(lax.shift_right_logical(first_block(qi), 1),
                      qi * (n_diag // 2) - 1, full_pair, 0)
        step(n_full - 2, 0)
        step(n_full - 1, 1, next_d=0)
        diagonal_and_finish(qi)
        return carry

    rowx = lax.broadcasted_iota(jnp.int32, (LANES, tq), 0)
    for h in heads:
        head = pair * HEADS_PER_STEP + h
        ones = rowx == head
        for j in range(1, N_SPLIT):
            ones = ones | (rowx == head + j * N_HEADS)
        qx_ref[h, LANES:PAIR_W, :] = jnp.where(ones, 1.0, 0.0).astype(BF16)
    reset_state()
    build_qx(0)
    for unit in units_for(0):
        qk_unit(0, 0, unit)
    diagonal_and_finish(0)
    lax.fori_loop(1, n_q, later_tile, 0)


def _attn_call(first, qT, kx, vT):
    b, s = kx.shape[0], kx.shape[1]
    tq, tkv = TQ, TKV
    n_pairs = N_HEADS // HEADS_PER_STEP
    n_tiles = s // TM_QKV
    kern = functools.partial(_attn_kernel, tq=tq, tkv=tkv, cw=CW, tile=TM_QKV,
                             n_tiles=n_tiles, n_q=s // tq)
    grid_spec = pltpu.PrefetchScalarGridSpec(
        num_scalar_prefetch=1,
        grid=(b, n_pairs),
        in_specs=[
            pl.BlockSpec((1, n_tiles, LANES, TM_QKV), lambda bi, p, first: (bi, 0, p, 0)),
            pl.BlockSpec((1, s, PAIR_W), lambda bi, p, first: (bi, 0, p)),
            pl.BlockSpec((1, s // tkv, LANES, tkv), lambda bi, p, first: (bi, 0, p, 0)),
        ],
        out_specs=pl.BlockSpec((1, s, LANES), lambda bi, p, first: (bi, 0, p)),
        scratch_shapes=[
            pltpu.VMEM((HEADS_PER_STEP, PAIR_W, tq), BF16),
            pltpu.VMEM((HEADS_PER_STEP, tkv, tq), F32),
            pltpu.VMEM((HEADS_PER_STEP, tkv, tq), F32),
            pltpu.VMEM((2, HEADS_PER_STEP, 1, tq), F32),
            pltpu.VMEM((HEADS_PER_STEP, 1, tq), F32),
            pltpu.VMEM((HEADS_PER_STEP, HEAD_DIM + BF16_ROWS, tq), F32),
        ],
    )
    return pl.pallas_call(
        kern,
        grid_spec=grid_spec,
        out_shape=jax.ShapeDtypeStruct((b, s, ATTN_WIDTH), BF16),
        compiler_params=pltpu.CompilerParams(
            dimension_semantics=("arbitrary", "arbitrary"),
            vmem_limit_bytes=VMEM_LIMIT),
        name="attn",
    )(first, qT, kx, vT)


def _mix_kernel(x_ref, yb_ref, gpre_ref, gpost_ref, wcg_ref, cw_ref, wb_ref, wo_ref,
                o_ref, ztail_ref, *, tm, sub, tiles_per_seq, d_model):
    parts = [slice(r * sub, (r + 1) * sub) for r in range(tm // sub)]
    xs = [x_ref[rows, :] for rows in parts]
    projs = []
    for x in xs:
        h = _rmsnorm(x, gpre_ref[...]).astype(BF16)
        projs.append(_dot_nt(h, wcg_ref[...]))

    first = pl.program_id(0) % tiles_per_seq == 0
    zh = jnp.where(first, 0.0, ztail_ref[...])
    branches = []
    for rows, proj in zip(parts, projs):
        gate_b = proj[:, 0:CONV_CH]
        z = proj[:, CONV_CH:2 * CONV_CH] * proj[:, 2 * CONV_CH:3 * CONV_CH]
        zz = jnp.concatenate([zh, z], axis=0)
        zh = z[sub - HALO:sub]
        conv = cw_ref[CONV_K - 1:CONV_K, :] * z
        for back in range(1, CONV_K):
            shifted = pltpu.roll(zz, back, 0)[HALO:]
            conv = conv + cw_ref[CONV_K - 1 - back:CONV_K - back, :] * shifted
        ya = (gate_b * conv).astype(BF16)
        branches.append((_dot(ya, wb_ref[0:CONV_CH, :]),
                         _dot(yb_ref[rows, :], wb_ref[CONV_CH:CONV_CH + ATTN_WIDTH, :])))
    ztail_ref[...] = zh

    g0 = 3 * CONV_CH
    ms = []
    for proj, (ya_d, yb_d) in zip(projs, branches):
        gc = jax.nn.sigmoid(proj[:, g0:g0 + d_model])
        ga = jax.nn.sigmoid(proj[:, g0 + d_model:g0 + 2 * d_model])
        merged = (gc * ya_d + ga * yb_d).astype(BF16)
        ms.append(_dot(merged, wo_ref[...]))
    for rows, x, m in zip(parts, xs, ms):
        o_ref[rows, :] = x + _rmsnorm(m, gpost_ref[...])


def _mix_call(x2, yb2, gpre, gpost, wcg, cw, wb, wo, seq_len):
    n, d = x2.shape
    tm = TM_MIX
    kern = functools.partial(_mix_kernel, tm=tm, sub=SUB_ROWS, tiles_per_seq=seq_len // tm,
                             d_model=d)
    return pl.pallas_call(
        kern,
        grid=(n // tm,),
        in_specs=[
            pl.BlockSpec((tm, d), lambda i: (i, 0)),
            pl.BlockSpec((tm, ATTN_WIDTH), lambda i: (i, 0)),
            _const_spec(gpre.shape), _const_spec(gpost.shape), _const_spec(wcg.shape),
            _const_spec(cw.shape), _const_spec(wb.shape), _const_spec(wo.shape),
        ],
        out_specs=pl.BlockSpec((tm, d), lambda i: (i, 0)),
        out_shape=jax.ShapeDtypeStruct((n, d), F32),
        scratch_shapes=[pltpu.VMEM((HALO, CONV_CH), F32)],
        compiler_params=pltpu.CompilerParams(
            dimension_semantics=("arbitrary",), vmem_limit_bytes=VMEM_LIMIT),
        name="mix",
    )(x2, yb2, gpre, gpost, wcg, cw, wb, wo)


def _ffn_kernel(x_ref, gpre_ref, gpost_ref, wgu_ref, wd_ref, o_ref, *, hidden, sub):
    parts = [slice(r * sub, (r + 1) * sub) for r in range(x_ref.shape[0] // sub)]
    xs = [x_ref[rows, :] for rows in parts]
    gus = []
    for x in xs:
        h = _rmsnorm(x, gpre_ref[...]).astype(BF16)
        gus.append(_dot(h, wgu_ref[...]))
    fs = []
    for gu in gus:
        g = gu[:, 0:hidden]
        u = gu[:, hidden:2 * hidden]
        a = (g * jax.nn.sigmoid(g) * u).astype(BF16)
        fs.append(_dot(a, wd_ref[...]))
    for rows, x, f in zip(parts, xs, fs):
        o_ref[rows, :] = x + _rmsnorm(f, gpost_ref[...])


def _ffn_call(x2, gpre, gpost, wgu, wd):
    n, d = x2.shape
    tm = TM_FFN
    hidden = wd.shape[0]
    kern = functools.partial(_ffn_kernel, hidden=hidden, sub=SUB_ROWS)
    return pl.pallas_call(
        kern,
        grid=(n // tm,),
        in_specs=[
            pl.BlockSpec((tm, d), lambda i: (i, 0)),
            _const_spec(gpre.shape), _const_spec(gpost.shape),
            _const_spec(wgu.shape), _const_spec(wd.shape),
        ],
        out_specs=pl.BlockSpec((tm, d), lambda i: (i, 0)),
        out_shape=jax.ShapeDtypeStruct((n, d), F32),
        compiler_params=pltpu.CompilerParams(
            dimension_semantics=("arbitrary",), vmem_limit_bytes=VMEM_LIMIT),
        name="ffn",
    )(x2, gpre, gpost, wgu, wd)


def _norm_indicator():
    ind = np.zeros((2 * ATTN_WIDTH, LANES), np.float32)
    for which in range(2):
        for col in range(ATTN_WIDTH):
            ind[which * ATTN_WIDTH + col, which * N_HEADS + col // HEAD_DIM] = 1.0
    return jnp.asarray(ind, BF16)


def kernel(x, norm_mix_pre, norm_mix_post, w_in, b_f, conv_w, w_branch, w_out,
           norm_ffn_pre, norm_ffn_post, w_gate_up, w_down):
    b, s, d = x.shape
    depth = w_in.shape[0]
    a = ATTN_WIDTH
    tri = jnp.asarray(np.tril(np.ones((SUB_ROWS, SUB_ROWS), np.float32)), BF16)
    ind = _norm_indicator()
    scale = LOG2E / math.sqrt(HEAD_DIM)

    for l in range(depth):
        bfp = jnp.pad(b_f[l], (0, LANES - N_HEADS)).reshape(1, LANES)
        qT, kx, vT, first, wb, wo, wgu, wd, wcg = _qkv_call(
            x, norm_mix_pre[l].reshape(1, d), w_in[l].T, bfp, tri, ind,
            [w_branch[l].reshape(CONV_CH + a, d), w_out[l], w_gate_up[l], w_down[l]], scale)
        yb = _attn_call(first[:, :, 0, 0:N_HEADS].reshape(-1), qT, kx, vT)
        x2 = _mix_call(
            x.reshape(b * s, d), yb.reshape(b * s, a),
            norm_mix_pre[l].reshape(1, d), norm_mix_post[l].reshape(1, d),
            wcg, conv_w[l], wb, wo, s)
        x2 = _ffn_call(x2, norm_ffn_pre[l].reshape(1, d), norm_ffn_post[l].reshape(1, d),
                       wgu, wd)
        x = x2.reshape(b, s, d)
    return x
```

```python
import functools
import math

import jax
import jax.numpy as jnp
import numpy as np
from jax import lax
from jax.experimental import pallas as pl
from jax.experimental.pallas import tpu as pltpu

F32 = jnp.float32
BF16 = jnp.bfloat16

RMS_EPS = 1e-6
N_HEADS = 8
HEAD_DIM = 64
ATTN_WIDTH = N_HEADS * HEAD_DIM
CONV_CH = 512
CONV_K = 3
LANES = 128
BF16_ROWS = 16
HEADS_PER_STEP = 2
PAIR_W = 2 * LANES
N_SPLIT = 3
MASKED = -1e30
LOG2E = math.log2(math.e)
PRUNE_GAP = 106.0 * LOG2E
NORM_SLACK = 4.0 * 1.04
VMEM_LIMIT = 60 * 1024 * 1024

TM_QKV = 1024
PRUNE_TILE = 512
TQ = 1024
TKV = 512
CW = 256
TM_MIX = 1024
TM_FFN = 1024
SUB_ROWS = 256
HALO = 8

NT_DIMS = (((1,), (1,)), ((), ()))


def _rmsnorm(x, g):
    ms = jnp.mean(x * x, axis=-1, keepdims=True)
    return (x * lax.rsqrt(ms + RMS_EPS)) * g


def _split3(v):
    parts = []
    r = v
    for _ in range(N_SPLIT):
        p = r.astype(BF16)
        parts.append(p)
        r = r - p.astype(F32)
    return parts


def _dot(a, b):
    return jnp.dot(a, b, preferred_element_type=F32)


def _dot_nt(a, b):
    return lax.dot_general(a, b, NT_DIMS, preferred_element_type=F32)


def _const_spec(shape):
    nd = len(shape)
    return pl.BlockSpec(shape, lambda *_: (0,) * nd, pipeline_mode=pl.Buffered(1))


def _qkv_kernel(*refs, tm, sub, pt, tkv, n_cast, scale):
    (x_ref, g_ref, w_ref, bf_ref, tri_ref, ind_ref), refs = refs[:6], refs[6:]
    cast_in, refs = refs[:n_cast], refs[n_cast:]
    (qT_ref, kx_ref, vT_ref, first_ref), refs = refs[:4], refs[4:]
    cast_out, (w16_ref, carry_ref, kmax_ref, hist_ref) = refs[:n_cast], refs[n_cast:]
    i = pl.program_id(1)

    a = ATTN_WIDTH

    @pl.when((pl.program_id(0) == 0) & (i == 0))
    def _():
        w16_ref[0:a, :] = (w_ref[0:a, :] * scale).astype(BF16)
        w16_ref[a:, :] = w_ref[a:, :].astype(BF16)

    @pl.when(i == 0)
    def _():
        carry_ref[...] = jnp.zeros_like(carry_ref)
        kmax_ref[...] = jnp.zeros_like(kmax_ref)
        hist_ref[...] = jnp.zeros_like(hist_ref)

    parts = [slice(r * sub, (r + 1) * sub) for r in range(tm // sub)]
    head_lane = lax.broadcasted_iota(jnp.int32, (sub, LANES), 1) < N_HEADS

    def pack3(v):
        pieces = _split3(jnp.where(head_lane, v, 0.0))
        out = pieces[0].astype(F32)
        for j in range(1, N_SPLIT):
            out = out + pltpu.roll(pieces[j].astype(F32), j * N_HEADS, 1)
        return out.astype(BF16)

    hs, vfs = [], []
    for rows in parts:
        hs.append(_rmsnorm(x_ref[0, rows, :], g_ref[...]).astype(BF16))
        vfs.append(_dot_nt(hs[-1], w16_ref[2 * a:3 * a + LANES, :]))
    for src, dst in zip(cast_in, cast_out):
        dst[...] = src[...].astype(BF16)

    qs, css = [], []
    for rows, h, vf in zip(parts, hs, vfs):
        r0 = rows.start
        vT_ref[0, r0 // tkv, :, r0 % tkv:r0 % tkv + sub] = vf[:, 0:a].T.astype(BF16)
        f = vf[:, a:a + LANES] + bf_ref[...]
        logf = (jnp.minimum(f, 0.0) - jnp.log1p(jnp.exp(-jnp.abs(f)))) * LOG2E
        qf32 = _dot_nt(h, w16_ref[0:a, :])
        qs.append(qf32.astype(BF16))
        qT_ref[0, r0 // pt, :, r0 % pt:r0 % pt + sub] = qf32.T.astype(BF16)
        css.append(_dot(tri_ref[...], pack3(logf)))
    ks, norms = [], []
    for h, q in zip(hs, qs):
        ks.append(_dot_nt(h, w16_ref[a:2 * a, :]).astype(BF16))
        qf = q.astype(F32)
        kf = ks[-1].astype(F32)
        sq = jnp.concatenate([qf * qf, kf * kf], axis=1).astype(BF16)
        norms.append(jnp.max(_dot(sq, ind_ref[...]), axis=0, keepdims=True))

    c_end = carry_ref[...]
    c_ends = []
    for rows, cs, k in zip(parts, css, ks):
        c = cs + c_end
        for j in range(1, N_SPLIT):
            c = c + pltpu.roll(cs, LANES - j * N_HEADS, 1)
        c_end = c[sub - 1:sub]
        c_ends.append(c_end)
        bias = pack3(-c)
        for p in range(N_HEADS // HEADS_PER_STEP):
            kx_ref[0, rows, p * PAIR_W:p * PAIR_W + LANES] = k[:, p * LANES:(p + 1) * LANES]
            kx_ref[0, rows, p * PAIR_W + LANES:(p + 1) * PAIR_W] = bias

    per = pt // sub
    c_prev = carry_ref[...]
    for t in range(tm // pt):
        gi = i * (tm // pt) + t
        norm2 = functools.reduce(jnp.maximum, norms[t * per:(t + 1) * per])
        kmax = jnp.maximum(kmax_ref[...], norm2)
        kmax_ref[...] = kmax
        qk2 = NORM_SLACK * norm2 * pltpu.roll(kmax, LANES - N_HEADS, 1)
        gap = hist_ref[...] - c_prev - PRUNE_GAP
        tile = lax.broadcasted_iota(jnp.int32, gap.shape, 0)
        skip = (tile < gi) & (gap > 0.0) & (qk2 < gap * gap)
        gi_f = gi.astype(F32)
        kept = jnp.where(skip, gi_f, jnp.minimum(tile.astype(F32), gi_f))
        first_ref[0, t] = jnp.min(kept, axis=0, keepdims=True).astype(jnp.int32)
        c_prev = c_ends[(t + 1) * per - 1]
        hist_ref[pl.ds(gi, 1), :] = c_prev
    carry_ref[...] = c_prev


def _cast_rider_spec(shape, steps, n_tiles):
    rows, cols = shape
    share = 1
    while (rows * share) % (steps * BF16_ROWS) or steps % share:
        share += 1
    return pl.BlockSpec((rows * share // steps, cols),
                        lambda bi, i: ((bi * n_tiles + i) // share, 0))


def _qkv_call(x, g, wT, bfp, tri, ind, casts, scale):
    b, s, d = x.shape
    tm, pt, tkv = TM_QKV, PRUNE_TILE, TKV
    n_pairs = N_HEADS // HEADS_PER_STEP
    n_tiles = s // tm
    steps = b * n_tiles
    own_rows = 3 * ATTN_WIDTH + LANES
    rest0 = 3 * ATTN_WIDTH + N_HEADS
    rest_rows = wT.shape[0] - rest0
    rest_blk = rest_rows // steps
    assert rest_rows % (steps * BF16_ROWS) == 0 and rest0 % 8 == 0
    kern = functools.partial(_qkv_kernel, tm=tm, sub=SUB_ROWS, pt=pt, tkv=tkv, n_cast=len(casts) + 1,
                             scale=scale)
    rider_specs = [_cast_rider_spec(c.shape, steps, n_tiles) for c in casts]
    rest_in = pl.BlockSpec((pl.Element(rest_blk), pl.Element(d)),
                           lambda bi, i: (pl.multiple_of(
                               rest0 + (bi * n_tiles + i) * rest_blk, 8), 0))
    rest_out = pl.BlockSpec((rest_blk, d), lambda bi, i: (bi * n_tiles + i, 0))
    return pl.pallas_call(
        kern,
        grid=(b, n_tiles),
        in_specs=[
            pl.BlockSpec((1, tm, d), lambda bi, i: (bi, i, 0)),
            _const_spec(g.shape),
            pl.BlockSpec((own_rows, d), lambda bi, i: (0, 0), pipeline_mode=pl.Buffered(1)),
            _const_spec(bfp.shape),
            _const_spec(tri.shape), _const_spec(ind.shape),
        ] + rider_specs + [rest_in],
        out_specs=[
            pl.BlockSpec((1, tm // pt, ATTN_WIDTH, pt), lambda bi, i: (bi, i, 0, 0)),
            pl.BlockSpec((1, tm, n_pairs * PAIR_W), lambda bi, i: (bi, i, 0)),
            pl.BlockSpec((1, tm // tkv, ATTN_WIDTH, tkv), lambda bi, i: (bi, i, 0, 0)),
            pl.BlockSpec((1, tm // pt, 1, LANES), lambda bi, i: (bi, i, 0, 0)),
        ] + rider_specs + [rest_out],
        out_shape=[
            jax.ShapeDtypeStruct((b, s // pt, ATTN_WIDTH, pt), BF16),
            jax.ShapeDtypeStruct((b, s, n_pairs * PAIR_W), BF16),
            jax.ShapeDtypeStruct((b, s // tkv, ATTN_WIDTH, tkv), BF16),
            jax.ShapeDtypeStruct((b, s // pt, 1, LANES), jnp.int32),
        ] + [jax.ShapeDtypeStruct(c.shape, BF16) for c in casts]
        + [jax.ShapeDtypeStruct((rest_rows, d), BF16)],
        scratch_shapes=[
            pltpu.VMEM((own_rows, d), BF16),
            pltpu.VMEM((1, LANES), F32),
            pltpu.VMEM((1, LANES), F32),
            pltpu.VMEM((s // pt, LANES), F32),
        ],
        compiler_params=pltpu.CompilerParams(
            dimension_semantics=("arbitrary", "arbitrary"),
            vmem_limit_bytes=VMEM_LIMIT),
        name="qkv",
    )(x, g, wT, bfp, tri, ind, *casts, wT)


def _attn_kernel(first_ref, qT_ref, kx_ref, vT_ref, o_ref, qx_ref, s0_ref, s1_ref, mx_ref,
                 m_ref, acc_ref, *, tq, tkv, cw, tile, n_tiles, n_q):
    bi, pair = pl.program_id(0), pl.program_id(1)
    heads = range(HEADS_PER_STEP)
    n_diag = tq // tkv
    sub_q = tq // tile
    s_bufs = (s0_ref, s1_ref)
    ones_rows = jnp.ones((BF16_ROWS, tkv), BF16)
    row = lax.broadcasted_iota(jnp.int32, (LANES, tile), 0)

    def build_qx(qi):
        for d in range(sub_q):
            qT = qT_ref[0, qi * sub_q + d].astype(F32)
            for h in heads:
                mine = (row >= h * HEAD_DIM) & (row < (h + 1) * HEAD_DIM)
                qx_ref[h, 0:LANES, d * tile:(d + 1) * tile] = (
                    jnp.where(mine, qT, 0.0).astype(BF16))

    def reset_state():
        m_ref[...] = jnp.full_like(m_ref, MASKED)
        acc_ref[...] = jnp.zeros_like(acc_ref)

    def first_block(qi):
        t0 = n_tiles
        for d in range(sub_q):
            for h in heads:
                t0 = jnp.minimum(t0, first_ref[(bi * n_tiles + qi * sub_q + d) * N_HEADS
                                               + pair * HEADS_PER_STEP + h])
        j0 = jnp.minimum(t0 * (tile // tkv), qi * n_diag - 2)
        return 2 * lax.shift_right_logical(j0, 1)

    def units_for(d):
        out = []
        for h in heads:
            for c in range(tq // cw):
                if d is None:
                    out.append((h, c, tkv, None))
                    continue
                if (c + 1) * cw <= d * tkv:
                    continue
                kr = min(tkv, (c + 1) * cw - d * tkv)
                masked = c * cw < d * tkv + kr - 1
                mask = None
                if masked:
                    mask = (lax.broadcasted_iota(jnp.int32, (kr, cw), 0) + (d * tkv - c * cw)
                            <= lax.broadcasted_iota(jnp.int32, (kr, cw), 1))
                out.append((h, c, kr, mask))
        return out

    def qk_unit(j, buf, unit):
        h, c, kr, mask = unit
        cols = slice(c * cw, (c + 1) * cw)
        start = pl.multiple_of(j * tkv, tkv)
        s = _dot(kx_ref[0, pl.ds(start, kr), :], qx_ref[h, :, cols])
        if mask is not None:
            s = jnp.where(mask, s, MASKED)
        s_bufs[buf][h, 0:kr, cols] = s
        mx_ref[buf, h, :, cols] = jnp.max(s, axis=0, keepdims=True)

    def softmax_unit(buf, unit):
        h, c, kr, _ = unit
        cols = slice(c * cw, (c + 1) * cw)
        m_old = m_ref[h, :, cols]
        m_new = jnp.maximum(m_old, mx_ref[buf, h, :, cols])
        m_ref[h, :, cols] = m_new
        p = jnp.exp2(s_bufs[buf][h, 0:kr, cols] - m_new).astype(BF16)
        return p, jnp.exp2(m_old - m_new)

    def pv_unit(j, unit, p, alpha):
        h, c, kr, _ = unit
        cols = slice(c * cw, (c + 1) * cw)
        vx = jnp.concatenate(
            [vT_ref[0, j, h * HEAD_DIM:(h + 1) * HEAD_DIM, :][:, 0:kr],
             jnp.ones((BF16_ROWS, kr), BF16)], axis=0)
        acc_ref[h, :, cols] = alpha * acc_ref[h, :, cols] + _dot(vx, p)

    def step(j, buf, cur_d=None, next_d=None, next_j=None, before_next=None):
        cur = units_for(cur_d)
        nxt = units_for(next_d)
        next_j = j + 1 if next_j is None else next_j
        if before_next is not None:
            before_next()
        pending = None
        for i in range(max(len(cur), len(nxt))):
            if i < len(nxt):
                qk_unit(next_j, 1 - buf, nxt[i])
            if pending is not None:
                pv_unit(j, *pending)
                pending = None
            if i < len(cur):
                pending = (cur[i],) + softmax_unit(buf, cur[i])
        if pending is not None:
            pv_unit(j, *pending)

    def diagonal_and_finish(qi):
        n_full = qi * n_diag
        for d in range(n_diag - 1):
            step(n_full + d, d % 2, cur_d=d, next_d=d + 1)
        nq = jnp.minimum(qi + 1, n_q - 1)
        step(n_full + n_diag - 1, (n_diag - 1) % 2, cur_d=n_diag - 1, next_d=None,
             next_j=first_block(nq), before_next=lambda: build_qx(nq))
        outs = []
        for h in heads:
            a = acc_ref[h]
            outs.append(a[0:HEAD_DIM] / a[HEAD_DIM:HEAD_DIM + 1])
        o = jnp.concatenate(outs, axis=0)
        start = qi * tq if isinstance(qi, int) else pl.multiple_of(qi * tq, tq)
        o_ref[0, pl.ds(start, tq), :] = o.T.astype(BF16)
        reset_state()

    def full_pair(i, carry):
        step(2 * i, 0)
        step(2 * i + 1, 1)
        return carry

    def later_tile(qi, carry):
        n_full = qi * n_diag
        lax.fori_loop(lax.shift_right_logical(first_block(qi), 1),
                      qi * (n_diag // 2) - 1, full_pair, 0)
        step(n_full - 2, 0)
        step(n_full - 1, 1, next_d=0)
        diagonal_and_finish(qi)
        return carry

    rowx = lax.broadcasted_iota(jnp.int32, (LANES, tq), 0)
    for h in heads:
        head = pair * HEADS_PER_STEP + h
        ones = rowx == head
        for j in range(1, N_SPLIT):
            ones = ones | (rowx == head + j * N_HEADS)
        qx_ref[h, LANES:PAIR_W, :] = jnp.where(ones, 1.0, 0.0).astype(BF16)
    reset_state()
    build_qx(0)
    for unit in units_for(0):
        qk_unit(0, 0, unit)
    diagonal_and_finish(0)
    lax.fori_loop(1, n_q, later_tile, 0)


def _attn_call(first, qT, kx, vT):
    b, s = kx.shape[0], kx.shape[1]
    tq, tkv = TQ, TKV
    n_pairs = N_HEADS // HEADS_PER_STEP
    n_tiles = s // PRUNE_TILE
    kern = functools.partial(_attn_kernel, tq=tq, tkv=tkv, cw=CW, tile=PRUNE_TILE,
                             n_tiles=n_tiles, n_q=s // tq)
    grid_spec = pltpu.PrefetchScalarGridSpec(
        num_scalar_prefetch=1,
        grid=(b, n_pairs),
        in_specs=[
            pl.BlockSpec((1, n_tiles, LANES, PRUNE_TILE), lambda bi, p, first: (bi, 0, p, 0)),
            pl.BlockSpec((1, s, PAIR_W), lambda bi, p, first: (bi, 0, p)),
            pl.BlockSpec((1, s // tkv, LANES, tkv), lambda bi, p, first: (bi, 0, p, 0)),
        ],
        out_specs=pl.BlockSpec((1, s, LANES), lambda bi, p, first: (bi, 0, p)),
        scratch_shapes=[
            pltpu.VMEM((HEADS_PER_STEP, PAIR_W, tq), BF16),
            pltpu.VMEM((HEADS_PER_STEP, tkv, tq), F32),
            pltpu.VMEM((HEADS_PER_STEP, tkv, tq), F32),
            pltpu.VMEM((2, HEADS_PER_STEP, 1, tq), F32),
            pltpu.VMEM((HEADS_PER_STEP, 1, tq), F32),
            pltpu.VMEM((HEADS_PER_STEP, HEAD_DIM + BF16_ROWS, tq), F32),
        ],
    )
    return pl.pallas_call(
        kern,
        grid_spec=grid_spec,
        out_shape=jax.ShapeDtypeStruct((b, s, ATTN_WIDTH), BF16),
        compiler_params=pltpu.CompilerParams(
            dimension_semantics=("arbitrary", "arbitrary"),
            vmem_limit_bytes=VMEM_LIMIT),
        name="attn",
    )(first, qT, kx, vT)


def _mix_kernel(x_ref, yb_ref, gpre_ref, gpost_ref, wcg_ref, cw_ref, wb_ref, wo_ref,
                o_ref, ztail_ref, *, tm, sub, tiles_per_seq, d_model):
    parts = [slice(r * sub, (r + 1) * sub) for r in range(tm // sub)]
    xs = [x_ref[rows, :] for rows in parts]
    projs = []
    for x in xs:
        h = _rmsnorm(x, gpre_ref[...]).astype(BF16)
        projs.append(_dot_nt(h, wcg_ref[...]))

    first = pl.program_id(0) % tiles_per_seq == 0
    zh = jnp.where(first, 0.0, ztail_ref[...])
    branches = []
    for rows, proj in zip(parts, projs):
        gate_b = proj[:, 0:CONV_CH]
        z = proj[:, CONV_CH:2 * CONV_CH] * proj[:, 2 * CONV_CH:3 * CONV_CH]
        zz = jnp.concatenate([zh, z], axis=0)
        zh = z[sub - HALO:sub]
        conv = cw_ref[CONV_K - 1:CONV_K, :] * z
        for back in range(1, CONV_K):
            shifted = pltpu.roll(zz, back, 0)[HALO:]
            conv = conv + cw_ref[CONV_K - 1 - back:CONV_K - back, :] * shifted
        ya = (gate_b * conv).astype(BF16)
        branches.append((_dot(ya, wb_ref[0:CONV_CH, :]),
                         _dot(yb_ref[rows, :], wb_ref[CONV_CH:CONV_CH + ATTN_WIDTH, :])))
    ztail_ref[...] = zh

    g0 = 3 * CONV_CH
    ms = []
    for proj, (ya_d, yb_d) in zip(projs, branches):
        gc = jax.nn.sigmoid(proj[:, g0:g0 + d_model])
        ga = jax.nn.sigmoid(proj[:, g0 + d_model:g0 + 2 * d_model])
        merged = (gc * ya_d + ga * yb_d).astype(BF16)
        ms.append(_dot(merged, wo_ref[...]))
    for rows, x, m in zip(parts, xs, ms):
        o_ref[rows, :] = x + _rmsnorm(m, gpost_ref[...])


def _mix_call(x2, yb2, gpre, gpost, wcg, cw, wb, wo, seq_len):
    n, d = x2.shape
    tm = TM_MIX
    kern = functools.partial(_mix_kernel, tm=tm, sub=SUB_ROWS, tiles_per_seq=seq_len // tm,
                             d_model=d)
    return pl.pallas_call(
        kern,
        grid=(n // tm,),
        in_specs=[
            pl.BlockSpec((tm, d), lambda i: (i, 0)),
            pl.BlockSpec((tm, ATTN_WIDTH), lambda i: (i, 0)),
            _const_spec(gpre.shape), _const_spec(gpost.shape), _const_spec(wcg.shape),
            _const_spec(cw.shape), _const_spec(wb.shape), _const_spec(wo.shape),
        ],
        out_specs=pl.BlockSpec((tm, d), lambda i: (i, 0)),
        out_shape=jax.ShapeDtypeStruct((n, d), F32),
        scratch_shapes=[pltpu.VMEM((HALO, CONV_CH), F32)],
        compiler_params=pltpu.CompilerParams(
            dimension_semantics=("arbitrary",), vmem_limit_bytes=VMEM_LIMIT),
        name="mix",
    )(x2, yb2, gpre, gpost, wcg, cw, wb, wo)


def _ffn_kernel(x_ref, gpre_ref, gpost_ref, wgu_ref, wd_ref, o_ref, *, hidden, sub):
    parts = [slice(r * sub, (r + 1) * sub) for r in range(x_ref.shape[0] // sub)]
    xs = [x_ref[rows, :] for rows in parts]
    gus = []
    for x in xs:
        h = _rmsnorm(x, gpre_ref[...]).astype(BF16)
        gus.append(_dot(h, wgu_ref[...]))
    fs = []
    for gu in gus:
        g = gu[:, 0:hidden]
        u = gu[:, hidden:2 * hidden]
        a = (g * jax.nn.sigmoid(g) * u).astype(BF16)
        fs.append(_dot(a, wd_ref[...]))
    for rows, x, f in zip(parts, xs, fs):
        o_ref[rows, :] = x + _rmsnorm(f, gpost_ref[...])


def _ffn_call(x2, gpre, gpost, wgu, wd):
    n, d = x2.shape
    tm = TM_FFN
    hidden = wd.shape[0]
    kern = functools.partial(_ffn_kernel, hidden=hidden, sub=SUB_ROWS)
    return pl.pallas_call(
        kern,
        grid=(n // tm,),
        in_specs=[
            pl.BlockSpec((tm, d), lambda i: (i, 0)),
            _const_spec(gpre.shape), _const_spec(gpost.shape),
            _const_spec(wgu.shape), _const_spec(wd.shape),
        ],
        out_specs=pl.BlockSpec((tm, d), lambda i: (i, 0)),
        out_shape=jax.ShapeDtypeStruct((n, d), F32),
        compiler_params=pltpu.CompilerParams(
            dimension_semantics=("arbitrary",), vmem_limit_bytes=VMEM_LIMIT),
        name="ffn",
    )(x2, gpre, gpost, wgu, wd)


def _norm_indicator():
    ind = np.zeros((2 * ATTN_WIDTH, LANES), np.float32)
    for which in range(2):
        for col in range(ATTN_WIDTH):
            ind[which * ATTN_WIDTH + col, which * N_HEADS + col // HEAD_DIM] = 1.0
    return jnp.asarray(ind, BF16)


def kernel(x, norm_mix_pre, norm_mix_post, w_in, b_f, conv_w, w_branch, w_out,
           norm_ffn_pre, norm_ffn_post, w_gate_up, w_down):
    b, s, d = x.shape
    depth = w_in.shape[0]
    a = ATTN_WIDTH
    tri = jnp.asarray(np.tril(np.ones((SUB_ROWS, SUB_ROWS), np.float32)), BF16)
    ind = _norm_indicator()
    scale = LOG2E / math.sqrt(HEAD_DIM)

    for l in range(depth):
        bfp = jnp.pad(b_f[l], (0, LANES - N_HEADS)).reshape(1, LANES)
        qT, kx, vT, first, wb, wo, wgu, wd, wcg = _qkv_call(
            x, norm_mix_pre[l].reshape(1, d), w_in[l].T, bfp, tri, ind,
            [w_branch[l].reshape(CONV_CH + a, d), w_out[l], w_gate_up[l], w_down[l]], scale)
        yb = _attn_call(first[:, :, 0, 0:N_HEADS].reshape(-1), qT, kx, vT)
        x2 = _mix_call(
            x.reshape(b * s, d), yb.reshape(b * s, a),
            norm_mix_pre[l].reshape(1, d), norm_mix_post[l].reshape(1, d),
            wcg, conv_w[l], wb, wo, s)
        x2 = _ffn_call(x2, norm_ffn_pre[l].reshape(1, d), norm_ffn_post[l].reshape(1, d),
                       wgu, wd)
        x = x2.reshape(b, s, d)
    return x
```

```python
import functools
import math

import jax
import jax.numpy as jnp
import numpy as np
from jax import lax
from jax.experimental import pallas as pl
from jax.experimental.pallas import tpu as pltpu

F32 = jnp.float32
BF16 = jnp.bfloat16

RMS_EPS = 1e-6
N_HEADS = 8
HEAD_DIM = 64
ATTN_WIDTH = N_HEADS * HEAD_DIM
CONV_CH = 512
CONV_K = 3
LANES = 128
BF16_ROWS = 16
HEADS_PER_STEP = 2
PAIR_W = 2 * LANES
N_SPLIT = 3
MASKED = -1e30
LOG2E = math.log2(math.e)
PRUNE_GAP = 106.0 * LOG2E
NORM_SLACK = 4.0 * 1.04
VMEM_LIMIT = 60 * 1024 * 1024

TM_QKV = 1024
PRUNE_TILE = 512
TQ = 1024
TKV = 512
CW = 256
TM_MIX = 1024
TM_FFN = 1024
SUB_ROWS = 256
HALO = 8

NT_DIMS = (((1,), (1,)), ((), ()))


def _rmsnorm(x, g):
    ms = jnp.mean(x * x, axis=-1, keepdims=True)
    return (x * lax.rsqrt(ms + RMS_EPS)) * g


def _split3(v):
    parts = []
    r = v
    for _ in range(N_SPLIT):
        p = r.astype(BF16)
        parts.append(p)
        r = r - p.astype(F32)
    return parts


def _dot(a, b):
    return jnp.dot(a, b, preferred_element_type=F32)


def _dot_nt(a, b):
    return lax.dot_general(a, b, NT_DIMS, preferred_element_type=F32)


def _const_spec(shape):
    nd = len(shape)
    return pl.BlockSpec(shape, lambda *_: (0,) * nd, pipeline_mode=pl.Buffered(1))


def _qkv_kernel(x_ref, g_ref, w_ref, bf_ref, tri_ref, ind_ref, qT_ref, kx_ref, vT_ref,
                first_ref, w16_ref, carry_ref, kmax_ref, hist_ref, *, tm, sub, pt, tkv, scale):
    i = pl.program_id(1)

    a = ATTN_WIDTH

    @pl.when((pl.program_id(0) == 0) & (i == 0))
    def _():
        w16_ref[0:a, :] = (w_ref[0:a, :] * scale).astype(BF16)
        w16_ref[a:, :] = w_ref[a:, :].astype(BF16)

    @pl.when(i == 0)
    def _():
        carry_ref[...] = jnp.zeros_like(carry_ref)
        kmax_ref[...] = jnp.zeros_like(kmax_ref)
        hist_ref[...] = jnp.zeros_like(hist_ref)

    parts = [slice(r * sub, (r + 1) * sub) for r in range(tm // sub)]
    head_lane = lax.broadcasted_iota(jnp.int32, (sub, LANES), 1) < N_HEADS

    def pack3(v):
        pieces = _split3(jnp.where(head_lane, v, 0.0))
        out = pieces[0].astype(F32)
        for j in range(1, N_SPLIT):
            out = out + pltpu.roll(pieces[j].astype(F32), j * N_HEADS, 1)
        return out.astype(BF16)

    hs, vfs = [], []
    for rows in parts:
        hs.append(_rmsnorm(x_ref[0, rows, :], g_ref[...]).astype(BF16))
        vfs.append(_dot_nt(hs[-1], w16_ref[2 * a:3 * a + LANES, :]))
    qs, css = [], []
    for rows, h, vf in zip(parts, hs, vfs):
        r0 = rows.start
        vT_ref[0, r0 // tkv, :, r0 % tkv:r0 % tkv + sub] = vf[:, 0:a].T.astype(BF16)
        f = vf[:, a:a + LANES] + bf_ref[...]
        logf = (jnp.minimum(f, 0.0) - jnp.log1p(jnp.exp(-jnp.abs(f)))) * LOG2E
        qf32 = _dot_nt(h, w16_ref[0:a, :])
        qs.append(qf32.astype(BF16))
        qT_ref[0, r0 // pt, :, r0 % pt:r0 % pt + sub] = qf32.T.astype(BF16)
        css.append(_dot(tri_ref[...], pack3(logf)))
    ks, norms = [], []
    for h, q in zip(hs, qs):
        ks.append(_dot_nt(h, w16_ref[a:2 * a, :]).astype(BF16))
        qf = q.astype(F32)
        kf = ks[-1].astype(F32)
        sq = jnp.concatenate([qf * qf, kf * kf], axis=1).astype(BF16)
        norms.append(jnp.max(_dot(sq, ind_ref[...]), axis=0, keepdims=True))

    c_end = carry_ref[...]
    c_ends = []
    for rows, cs, k in zip(parts, css, ks):
        c = cs + c_end
        for j in range(1, N_SPLIT):
            c = c + pltpu.roll(cs, LANES - j * N_HEADS, 1)
        c_end = c[sub - 1:sub]
        c_ends.append(c_end)
        bias = pack3(-c)
        for p in range(N_HEADS // HEADS_PER_STEP):
            kx_ref[0, rows, p * PAIR_W:p * PAIR_W + LANES] = k[:, p * LANES:(p + 1) * LANES]
            kx_ref[0, rows, p * PAIR_W + LANES:(p + 1) * PAIR_W] = bias

    per = pt // sub
    c_prev = carry_ref[...]
    for t in range(tm // pt):
        gi = i * (tm // pt) + t
        norm2 = functools.reduce(jnp.maximum, norms[t * per:(t + 1) * per])
        kmax = jnp.maximum(kmax_ref[...], norm2)
        kmax_ref[...] = kmax
        qk2 = NORM_SLACK * norm2 * pltpu.roll(kmax, LANES - N_HEADS, 1)
        gap = hist_ref[...] - c_prev - PRUNE_GAP
        tile = lax.broadcasted_iota(jnp.int32, gap.shape, 0)
        skip = (tile < gi) & (gap > 0.0) & (qk2 < gap * gap)
        gi_f = gi.astype(F32)
        kept = jnp.where(skip, gi_f, jnp.minimum(tile.astype(F32), gi_f))
        first_ref[0, t] = jnp.min(kept, axis=0, keepdims=True).astype(jnp.int32)
        c_prev = c_ends[(t + 1) * per - 1]
        hist_ref[pl.ds(gi, 1), :] = c_prev
    carry_ref[...] = c_prev


def _riders(items, steps, step_of):
    in_specs, out_specs, out_shapes = [], [], []
    for arr, row0, rows in items:
        cols = arr.shape[1]
        assert rows % (steps * BF16_ROWS) == 0 and row0 % 8 == 0
        blk = rows // steps
        if row0 % blk == 0:
            in_specs.append(pl.BlockSpec(
                (blk, cols), lambda *g, o=row0 // blk: (step_of(*g) + o, 0)))
        else:
            in_specs.append(pl.BlockSpec(
                (pl.Element(blk), pl.Element(cols)),
                lambda *g, r=row0, n=blk: (pl.multiple_of(r + step_of(*g) * n, 8), 0)))
        out_specs.append(pl.BlockSpec((blk, cols), lambda *g: (step_of(*g), 0)))
        out_shapes.append(jax.ShapeDtypeStruct((rows, cols), BF16))
    return in_specs, out_specs, out_shapes


def _cast_riders(cast_in, cast_out):
    for src, dst in zip(cast_in, cast_out):
        dst[...] = src[...].astype(BF16)


def _qkv_call(x, g, wT, bfp, tri, ind, scale):
    b, s, d = x.shape
    tm, pt, tkv = TM_QKV, PRUNE_TILE, TKV
    n_pairs = N_HEADS // HEADS_PER_STEP
    n_tiles = s // tm
    own_rows = 3 * ATTN_WIDTH + LANES
    kern = functools.partial(_qkv_kernel, tm=tm, sub=SUB_ROWS, pt=pt, tkv=tkv, scale=scale)
    return pl.pallas_call(
        kern,
        grid=(b, n_tiles),
        in_specs=[
            pl.BlockSpec((1, tm, d), lambda bi, i: (bi, i, 0)),
            _const_spec(g.shape),
            pl.BlockSpec((own_rows, d), lambda bi, i: (0, 0), pipeline_mode=pl.Buffered(1)),
            _const_spec(bfp.shape),
            _const_spec(tri.shape), _const_spec(ind.shape),
        ],
        out_specs=[
            pl.BlockSpec((1, tm // pt, ATTN_WIDTH, pt), lambda bi, i: (bi, i, 0, 0)),
            pl.BlockSpec((1, tm, n_pairs * PAIR_W), lambda bi, i: (bi, i, 0)),
            pl.BlockSpec((1, tm // tkv, ATTN_WIDTH, tkv), lambda bi, i: (bi, i, 0, 0)),
            pl.BlockSpec((1, tm // pt, 1, LANES), lambda bi, i: (bi, i, 0, 0)),
        ],
        out_shape=[
            jax.ShapeDtypeStruct((b, s // pt, ATTN_WIDTH, pt), BF16),
            jax.ShapeDtypeStruct((b, s, n_pairs * PAIR_W), BF16),
            jax.ShapeDtypeStruct((b, s // tkv, ATTN_WIDTH, tkv), BF16),
            jax.ShapeDtypeStruct((b, s // pt, 1, LANES), jnp.int32),
        ],
        scratch_shapes=[
            pltpu.VMEM((own_rows, d), BF16),
            pltpu.VMEM((1, LANES), F32),
            pltpu.VMEM((1, LANES), F32),
            pltpu.VMEM((s // pt, LANES), F32),
        ],
        compiler_params=pltpu.CompilerParams(
            dimension_semantics=("arbitrary", "arbitrary"),
            vmem_limit_bytes=VMEM_LIMIT),
        name="qkv",
    )(x, g, wT, bfp, tri, ind)


def _attn_kernel(*refs, tq, tkv, cw, tile, n_tiles, n_q, n_cast):
    (first_ref, qT_ref, kx_ref, vT_ref), refs = refs[:4], refs[4:]
    cast_in, o_ref, refs = refs[:n_cast], refs[n_cast], refs[n_cast + 1:]
    cast_out, (qx_ref, s0_ref, s1_ref, mx_ref, m_ref, acc_ref) = refs[:n_cast], refs[n_cast:]
    _cast_riders(cast_in, cast_out)
    bi, pair = pl.program_id(0), pl.program_id(1)
    heads = range(HEADS_PER_STEP)
    n_diag = tq // tkv
    sub_q = tq // tile
    s_bufs = (s0_ref, s1_ref)
    row = lax.broadcasted_iota(jnp.int32, (LANES, tile), 0)

    def build_qx(qi):
        for d in range(sub_q):
            qT = qT_ref[0, qi * sub_q + d].astype(F32)
            for h in heads:
                mine = (row >= h * HEAD_DIM) & (row < (h + 1) * HEAD_DIM)
                qx_ref[h, 0:LANES, d * tile:(d + 1) * tile] = (
                    jnp.where(mine, qT, 0.0).astype(BF16))

    def reset_state():
        m_ref[...] = jnp.full_like(m_ref, MASKED)
        acc_ref[...] = jnp.zeros_like(acc_ref)

    def first_block(qi):
        t0 = n_tiles
        for d in range(sub_q):
            for h in heads:
                t0 = jnp.minimum(t0, first_ref[(bi * n_tiles + qi * sub_q + d) * N_HEADS
                                               + pair * HEADS_PER_STEP + h])
        j0 = jnp.minimum(t0 * (tile // tkv), qi * n_diag - 2)
        return 2 * lax.shift_right_logical(j0, 1)

    def units_for(d):
        out = []
        for h in heads:
            for c in range(tq // cw):
                if d is None:
                    out.append((h, c, tkv, None))
                    continue
                if (c + 1) * cw <= d * tkv:
                    continue
                kr = min(tkv, (c + 1) * cw - d * tkv)
                masked = c * cw < d * tkv + kr - 1
                mask = None
                if masked:
                    mask = (lax.broadcasted_iota(jnp.int32, (kr, cw), 0) + (d * tkv - c * cw)
                            <= lax.broadcasted_iota(jnp.int32, (kr, cw), 1))
                out.append((h, c, kr, mask))
        return out

    def qk_unit(j, buf, unit):
        h, c, kr, mask = unit
        cols = slice(c * cw, (c + 1) * cw)
        start = pl.multiple_of(j * tkv, tkv)
        s = _dot(kx_ref[0, pl.ds(start, kr), :], qx_ref[h, :, cols])
        if mask is not None:
            s = jnp.where(mask, s, MASKED)
        s_bufs[buf][h, 0:kr, cols] = s
        mx_ref[buf, h, :, cols] = jnp.max(s, axis=0, keepdims=True)

    def softmax_unit(buf, unit):
        h, c, kr, _ = unit
        cols = slice(c * cw, (c + 1) * cw)
        m_old = m_ref[h, :, cols]
        m_new = jnp.maximum(m_old, mx_ref[buf, h, :, cols])
        m_ref[h, :, cols] = m_new
        p = jnp.exp2(s_bufs[buf][h, 0:kr, cols] - m_new).astype(BF16)
        return p, jnp.exp2(m_old - m_new)

    def pv_unit(j, unit, p, alpha):
        h, c, kr, _ = unit
        cols = slice(c * cw, (c + 1) * cw)
        vx = jnp.concatenate(
            [vT_ref[0, j, h * HEAD_DIM:(h + 1) * HEAD_DIM, :][:, 0:kr],
             jnp.ones((BF16_ROWS, kr), BF16)], axis=0)
        acc_ref[h, :, cols] = alpha * acc_ref[h, :, cols] + _dot(vx, p)

    def step(j, buf, cur_d=None, next_d=None, next_j=None, before_next=None):
        cur = units_for(cur_d)
        nxt = units_for(next_d)
        next_j = j + 1 if next_j is None else next_j
        if before_next is not None:
            before_next()
        pending = None
        for i in range(max(len(cur), len(nxt))):
            if i < len(nxt):
                qk_unit(next_j, 1 - buf, nxt[i])
            if pending is not None:
                pv_unit(j, *pending)
                pending = None
            if i < len(cur):
                pending = (cur[i],) + softmax_unit(buf, cur[i])
        if pending is not None:
            pv_unit(j, *pending)

    def diagonal_and_finish(qi):
        n_full = qi * n_diag
        for d in range(n_diag - 1):
            step(n_full + d, d % 2, cur_d=d, next_d=d + 1)
        nq = jnp.minimum(qi + 1, n_q - 1)
        step(n_full + n_diag - 1, (n_diag - 1) % 2, cur_d=n_diag - 1, next_d=None,
             next_j=first_block(nq), before_next=lambda: build_qx(nq))
        outs = []
        for h in heads:
            a = acc_ref[h]
            outs.append(a[0:HEAD_DIM] / a[HEAD_DIM:HEAD_DIM + 1])
        o = jnp.concatenate(outs, axis=0)
        start = qi * tq if isinstance(qi, int) else pl.multiple_of(qi * tq, tq)
        o_ref[0, pl.ds(start, tq), :] = o.T.astype(BF16)
        reset_state()

    def full_pair(i, carry):
        step(2 * i, 0)
        step(2 * i + 1, 1)
        return carry

    def later_tile(qi, carry):
        n_full = qi * n_diag
        lax.fori_loop(lax.shift_right_logical(first_block(qi), 1),
                      qi * (n_diag // 2) - 1, full_pair, 0)
        step(n_full - 2, 0)
        step(n_full - 1, 1, next_d=0)
        diagonal_and_finish(qi)
        return carry

    rowx = lax.broadcasted_iota(jnp.int32, (LANES, tq), 0)
    for h in heads:
        head = pair * HEADS_PER_STEP + h
        ones = rowx == head
        for j in range(1, N_SPLIT):
            ones = ones | (rowx == head + j * N_HEADS)
        qx_ref[h, LANES:PAIR_W, :] = jnp.where(ones, 1.0, 0.0).astype(BF16)
    reset_state()
    build_qx(0)
    for unit in units_for(0):
        qk_unit(0, 0, unit)
    diagonal_and_finish(0)
    lax.fori_loop(1, n_q, later_tile, 0)


def _attn_call(first, qT, kx, vT, casts):
    b, s = kx.shape[0], kx.shape[1]
    tq, tkv = TQ, TKV
    n_pairs = N_HEADS // HEADS_PER_STEP
    n_tiles = s // PRUNE_TILE
    kern = functools.partial(_attn_kernel, tq=tq, tkv=tkv, cw=CW, tile=PRUNE_TILE,
                             n_tiles=n_tiles, n_q=s // tq, n_cast=len(casts))
    r_in, r_out, r_shapes = _riders(casts, b * n_pairs, lambda bi, p, first: bi * n_pairs + p)
    grid_spec = pltpu.PrefetchScalarGridSpec(
        num_scalar_prefetch=1,
        grid=(b, n_pairs),
        in_specs=[
            pl.BlockSpec((1, n_tiles, LANES, PRUNE_TILE), lambda bi, p, first: (bi, 0, p, 0)),
            pl.BlockSpec((1, s, PAIR_W), lambda bi, p, first: (bi, 0, p)),
            pl.BlockSpec((1, s // tkv, LANES, tkv), lambda bi, p, first: (bi, 0, p, 0)),
        ] + r_in,
        out_specs=[pl.BlockSpec((1, s, LANES), lambda bi, p, first: (bi, 0, p))] + r_out,
        scratch_shapes=[
            pltpu.VMEM((HEADS_PER_STEP, PAIR_W, tq), BF16),
            pltpu.VMEM((HEADS_PER_STEP, tkv, tq), F32),
            pltpu.VMEM((HEADS_PER_STEP, tkv, tq), F32),
            pltpu.VMEM((2, HEADS_PER_STEP, 1, tq), F32),
            pltpu.VMEM((HEADS_PER_STEP, 1, tq), F32),
            pltpu.VMEM((HEADS_PER_STEP, HEAD_DIM + BF16_ROWS, tq), F32),
        ],
    )
    return pl.pallas_call(
        kern,
        grid_spec=grid_spec,
        out_shape=[jax.ShapeDtypeStruct((b, s, ATTN_WIDTH), BF16)] + r_shapes,
        compiler_params=pltpu.CompilerParams(
            dimension_semantics=("arbitrary", "arbitrary"),
            vmem_limit_bytes=VMEM_LIMIT),
        name="attn",
    )(first, qT, kx, vT, *[c[0] for c in casts])


def _mix_kernel(*refs, tm, sub, tiles_per_seq, d_model, n_cast):
    (x_ref, yb_ref, gpre_ref, gpost_ref, wcg_ref, cw_ref, wb_ref, wo_ref), refs = (
        refs[:8], refs[8:])
    cast_in, o_ref, refs = refs[:n_cast], refs[n_cast], refs[n_cast + 1:]
    cast_out, (ztail_ref,) = refs[:n_cast], refs[n_cast:]
    parts = [slice(r * sub, (r + 1) * sub) for r in range(tm // sub)]
    xs = [x_ref[rows, :] for rows in parts]
    projs = []
    for x in xs:
        h = _rmsnorm(x, gpre_ref[...]).astype(BF16)
        projs.append(_dot_nt(h, wcg_ref[...]))
    _cast_riders(cast_in, cast_out)

    first = pl.program_id(0) % tiles_per_seq == 0
    zh = jnp.where(first, 0.0, ztail_ref[...])
    branches = []
    for rows, proj in zip(parts, projs):
        gate_b = proj[:, 0:CONV_CH]
        z = proj[:, CONV_CH:2 * CONV_CH] * proj[:, 2 * CONV_CH:3 * CONV_CH]
        zz = jnp.concatenate([zh, z], axis=0)
        zh = z[sub - HALO:sub]
        conv = cw_ref[CONV_K - 1:CONV_K, :] * z
        for back in range(1, CONV_K):
            shifted = pltpu.roll(zz, back, 0)[HALO:]
            conv = conv + cw_ref[CONV_K - 1 - back:CONV_K - back, :] * shifted
        ya = (gate_b * conv).astype(BF16)
        branches.append((_dot(ya, wb_ref[0:CONV_CH, :]),
                         _dot(yb_ref[rows, :], wb_ref[CONV_CH:CONV_CH + ATTN_WIDTH, :])))
    ztail_ref[...] = zh

    g0 = 3 * CONV_CH
    ms = []
    for proj, (ya_d, yb_d) in zip(projs, branches):
        gc = jax.nn.sigmoid(proj[:, g0:g0 + d_model])
        ga = jax.nn.sigmoid(proj[:, g0 + d_model:g0 + 2 * d_model])
        merged = (gc * ya_d + ga * yb_d).astype(BF16)
        ms.append(_dot(merged, wo_ref[...]))
    for rows, x, m in zip(parts, xs, ms):
        o_ref[rows, :] = x + _rmsnorm(m, gpost_ref[...])


def _mix_call(x2, yb2, gpre, gpost, wcg, cw, wb, wo, seq_len, casts):
    n, d = x2.shape
    tm = TM_MIX
    kern = functools.partial(_mix_kernel, tm=tm, sub=SUB_ROWS, tiles_per_seq=seq_len // tm,
                             d_model=d, n_cast=len(casts))
    r_in, r_out, r_shapes = _riders(casts, n // tm, lambda i: i)
    return pl.pallas_call(
        kern,
        grid=(n // tm,),
        in_specs=[
            pl.BlockSpec((tm, d), lambda i: (i, 0)),
            pl.BlockSpec((tm, ATTN_WIDTH), lambda i: (i, 0)),
            _const_spec(gpre.shape), _const_spec(gpost.shape), _const_spec(wcg.shape),
            _const_spec(cw.shape), _const_spec(wb.shape), _const_spec(wo.shape),
        ] + r_in,
        out_specs=[pl.BlockSpec((tm, d), lambda i: (i, 0))] + r_out,
        out_shape=[jax.ShapeDtypeStruct((n, d), F32)] + r_shapes,
        scratch_shapes=[pltpu.VMEM((HALO, CONV_CH), F32)],
        compiler_params=pltpu.CompilerParams(
            dimension_semantics=("arbitrary",), vmem_limit_bytes=VMEM_LIMIT),
        name="mix",
    )(x2, yb2, gpre, gpost, wcg, cw, wb, wo, *[c[0] for c in casts])


def _ffn_kernel(x_ref, gpre_ref, gpost_ref, wgu_ref, wd_ref, o_ref, *, hidden, sub):
    parts = [slice(r * sub, (r + 1) * sub) for r in range(x_ref.shape[0] // sub)]
    xs = [x_ref[rows, :] for rows in parts]
    gus = []
    for x in xs:
        h = _rmsnorm(x, gpre_ref[...]).astype(BF16)
        gus.append(_dot(h, wgu_ref[...]))
    fs = []
    for gu in gus:
        g = gu[:, 0:hidden]
        u = gu[:, hidden:2 * hidden]
        a = (g * jax.nn.sigmoid(g) * u).astype(BF16)
        fs.append(_dot(a, wd_ref[...]))
    for rows, x, f in zip(parts, xs, fs):
        o_ref[rows, :] = x + _rmsnorm(f, gpost_ref[...])


def _ffn_call(x2, gpre, gpost, wgu, wd):
    n, d = x2.shape
    tm = TM_FFN
    hidden = wd.shape[0]
    kern = functools.partial(_ffn_kernel, hidden=hidden, sub=SUB_ROWS)
    return pl.pallas_call(
        kern,
        grid=(n // tm,),
        in_specs=[
            pl.BlockSpec((tm, d), lambda i: (i, 0)),
            _const_spec(gpre.shape), _const_spec(gpost.shape),
            _const_spec(wgu.shape), _const_spec(wd.shape),
        ],
        out_specs=pl.BlockSpec((tm, d), lambda i: (i, 0)),
        out_shape=jax.ShapeDtypeStruct((n, d), F32),
        compiler_params=pltpu.CompilerParams(
            dimension_semantics=("arbitrary",), vmem_limit_bytes=VMEM_LIMIT),
        name="ffn",
    )(x2, gpre, gpost, wgu, wd)


def _norm_indicator():
    ind = np.zeros((2 * ATTN_WIDTH, LANES), np.float32)
    for which in range(2):
        for col in range(ATTN_WIDTH):
            ind[which * ATTN_WIDTH + col, which * N_HEADS + col // HEAD_DIM] = 1.0
    return jnp.asarray(ind, BF16)


def kernel(x, norm_mix_pre, norm_mix_post, w_in, b_f, conv_w, w_branch, w_out,
           norm_ffn_pre, norm_ffn_post, w_gate_up, w_down):
    b, s, d = x.shape
    depth = w_in.shape[0]
    a = ATTN_WIDTH
    tri = jnp.asarray(np.tril(np.ones((SUB_ROWS, SUB_ROWS), np.float32)), BF16)
    ind = _norm_indicator()
    scale = LOG2E / math.sqrt(HEAD_DIM)

    for l in range(depth):
        bfp = jnp.pad(b_f[l], (0, LANES - N_HEADS)).reshape(1, LANES)
        wT = w_in[l].T
        rest0 = 3 * a + N_HEADS
        qT, kx, vT, first = _qkv_call(x, norm_mix_pre[l].reshape(1, d), wT, bfp, tri, ind, scale)
        yb, wb, wo, wcg = _attn_call(
            first[:, :, 0, 0:N_HEADS].reshape(-1), qT, kx, vT,
            [(w_branch[l].reshape(CONV_CH + a, d), 0, CONV_CH + a), (w_out[l], 0, d),
             (wT, rest0, wT.shape[0] - rest0)])
        x2, wgu, wd = _mix_call(
            x.reshape(b * s, d), yb.reshape(b * s, a),
            norm_mix_pre[l].reshape(1, d), norm_mix_post[l].reshape(1, d),
            wcg, conv_w[l], wb, wo, s,
            [(w_gate_up[l], 0, w_gate_up.shape[1]), (w_down[l], 0, w_down.shape[1])])
        x2 = _ffn_call(x2, norm_ffn_pre[l].reshape(1, d), norm_ffn_post[l].reshape(1, d),
                       wgu, wd)
        x = x2.reshape(b, s, d)
    return x
```

```python
import functools
import math

import jax
import jax.numpy as jnp
import numpy as np
from jax import lax
from jax.experimental import pallas as pl
from jax.experimental.pallas import tpu as pltpu

F32 = jnp.float32
BF16 = jnp.bfloat16

RMS_EPS = 1e-6
N_HEADS = 8
HEAD_DIM = 64
ATTN_WIDTH = N_HEADS * HEAD_DIM
CONV_CH = 512
CONV_K = 3
LANES = 128
BF16_ROWS = 16
HEADS_PER_STEP = 2
PAIR_W = 2 * LANES
N_SPLIT = 3
MASKED = -1e30
LOG2E = math.log2(math.e)
PRUNE_GAP = 106.0 * LOG2E
NORM_SLACK = 4.0 * 1.04
VMEM_LIMIT = 60 * 1024 * 1024

TM_QKV = 1024
PRUNE_TILE = 512
TQ = 1024
TKV = 512
CW = 256
TM_MIX = 1024
TM_FFN = 1024
SUB_ROWS = 256
HALO = 8


def _rmsnorm(x, g):
    ms = jnp.mean(x * x, axis=-1, keepdims=True)
    return (x * lax.rsqrt(ms + RMS_EPS)) * g


def _split3(v):
    parts = []
    r = v
    for _ in range(N_SPLIT):
        p = r.astype(BF16)
        parts.append(p)
        r = r - p.astype(F32)
    return parts


def _dot(a, b):
    return jnp.dot(a, b, preferred_element_type=F32)


def _const_spec(shape):
    nd = len(shape)
    return pl.BlockSpec(shape, lambda *_: (0,) * nd, pipeline_mode=pl.Buffered(1))


def _qkv_kernel(x_ref, g_ref, w_ref, bf_ref, tri_ref, ind_ref, qT_ref, kx_ref, vT_ref,
                first_ref, w16_ref, carry_ref, kmax_ref, hist_ref, *, tm, sub, pt, tkv, scale):
    i = pl.program_id(1)

    a = ATTN_WIDTH

    @pl.when((pl.program_id(0) == 0) & (i == 0))
    def _():
        w16_ref[:, 0:a] = (w_ref[0:a, :] * scale).T.astype(BF16)
        w16_ref[:, a:] = w_ref[a:, :].T.astype(BF16)

    @pl.when(i == 0)
    def _():
        carry_ref[...] = jnp.zeros_like(carry_ref)
        kmax_ref[...] = jnp.zeros_like(kmax_ref)
        hist_ref[...] = jnp.zeros_like(hist_ref)

    parts = [slice(r * sub, (r + 1) * sub) for r in range(tm // sub)]
    head_lane = lax.broadcasted_iota(jnp.int32, (sub, LANES), 1) < N_HEADS

    def pack3(v):
        pieces = _split3(jnp.where(head_lane, v, 0.0))
        out = pieces[0].astype(F32)
        for j in range(1, N_SPLIT):
            out = out + pltpu.roll(pieces[j].astype(F32), j * N_HEADS, 1)
        return out.astype(BF16)

    hs, vfs = [], []
    for rows in parts:
        hs.append(_rmsnorm(x_ref[0, rows, :], g_ref[...]).astype(BF16))
        vfs.append(_dot(hs[-1], w16_ref[:, 2 * a:3 * a + LANES]))
    qs, css = [], []
    for rows, h, vf in zip(parts, hs, vfs):
        r0 = rows.start
        vT_ref[0, r0 // tkv, :, r0 % tkv:r0 % tkv + sub] = vf[:, 0:a].T.astype(BF16)
        f = vf[:, a:a + LANES] + bf_ref[...]
        logf = (jnp.minimum(f, 0.0) - jnp.log1p(jnp.exp(-jnp.abs(f)))) * LOG2E
        qf32 = _dot(h, w16_ref[:, 0:a])
        qs.append(qf32.astype(BF16))
        qT_ref[0, r0 // pt, :, r0 % pt:r0 % pt + sub] = qf32.T.astype(BF16)
        css.append(_dot(tri_ref[...], pack3(logf)))
    ks, norms = [], []
    for h, q in zip(hs, qs):
        ks.append(_dot(h, w16_ref[:, a:2 * a]).astype(BF16))
        qf = q.astype(F32)
        kf = ks[-1].astype(F32)
        sq = jnp.concatenate([qf * qf, kf * kf], axis=1).astype(BF16)
        norms.append(jnp.max(_dot(sq, ind_ref[...]), axis=0, keepdims=True))

    c_end = carry_ref[...]
    c_ends = []
    for rows, cs, k in zip(parts, css, ks):
        c = cs + c_end
        for j in range(1, N_SPLIT):
            c = c + pltpu.roll(cs, LANES - j * N_HEADS, 1)
        c_end = c[sub - 1:sub]
        c_ends.append(c_end)
        bias = pack3(-c)
        for p in range(N_HEADS // HEADS_PER_STEP):
            kx_ref[0, rows, p * PAIR_W:p * PAIR_W + LANES] = k[:, p * LANES:(p + 1) * LANES]
            kx_ref[0, rows, p * PAIR_W + LANES:(p + 1) * PAIR_W] = bias

    per = pt // sub
    c_prev = carry_ref[...]
    for t in range(tm // pt):
        gi = i * (tm // pt) + t
        norm2 = functools.reduce(jnp.maximum, norms[t * per:(t + 1) * per])
        kmax = jnp.maximum(kmax_ref[...], norm2)
        kmax_ref[...] = kmax
        qk2 = NORM_SLACK * norm2 * pltpu.roll(kmax, LANES - N_HEADS, 1)
        gap = hist_ref[...] - c_prev - PRUNE_GAP
        tile = lax.broadcasted_iota(jnp.int32, gap.shape, 0)
        skip = (tile < gi) & (gap > 0.0) & (qk2 < gap * gap)
        gi_f = gi.astype(F32)
        kept = jnp.where(skip, gi_f, jnp.minimum(tile.astype(F32), gi_f))
        first_ref[0, t] = jnp.min(kept, axis=0, keepdims=True).astype(jnp.int32)
        c_prev = c_ends[(t + 1) * per - 1]
        hist_ref[pl.ds(gi, 1), :] = c_prev
    carry_ref[...] = c_prev


def _riders(items, steps, step_of):
    in_specs, out_specs, out_shapes = [], [], []
    for arr, row0, rows, transpose in items:
        cols = arr.shape[1]
        align = LANES if transpose else BF16_ROWS
        n_blocks = max(n for n in range(1, steps + 1)
                       if rows % n == 0 and (rows // n) % align == 0)
        blk = rows // n_blocks
        assert row0 % 8 == 0 and not (transpose and blk == cols)

        def block_of(*g, last=n_blocks - 1):
            return jnp.minimum(step_of(*g), last)

        if row0 % blk == 0:
            in_specs.append(pl.BlockSpec(
                (blk, cols), lambda *g, o=row0 // blk, f=block_of: (f(*g) + o, 0)))
        else:
            in_specs.append(pl.BlockSpec(
                (pl.Element(blk), pl.Element(cols)),
                lambda *g, r=row0, n=blk, f=block_of: (pl.multiple_of(r + f(*g) * n, 8), 0)))
        if transpose:
            out_specs.append(pl.BlockSpec((cols, blk), lambda *g, f=block_of: (0, f(*g))))
            out_shapes.append(jax.ShapeDtypeStruct((cols, rows), BF16))
        else:
            out_specs.append(pl.BlockSpec((blk, cols), lambda *g, f=block_of: (f(*g), 0)))
            out_shapes.append(jax.ShapeDtypeStruct((rows, cols), BF16))
    return in_specs, out_specs, out_shapes


def _cast_riders(cast_in, cast_out):
    for src, dst in zip(cast_in, cast_out):
        v = src[...]
        dst[...] = (v.T if dst.shape != src.shape else v).astype(BF16)


def _qkv_call(x, g, wT, bfp, tri, ind, scale):
    b, s, d = x.shape
    tm, pt, tkv = TM_QKV, PRUNE_TILE, TKV
    n_pairs = N_HEADS // HEADS_PER_STEP
    n_tiles = s // tm
    own_rows = 3 * ATTN_WIDTH + LANES
    kern = functools.partial(_qkv_kernel, tm=tm, sub=SUB_ROWS, pt=pt, tkv=tkv, scale=scale)
    return pl.pallas_call(
        kern,
        grid=(b, n_tiles),
        in_specs=[
            pl.BlockSpec((1, tm, d), lambda bi, i: (bi, i, 0)),
            _const_spec(g.shape),
            pl.BlockSpec((own_rows, d), lambda bi, i: (0, 0), pipeline_mode=pl.Buffered(1)),
            _const_spec(bfp.shape),
            _const_spec(tri.shape), _const_spec(ind.shape),
        ],
        out_specs=[
            pl.BlockSpec((1, tm // pt, ATTN_WIDTH, pt), lambda bi, i: (bi, i, 0, 0)),
            pl.BlockSpec((1, tm, n_pairs * PAIR_W), lambda bi, i: (bi, i, 0)),
            pl.BlockSpec((1, tm // tkv, ATTN_WIDTH, tkv), lambda bi, i: (bi, i, 0, 0)),
            pl.BlockSpec((1, tm // pt, 1, LANES), lambda bi, i: (bi, i, 0, 0)),
        ],
        out_shape=[
            jax.ShapeDtypeStruct((b, s // pt, ATTN_WIDTH, pt), BF16),
            jax.ShapeDtypeStruct((b, s, n_pairs * PAIR_W), BF16),
            jax.ShapeDtypeStruct((b, s // tkv, ATTN_WIDTH, tkv), BF16),
            jax.ShapeDtypeStruct((b, s // pt, 1, LANES), jnp.int32),
        ],
        scratch_shapes=[
            pltpu.VMEM((d, own_rows), BF16),
            pltpu.VMEM((1, LANES), F32),
            pltpu.VMEM((1, LANES), F32),
            pltpu.VMEM((s // pt, LANES), F32),
        ],
        compiler_params=pltpu.CompilerParams(
            dimension_semantics=("arbitrary", "arbitrary"),
            vmem_limit_bytes=VMEM_LIMIT),
        name="qkv",
    )(x, g, wT, bfp, tri, ind)


def _attn_kernel(*refs, tq, tkv, cw, tile, n_tiles, n_q, n_cast):
    (first_ref, qT_ref, kx_ref, vT_ref), refs = refs[:4], refs[4:]
    cast_in, o_ref, refs = refs[:n_cast], refs[n_cast], refs[n_cast + 1:]
    cast_out, (qx_ref, s0_ref, s1_ref, mx_ref, m_ref, acc_ref) = refs[:n_cast], refs[n_cast:]
    _cast_riders(cast_in, cast_out)
    bi, pair = pl.program_id(0), pl.program_id(1)
    heads = range(HEADS_PER_STEP)
    n_diag = tq // tkv
    sub_q = tq // tile
    s_bufs = (s0_ref, s1_ref)
    row = lax.broadcasted_iota(jnp.int32, (LANES, tile), 0)

    def build_qx(qi):
        for d in range(sub_q):
            qT = qT_ref[0, qi * sub_q + d].astype(F32)
            for h in heads:
                mine = (row >= h * HEAD_DIM) & (row < (h + 1) * HEAD_DIM)
                qx_ref[h, 0:LANES, d * tile:(d + 1) * tile] = (
                    jnp.where(mine, qT, 0.0).astype(BF16))

    def reset_state():
        m_ref[...] = jnp.full_like(m_ref, MASKED)
        acc_ref[...] = jnp.zeros_like(acc_ref)

    def first_block(qi):
        t0 = n_tiles
        for d in range(sub_q):
            for h in heads:
                t0 = jnp.minimum(t0, first_ref[(bi * n_tiles + qi * sub_q + d) * N_HEADS
                                               + pair * HEADS_PER_STEP + h])
        j0 = jnp.minimum(t0 * (tile // tkv), qi * n_diag - 2)
        return 2 * lax.shift_right_logical(j0, 1)

    def units_for(d):
        out = []
        for h in heads:
            for c in range(tq // cw):
                if d is None:
                    out.append((h, c, tkv, None))
                    continue
                if (c + 1) * cw <= d * tkv:
                    continue
                kr = min(tkv, (c + 1) * cw - d * tkv)
                masked = c * cw < d * tkv + kr - 1
                mask = None
                if masked:
                    mask = (lax.broadcasted_iota(jnp.int32, (kr, cw), 0) + (d * tkv - c * cw)
                            <= lax.broadcasted_iota(jnp.int32, (kr, cw), 1))
                out.append((h, c, kr, mask))
        return out

    def qk_unit(j, buf, unit):
        h, c, kr, mask = unit
        cols = slice(c * cw, (c + 1) * cw)
        start = pl.multiple_of(j * tkv, tkv)
        s = _dot(kx_ref[0, pl.ds(start, kr), :], qx_ref[h, :, cols])
        if mask is not None:
            s = jnp.where(mask, s, MASKED)
        s_bufs[buf][h, 0:kr, cols] = s
        mx_ref[buf, h, :, cols] = jnp.max(s, axis=0, keepdims=True)

    def softmax_unit(buf, unit):
        h, c, kr, _ = unit
        cols = slice(c * cw, (c + 1) * cw)
        m_old = m_ref[h, :, cols]
        m_new = jnp.maximum(m_old, mx_ref[buf, h, :, cols])
        m_ref[h, :, cols] = m_new
        p = jnp.exp2(s_bufs[buf][h, 0:kr, cols] - m_new).astype(BF16)
        return p, jnp.exp2(m_old - m_new)

    def pv_unit(j, unit, p, alpha):
        h, c, kr, _ = unit
        cols = slice(c * cw, (c + 1) * cw)
        vx = jnp.concatenate(
            [vT_ref[0, j, h * HEAD_DIM:(h + 1) * HEAD_DIM, :][:, 0:kr],
             jnp.ones((BF16_ROWS, kr), BF16)], axis=0)
        acc_ref[h, :, cols] = alpha * acc_ref[h, :, cols] + _dot(vx, p)

    def step(j, buf, cur_d=None, next_d=None, next_j=None, before_next=None):
        cur = units_for(cur_d)
        nxt = units_for(next_d)
        next_j = j + 1 if next_j is None else next_j
        if before_next is not None:
            before_next()
        pending = None
        for i in range(max(len(cur), len(nxt))):
            if i < len(nxt):
                qk_unit(next_j, 1 - buf, nxt[i])
            if pending is not None:
                pv_unit(j, *pending)
                pending = None
            if i < len(cur):
                pending = (cur[i],) + softmax_unit(buf, cur[i])
        if pending is not None:
            pv_unit(j, *pending)

    def diagonal_and_finish(qi):
        n_full = qi * n_diag
        for d in range(n_diag - 1):
            step(n_full + d, d % 2, cur_d=d, next_d=d + 1)
        nq = jnp.minimum(qi + 1, n_q - 1)
        step(n_full + n_diag - 1, (n_diag - 1) % 2, cur_d=n_diag - 1, next_d=None,
             next_j=first_block(nq), before_next=lambda: build_qx(nq))
        outs = []
        for h in heads:
            a = acc_ref[h]
            outs.append(a[0:HEAD_DIM] / a[HEAD_DIM:HEAD_DIM + 1])
        o = jnp.concatenate(outs, axis=0)
        start = qi * tq if isinstance(qi, int) else pl.multiple_of(qi * tq, tq)
        o_ref[0, pl.ds(start, tq), :] = o.T.astype(BF16)
        reset_state()

    def full_pair(i, carry):
        step(2 * i, 0)
        step(2 * i + 1, 1)
        return carry

    def later_tile(qi, carry):
        n_full = qi * n_diag
        lax.fori_loop(lax.shift_right_logical(first_block(qi), 1),
                      qi * (n_diag // 2) - 1, full_pair, 0)
        step(n_full - 2, 0)
        step(n_full - 1, 1, next_d=0)
        diagonal_and_finish(qi)
        return carry

    rowx = lax.broadcasted_iota(jnp.int32, (LANES, tq), 0)
    for h in heads:
        head = pair * HEADS_PER_STEP + h
        ones = rowx == head
        for j in range(1, N_SPLIT):
            ones = ones | (rowx == head + j * N_HEADS)
        qx_ref[h, LANES:PAIR_W, :] = jnp.where(ones, 1.0, 0.0).astype(BF16)
    reset_state()
    build_qx(0)
    for unit in units_for(0):
        qk_unit(0, 0, unit)
    diagonal_and_finish(0)
    lax.fori_loop(1, n_q, later_tile, 0)


def _attn_call(first, qT, kx, vT, casts):
    b, s = kx.shape[0], kx.shape[1]
    tq, tkv = TQ, TKV
    n_pairs = N_HEADS // HEADS_PER_STEP
    n_tiles = s // PRUNE_TILE
    kern = functools.partial(_attn_kernel, tq=tq, tkv=tkv, cw=CW, tile=PRUNE_TILE,
                             n_tiles=n_tiles, n_q=s // tq, n_cast=len(casts))
    r_in, r_out, r_shapes = _riders(casts, b * n_pairs, lambda bi, p, first: bi * n_pairs + p)
    grid_spec = pltpu.PrefetchScalarGridSpec(
        num_scalar_prefetch=1,
        grid=(b, n_pairs),
        in_specs=[
            pl.BlockSpec((1, n_tiles, LANES, PRUNE_TILE), lambda bi, p, first: (bi, 0, p, 0)),
            pl.BlockSpec((1, s, PAIR_W), lambda bi, p, first: (bi, 0, p)),
            pl.BlockSpec((1, s // tkv, LANES, tkv), lambda bi, p, first: (bi, 0, p, 0)),
        ] + r_in,
        out_specs=[pl.BlockSpec((1, s, LANES), lambda bi, p, first: (bi, 0, p))] + r_out,
        scratch_shapes=[
            pltpu.VMEM((HEADS_PER_STEP, PAIR_W, tq), BF16),
            pltpu.VMEM((HEADS_PER_STEP, tkv, tq), F32),
            pltpu.VMEM((HEADS_PER_STEP, tkv, tq), F32),
            pltpu.VMEM((2, HEADS_PER_STEP, 1, tq), F32),
            pltpu.VMEM((HEADS_PER_STEP, 1, tq), F32),
            pltpu.VMEM((HEADS_PER_STEP, HEAD_DIM + BF16_ROWS, tq), F32),
        ],
    )
    return pl.pallas_call(
        kern,
        grid_spec=grid_spec,
        out_shape=[jax.ShapeDtypeStruct((b, s, ATTN_WIDTH), BF16)] + r_shapes,
        compiler_params=pltpu.CompilerParams(
            dimension_semantics=("arbitrary", "arbitrary"),
            vmem_limit_bytes=VMEM_LIMIT),
        name="attn",
    )(first, qT, kx, vT, *[c[0] for c in casts])


def _mix_kernel(*refs, tm, sub, tiles_per_seq, d_model, n_cast):
    (x_ref, yb_ref, gpre_ref, gpost_ref, wcg_ref, cw_ref, wb_ref, wo_ref), refs = (
        refs[:8], refs[8:])
    cast_in, o_ref, refs = refs[:n_cast], refs[n_cast], refs[n_cast + 1:]
    cast_out, (ztail_ref,) = refs[:n_cast], refs[n_cast:]
    parts = [slice(r * sub, (r + 1) * sub) for r in range(tm // sub)]
    xs = [x_ref[rows, :] for rows in parts]
    projs = []
    for x in xs:
        h = _rmsnorm(x, gpre_ref[...]).astype(BF16)
        projs.append(_dot(h, wcg_ref[...]))
    _cast_riders(cast_in, cast_out)

    first = pl.program_id(0) % tiles_per_seq == 0
    zh = jnp.where(first, 0.0, ztail_ref[...])
    branches = []
    for rows, proj in zip(parts, projs):
        gate_b = proj[:, 0:CONV_CH]
        z = proj[:, CONV_CH:2 * CONV_CH] * proj[:, 2 * CONV_CH:3 * CONV_CH]
        zz = jnp.concatenate([zh, z], axis=0)
        zh = z[sub - HALO:sub]
        conv = cw_ref[CONV_K - 1:CONV_K, :] * z
        for back in range(1, CONV_K):
            shifted = pltpu.roll(zz, back, 0)[HALO:]
            conv = conv + cw_ref[CONV_K - 1 - back:CONV_K - back, :] * shifted
        ya = (gate_b * conv).astype(BF16)
        branches.append((_dot(ya, wb_ref[0:CONV_CH, :]),
                         _dot(yb_ref[rows, :], wb_ref[CONV_CH:CONV_CH + ATTN_WIDTH, :])))
    ztail_ref[...] = zh

    g0 = 3 * CONV_CH
    ms = []
    for proj, (ya_d, yb_d) in zip(projs, branches):
        gc = jax.nn.sigmoid(proj[:, g0:g0 + d_model])
        ga = jax.nn.sigmoid(proj[:, g0 + d_model:g0 + 2 * d_model])
        merged = (gc * ya_d + ga * yb_d).astype(BF16)
        ms.append(_dot(merged, wo_ref[...]))
    for rows, x, m in zip(parts, xs, ms):
        o_ref[rows, :] = x + _rmsnorm(m, gpost_ref[...])


def _mix_call(x2, yb2, gpre, gpost, wcg, cw, wb, wo, seq_len, casts):
    n, d = x2.shape
    tm = TM_MIX
    kern = functools.partial(_mix_kernel, tm=tm, sub=SUB_ROWS, tiles_per_seq=seq_len // tm,
                             d_model=d, n_cast=len(casts))
    r_in, r_out, r_shapes = _riders(casts, n // tm, lambda i: i)
    return pl.pallas_call(
        kern,
        grid=(n // tm,),
        in_specs=[
            pl.BlockSpec((tm, d), lambda i: (i, 0)),
            pl.BlockSpec((tm, ATTN_WIDTH), lambda i: (i, 0)),
            _const_spec(gpre.shape), _const_spec(gpost.shape), _const_spec(wcg.shape),
            _const_spec(cw.shape), _const_spec(wb.shape), _const_spec(wo.shape),
        ] + r_in,
        out_specs=[pl.BlockSpec((tm, d), lambda i: (i, 0))] + r_out,
        out_shape=[jax.ShapeDtypeStruct((n, d), F32)] + r_shapes,
        scratch_shapes=[pltpu.VMEM((HALO, CONV_CH), F32)],
        compiler_params=pltpu.CompilerParams(
            dimension_semantics=("arbitrary",), vmem_limit_bytes=VMEM_LIMIT),
        name="mix",
    )(x2, yb2, gpre, gpost, wcg, cw, wb, wo, *[c[0] for c in casts])


def _ffn_kernel(x_ref, gpre_ref, gpost_ref, wgu_ref, wd_ref, o_ref, *, hidden, sub):
    parts = [slice(r * sub, (r + 1) * sub) for r in range(x_ref.shape[0] // sub)]
    xs = [x_ref[rows, :] for rows in parts]
    gus = []
    for x in xs:
        h = _rmsnorm(x, gpre_ref[...]).astype(BF16)
        gus.append(_dot(h, wgu_ref[...]))
    fs = []
    for gu in gus:
        g = gu[:, 0:hidden]
        u = gu[:, hidden:2 * hidden]
        a = (g * jax.nn.sigmoid(g) * u).astype(BF16)
        fs.append(_dot(a, wd_ref[...]))
    for rows, x, f in zip(parts, xs, fs):
        o_ref[rows, :] = x + _rmsnorm(f, gpost_ref[...])


def _ffn_call(x2, gpre, gpost, wgu, wd):
    n, d = x2.shape
    tm = TM_FFN
    hidden = wd.shape[0]
    kern = functools.partial(_ffn_kernel, hidden=hidden, sub=SUB_ROWS)
    return pl.pallas_call(
        kern,
        grid=(n // tm,),
        in_specs=[
            pl.BlockSpec((tm, d), lambda i: (i, 0)),
            _const_spec(gpre.shape), _const_spec(gpost.shape),
            _const_spec(wgu.shape), _const_spec(wd.shape),
        ],
        out_specs=pl.BlockSpec((tm, d), lambda i: (i, 0)),
        out_shape=jax.ShapeDtypeStruct((n, d), F32),
        compiler_params=pltpu.CompilerParams(
            dimension_semantics=("arbitrary",), vmem_limit_bytes=VMEM_LIMIT),
        name="ffn",
    )(x2, gpre, gpost, wgu, wd)


def _norm_indicator():
    ind = np.zeros((2 * ATTN_WIDTH, LANES), np.float32)
    for which in range(2):
        for col in range(ATTN_WIDTH):
            ind[which * ATTN_WIDTH + col, which * N_HEADS + col // HEAD_DIM] = 1.0
    return jnp.asarray(ind, BF16)


def kernel(x, norm_mix_pre, norm_mix_post, w_in, b_f, conv_w, w_branch, w_out,
           norm_ffn_pre, norm_ffn_post, w_gate_up, w_down):
    b, s, d = x.shape
    depth = w_in.shape[0]
    a = ATTN_WIDTH
    tri = jnp.asarray(np.tril(np.ones((SUB_ROWS, SUB_ROWS), np.float32)), BF16)
    ind = _norm_indicator()
    scale = LOG2E / math.sqrt(HEAD_DIM)

    for l in range(depth):
        bfp = jnp.pad(b_f[l], (0, LANES - N_HEADS)).reshape(1, LANES)
        wT = w_in[l].T
        rest0 = 3 * a + N_HEADS
        qT, kx, vT, first = _qkv_call(x, norm_mix_pre[l].reshape(1, d), wT, bfp, tri, ind, scale)
        yb, wb, wo, wcg = _attn_call(
            first[:, :, 0, 0:N_HEADS].reshape(-1), qT, kx, vT,
            [(w_branch[l].reshape(CONV_CH + a, d), 0, CONV_CH + a, False),
             (w_out[l], 0, d, False),
             (wT, rest0, wT.shape[0] - rest0, True)])
        x2, wgu, wd = _mix_call(
            x.reshape(b * s, d), yb.reshape(b * s, a),
            norm_mix_pre[l].reshape(1, d), norm_mix_post[l].reshape(1, d),
            wcg, conv_w[l], wb, wo, s,
            [(w_gate_up[l], 0, w_gate_up.shape[1], False),
             (w_down[l], 0, w_down.shape[1], False)])
        x2 = _ffn_call(x2, norm_ffn_pre[l].reshape(1, d), norm_ffn_post[l].reshape(1, d),
                       wgu, wd)
        x = x2.reshape(b, s, d)
    return x
```

```python
import functools
import math

import jax
import jax.numpy as jnp
import numpy as np
from jax import lax
from jax.experimental import pallas as pl
from jax.experimental.pallas import tpu as pltpu

F32 = jnp.float32
BF16 = jnp.bfloat16

RMS_EPS = 1e-6
N_HEADS = 8
HEAD_DIM = 64
ATTN_WIDTH = N_HEADS * HEAD_DIM
CONV_CH = 512
CONV_K = 3
LANES = 128
F32_ROWS = 8
BF16_ROWS = 16
HEADS_PER_STEP = 2
PAIR_W = 2 * LANES
N_SPLIT = 3
MASKED = -1e30
LOG2E = math.log2(math.e)
PRUNE_GAP = 106.0 * LOG2E
NORM_SLACK = 4.0 * 1.04
VMEM_LIMIT = 60 * 1024 * 1024

TM_QKV = 1024
PRUNE_TILE = 512
TQ = 1024
TKV = 512
CW = 256
TM_MIX = 1024
TM_FFN = 1024
SUB_ROWS = 256
HALO = F32_ROWS


def _rmsnorm(x, g):
    ms = jnp.mean(x * x, axis=-1, keepdims=True)
    return (x * lax.rsqrt(ms + RMS_EPS)) * g


def _split3(v):
    parts = []
    r = v
    for _ in range(N_SPLIT):
        p = r.astype(BF16)
        parts.append(p)
        r = r - p.astype(F32)
    return parts


def _dot(a, b):
    return jnp.dot(a, b, preferred_element_type=F32)


def _const_spec(shape):
    nd = len(shape)
    return pl.BlockSpec(shape, lambda *_: (0,) * nd, pipeline_mode=pl.Buffered(1))


def _qkv_kernel(x_ref, g_ref, w_ref, bf_ref, tri_ref, ind_ref, qT_ref, kx_ref, vT_ref,
                first_ref, w16_ref, carry_ref, kmax_ref, hist_ref, *, tm, sub, pt, tkv, scale):
    i = pl.program_id(1)

    a = ATTN_WIDTH

    @pl.when((pl.program_id(0) == 0) & (i == 0))
    def _():
        w16_ref[:, 0:a] = (w_ref[0:a, :] * scale).T.astype(BF16)
        w16_ref[:, a:] = w_ref[a:, :].T.astype(BF16)

    @pl.when(i == 0)
    def _():
        carry_ref[...] = jnp.zeros_like(carry_ref)
        kmax_ref[...] = jnp.zeros_like(kmax_ref)
        hist_ref[...] = jnp.zeros_like(hist_ref)

    parts = [slice(r * sub, (r + 1) * sub) for r in range(tm // sub)]
    head_lane = lax.broadcasted_iota(jnp.int32, (sub, LANES), 1) < N_HEADS

    def pack3(v):
        pieces = _split3(jnp.where(head_lane, v, 0.0))
        out = pieces[0].astype(F32)
        for j in range(1, N_SPLIT):
            out = out + pltpu.roll(pieces[j].astype(F32), j * N_HEADS, 1)
        return out.astype(BF16)

    hs, vfs = [], []
    for rows in parts:
        hs.append(_rmsnorm(x_ref[0, rows, :], g_ref[...]).astype(BF16))
        vfs.append(_dot(hs[-1], w16_ref[:, 2 * a:3 * a + LANES]))
    qs, css = [], []
    for rows, h, vf in zip(parts, hs, vfs):
        r0 = rows.start
        vT_ref[0, r0 // tkv, :, r0 % tkv:r0 % tkv + sub] = vf[:, 0:a].T.astype(BF16)
        f = vf[:, a:a + LANES] + bf_ref[...]
        logf = (jnp.minimum(f, 0.0) - jnp.log1p(jnp.exp(-jnp.abs(f)))) * LOG2E
        qf32 = _dot(h, w16_ref[:, 0:a])
        qs.append(qf32.astype(BF16))
        qT_ref[0, r0 // pt, :, r0 % pt:r0 % pt + sub] = qf32.T.astype(BF16)
        css.append(_dot(tri_ref[...], pack3(logf)))
    ks, norms = [], []
    for h, q in zip(hs, qs):
        ks.append(_dot(h, w16_ref[:, a:2 * a]).astype(BF16))
        qf = q.astype(F32)
        kf = ks[-1].astype(F32)
        sq = jnp.concatenate([qf * qf, kf * kf], axis=1).astype(BF16)
        norms.append(jnp.max(_dot(sq, ind_ref[...]), axis=0, keepdims=True))

    c_end = carry_ref[...]
    c_ends = []
    for rows, cs, k in zip(parts, css, ks):
        c = cs + c_end
        for j in range(1, N_SPLIT):
            c = c + pltpu.roll(cs, LANES - j * N_HEADS, 1)
        c_end = c[sub - 1:sub]
        c_ends.append(c_end)
        bias = pack3(-c)
        for p in range(N_HEADS // HEADS_PER_STEP):
            kx_ref[0, rows, p * PAIR_W:p * PAIR_W + LANES] = k[:, p * LANES:(p + 1) * LANES]
            kx_ref[0, rows, p * PAIR_W + LANES:(p + 1) * PAIR_W] = bias

    per = pt // sub
    c_prev = carry_ref[...]
    for t in range(tm // pt):
        gi = i * (tm // pt) + t
        norm2 = functools.reduce(jnp.maximum, norms[t * per:(t + 1) * per])
        kmax = jnp.maximum(kmax_ref[...], norm2)
        kmax_ref[...] = kmax
        qk2 = NORM_SLACK * norm2 * pltpu.roll(kmax, LANES - N_HEADS, 1)
        gap = hist_ref[...] - c_prev - PRUNE_GAP
        tile = lax.broadcasted_iota(jnp.int32, gap.shape, 0)
        skip = (tile < gi) & (gap > 0.0) & (qk2 < gap * gap)
        gi_f = gi.astype(F32)
        kept = jnp.where(skip, gi_f, jnp.minimum(tile.astype(F32), gi_f))
        first_ref[0, t] = jnp.min(kept, axis=0, keepdims=True).astype(jnp.int32)
        c_prev = c_ends[(t + 1) * per - 1]
        hist_ref[pl.ds(gi, 1), :] = c_prev
    carry_ref[...] = c_prev


def _riders(items, steps, step_of):
    in_specs, out_specs, out_shapes = [], [], []
    for arr, row0, rows, transpose in items:
        cols = arr.shape[1]
        align = LANES if transpose else BF16_ROWS
        n_blocks = max(n for n in range(1, steps + 1)
                       if rows % n == 0 and (rows // n) % align == 0)
        blk = rows // n_blocks
        assert row0 % F32_ROWS == 0 and not (transpose and blk == cols)

        def block_of(*g, last=n_blocks - 1):
            return jnp.minimum(step_of(*g), last)

        if row0 % blk == 0:
            in_specs.append(pl.BlockSpec(
                (blk, cols), lambda *g, o=row0 // blk, f=block_of: (f(*g) + o, 0)))
        else:
            in_specs.append(pl.BlockSpec(
                (pl.Element(blk), pl.Element(cols)),
                lambda *g, r=row0, n=blk, f=block_of: (
                    pl.multiple_of(r + f(*g) * n, F32_ROWS), 0)))
        if transpose:
            out_specs.append(pl.BlockSpec((cols, blk), lambda *g, f=block_of: (0, f(*g))))
            out_shapes.append(jax.ShapeDtypeStruct((cols, rows), BF16))
        else:
            out_specs.append(pl.BlockSpec((blk, cols), lambda *g, f=block_of: (f(*g), 0)))
            out_shapes.append(jax.ShapeDtypeStruct((rows, cols), BF16))
    return in_specs, out_specs, out_shapes


def _cast_riders(cast_in, cast_out):
    for src, dst in zip(cast_in, cast_out):
        v = src[...]
        dst[...] = (v.T if dst.shape != src.shape else v).astype(BF16)


def _qkv_call(x, g, wT, bfp, tri, ind, scale):
    b, s, d = x.shape
    tm, pt, tkv = TM_QKV, PRUNE_TILE, TKV
    n_pairs = N_HEADS // HEADS_PER_STEP
    n_tiles = s // tm
    own_rows = 3 * ATTN_WIDTH + LANES
    kern = functools.partial(_qkv_kernel, tm=tm, sub=SUB_ROWS, pt=pt, tkv=tkv, scale=scale)
    return pl.pallas_call(
        kern,
        grid=(b, n_tiles),
        in_specs=[
            pl.BlockSpec((1, tm, d), lambda bi, i: (bi, i, 0)),
            _const_spec(g.shape),
            pl.BlockSpec((own_rows, d), lambda bi, i: (0, 0), pipeline_mode=pl.Buffered(1)),
            _const_spec(bfp.shape),
            _const_spec(tri.shape), _const_spec(ind.shape),
        ],
        out_specs=[
            pl.BlockSpec((1, tm // pt, ATTN_WIDTH, pt), lambda bi, i: (bi, i, 0, 0)),
            pl.BlockSpec((1, tm, n_pairs * PAIR_W), lambda bi, i: (bi, i, 0)),
            pl.BlockSpec((1, tm // tkv, ATTN_WIDTH, tkv), lambda bi, i: (bi, i, 0, 0)),
            pl.BlockSpec((1, tm // pt, 1, LANES), lambda bi, i: (bi, i, 0, 0)),
        ],
        out_shape=[
            jax.ShapeDtypeStruct((b, s // pt, ATTN_WIDTH, pt), BF16),
            jax.ShapeDtypeStruct((b, s, n_pairs * PAIR_W), BF16),
            jax.ShapeDtypeStruct((b, s // tkv, ATTN_WIDTH, tkv), BF16),
            jax.ShapeDtypeStruct((b, s // pt, 1, LANES), jnp.int32),
        ],
        scratch_shapes=[
            pltpu.VMEM((d, own_rows), BF16),
            pltpu.VMEM((1, LANES), F32),
            pltpu.VMEM((1, LANES), F32),
            pltpu.VMEM((s // pt, LANES), F32),
        ],
        compiler_params=pltpu.CompilerParams(
            dimension_semantics=("arbitrary", "arbitrary"),
            vmem_limit_bytes=VMEM_LIMIT),
        name="qkv",
    )(x, g, wT, bfp, tri, ind)


def _attn_kernel(*refs, tq, tkv, cw, tile, n_tiles, n_q, n_cast):
    (first_ref, qT_ref, kx_ref, vT_ref), refs = refs[:4], refs[4:]
    cast_in, o_ref, refs = refs[:n_cast], refs[n_cast], refs[n_cast + 1:]
    cast_out, (qx_ref, s0_ref, s1_ref, mx_ref, m_ref, acc_ref) = refs[:n_cast], refs[n_cast:]
    bi, pair = pl.program_id(0), pl.program_id(1)
    heads = range(HEADS_PER_STEP)
    n_diag = tq // tkv
    sub_q = tq // tile
    s_bufs = (s0_ref, s1_ref)
    row = lax.broadcasted_iota(jnp.int32, (LANES, tile), 0)

    def build_qx(qi):
        for d in range(sub_q):
            qT = qT_ref[0, qi * sub_q + d].astype(F32)
            for h in heads:
                mine = (row >= h * HEAD_DIM) & (row < (h + 1) * HEAD_DIM)
                qx_ref[h, 0:LANES, d * tile:(d + 1) * tile] = (
                    jnp.where(mine, qT, 0.0).astype(BF16))

    def reset_state():
        m_ref[...] = jnp.full_like(m_ref, MASKED)
        acc_ref[...] = jnp.zeros_like(acc_ref)

    def first_block(qi):
        t0 = n_tiles
        for d in range(sub_q):
            for h in heads:
                t0 = jnp.minimum(t0, first_ref[(bi * n_tiles + qi * sub_q + d) * N_HEADS
                                               + pair * HEADS_PER_STEP + h])
        j0 = jnp.minimum(t0 * (tile // tkv), qi * n_diag - 2)
        return 2 * lax.shift_right_logical(j0, 1)

    def units_for(d):
        out = []
        for h in heads:
            for c in range(tq // cw):
                if d is None:
                    out.append((h, c, tkv, None))
                    continue
                if (c + 1) * cw <= d * tkv:
                    continue
                kr = min(tkv, (c + 1) * cw - d * tkv)
                masked = c * cw < d * tkv + kr - 1
                mask = None
                if masked:
                    mask = (lax.broadcasted_iota(jnp.int32, (kr, cw), 0) + (d * tkv - c * cw)
                            <= lax.broadcasted_iota(jnp.int32, (kr, cw), 1))
                out.append((h, c, kr, mask))
        return out

    def qk_unit(j, buf, unit):
        h, c, kr, mask = unit
        cols = slice(c * cw, (c + 1) * cw)
        start = pl.multiple_of(j * tkv, tkv)
        s = _dot(kx_ref[0, pl.ds(start, kr), :], qx_ref[h, :, cols])
        if mask is not None:
            s = jnp.where(mask, s, MASKED)
        s_bufs[buf][h, 0:kr, cols] = s
        mx_ref[buf, h, :, cols] = jnp.max(s, axis=0, keepdims=True)

    def softmax_unit(buf, unit):
        h, c, kr, _ = unit
        cols = slice(c * cw, (c + 1) * cw)
        m_old = m_ref[h, :, cols]
        m_new = jnp.maximum(m_old, mx_ref[buf, h, :, cols])
        m_ref[h, :, cols] = m_new
        p = jnp.exp2(s_bufs[buf][h, 0:kr, cols] - m_new).astype(BF16)
        return p, jnp.exp2(m_old - m_new)

    def pv_unit(j, unit, p, alpha):
        h, c, kr, _ = unit
        cols = slice(c * cw, (c + 1) * cw)
        vx = jnp.concatenate(
            [vT_ref[0, j, h * HEAD_DIM:(h + 1) * HEAD_DIM, :][:, 0:kr],
             jnp.ones((BF16_ROWS, kr), BF16)], axis=0)
        acc_ref[h, :, cols] = alpha * acc_ref[h, :, cols] + _dot(vx, p)

    def step(j, buf, cur_d=None, next_d=None, next_j=None, before_next=None):
        cur = units_for(cur_d)
        nxt = units_for(next_d)
        next_j = j + 1 if next_j is None else next_j
        if before_next is not None:
            before_next()
        pending = None
        for i in range(max(len(cur), len(nxt))):
            if i < len(nxt):
                qk_unit(next_j, 1 - buf, nxt[i])
            if pending is not None:
                pv_unit(j, *pending)
                pending = None
            if i < len(cur):
                pending = (cur[i],) + softmax_unit(buf, cur[i])
        if pending is not None:
            pv_unit(j, *pending)

    def diagonal_and_finish(qi):
        n_full = qi * n_diag
        for d in range(n_diag - 1):
            step(n_full + d, d % 2, cur_d=d, next_d=d + 1)
        nq = jnp.minimum(qi + 1, n_q - 1)
        step(n_full + n_diag - 1, (n_diag - 1) % 2, cur_d=n_diag - 1, next_d=None,
             next_j=first_block(nq), before_next=lambda: build_qx(nq))
        outs = []
        for h in heads:
            a = acc_ref[h]
            outs.append(a[0:HEAD_DIM] / a[HEAD_DIM:HEAD_DIM + 1])
        o = jnp.concatenate(outs, axis=0)
        start = qi * tq if isinstance(qi, int) else pl.multiple_of(qi * tq, tq)
        o_ref[0, pl.ds(start, tq), :] = o.T.astype(BF16)
        reset_state()

    def full_pair(i, carry):
        step(2 * i, 0)
        step(2 * i + 1, 1)
        return carry

    def later_tile(qi, carry):
        n_full = qi * n_diag
        lax.fori_loop(lax.shift_right_logical(first_block(qi), 1),
                      qi * (n_diag // 2) - 1, full_pair, 0)
        step(n_full - 2, 0)
        step(n_full - 1, 1, next_d=0)
        diagonal_and_finish(qi)
        return carry

    rowx = lax.broadcasted_iota(jnp.int32, (LANES, tq), 0)
    for h in heads:
        head = pair * HEADS_PER_STEP + h
        ones = rowx == head
        for j in range(1, N_SPLIT):
            ones = ones | (rowx == head + j * N_HEADS)
        qx_ref[h, LANES:PAIR_W, :] = jnp.where(ones, 1.0, 0.0).astype(BF16)
    reset_state()
    build_qx(0)
    for unit in units_for(0):
        qk_unit(0, 0, unit)
    _cast_riders(cast_in, cast_out)
    diagonal_and_finish(0)
    lax.fori_loop(1, n_q, later_tile, 0)


def _attn_call(first, qT, kx, vT, casts):
    b, s = kx.shape[0], kx.shape[1]
    tq, tkv = TQ, TKV
    n_pairs = N_HEADS // HEADS_PER_STEP
    n_tiles = s // PRUNE_TILE
    kern = functools.partial(_attn_kernel, tq=tq, tkv=tkv, cw=CW, tile=PRUNE_TILE,
                             n_tiles=n_tiles, n_q=s // tq, n_cast=len(casts))
    r_in, r_out, r_shapes = _riders(casts, b * n_pairs, lambda bi, p, first: bi * n_pairs + p)
    grid_spec = pltpu.PrefetchScalarGridSpec(
        num_scalar_prefetch=1,
        grid=(b, n_pairs),
        in_specs=[
            pl.BlockSpec((1, n_tiles, LANES, PRUNE_TILE), lambda bi, p, first: (bi, 0, p, 0)),
            pl.BlockSpec((1, s, PAIR_W), lambda bi, p, first: (bi, 0, p)),
            pl.BlockSpec((1, s // tkv, LANES, tkv), lambda bi, p, first: (bi, 0, p, 0)),
        ] + r_in,
        out_specs=[pl.BlockSpec((1, s, LANES), lambda bi, p, first: (bi, 0, p))] + r_out,
        scratch_shapes=[
            pltpu.VMEM((HEADS_PER_STEP, PAIR_W, tq), BF16),
            pltpu.VMEM((HEADS_PER_STEP, tkv, tq), F32),
            pltpu.VMEM((HEADS_PER_STEP, tkv, tq), F32),
            pltpu.VMEM((2, HEADS_PER_STEP, 1, tq), F32),
            pltpu.VMEM((HEADS_PER_STEP, 1, tq), F32),
            pltpu.VMEM((HEADS_PER_STEP, HEAD_DIM + BF16_ROWS, tq), F32),
        ],
    )
    return pl.pallas_call(
        kern,
        grid_spec=grid_spec,
        out_shape=[jax.ShapeDtypeStruct((b, s, ATTN_WIDTH), BF16)] + r_shapes,
        compiler_params=pltpu.CompilerParams(
            dimension_semantics=("arbitrary", "arbitrary"),
            vmem_limit_bytes=VMEM_LIMIT),
        name="attn",
    )(first, qT, kx, vT, *[c[0] for c in casts])


def _mix_kernel(*refs, tm, sub, tiles_per_seq, d_model, n_cast):
    (x_ref, yb_ref, gpre_ref, gpost_ref, wcg_ref, cw_ref, wb_ref, wo_ref), refs = (
        refs[:8], refs[8:])
    cast_in, o_ref, refs = refs[:n_cast], refs[n_cast], refs[n_cast + 1:]
    cast_out, (ztail_ref,) = refs[:n_cast], refs[n_cast:]
    parts = [slice(r * sub, (r + 1) * sub) for r in range(tm // sub)]
    xs = [x_ref[rows, :] for rows in parts]
    projs = []
    for x in xs:
        h = _rmsnorm(x, gpre_ref[...]).astype(BF16)
        projs.append(_dot(h, wcg_ref[...]))
    _cast_riders(cast_in, cast_out)

    first = pl.program_id(0) % tiles_per_seq == 0
    zh = jnp.where(first, 0.0, ztail_ref[...])
    branches = []
    for rows, proj in zip(parts, projs):
        gate_b = proj[:, 0:CONV_CH]
        z = proj[:, CONV_CH:2 * CONV_CH] * proj[:, 2 * CONV_CH:3 * CONV_CH]
        zz = jnp.concatenate([zh, z], axis=0)
        zh = z[sub - HALO:sub]
        conv = cw_ref[CONV_K - 1:CONV_K, :] * z
        for back in range(1, CONV_K):
            shifted = pltpu.roll(zz, back, 0)[HALO:]
            conv = conv + cw_ref[CONV_K - 1 - back:CONV_K - back, :] * shifted
        ya = (gate_b * conv).astype(BF16)
        branches.append((_dot(ya, wb_ref[0:CONV_CH, :]),
                         _dot(yb_ref[rows, :], wb_ref[CONV_CH:CONV_CH + ATTN_WIDTH, :])))
    ztail_ref[...] = zh

    g0 = 3 * CONV_CH
    ms = []
    for proj, (ya_d, yb_d) in zip(projs, branches):
        gc = jax.nn.sigmoid(proj[:, g0:g0 + d_model])
        ga = jax.nn.sigmoid(proj[:, g0 + d_model:g0 + 2 * d_model])
        merged = (gc * ya_d + ga * yb_d).astype(BF16)
        ms.append(_dot(merged, wo_ref[...]))
    for rows, x, m in zip(parts, xs, ms):
        o_ref[rows, :] = x + _rmsnorm(m, gpost_ref[...])


def _mix_call(x2, yb2, gpre, gpost, wcg, cw, wb, wo, seq_len, casts):
    n, d = x2.shape
    tm = TM_MIX
    kern = functools.partial(_mix_kernel, tm=tm, sub=SUB_ROWS, tiles_per_seq=seq_len // tm,
                             d_model=d, n_cast=len(casts))
    r_in, r_out, r_shapes = _riders(casts, n // tm, lambda i: i)
    return pl.pallas_call(
        kern,
        grid=(n // tm,),
        in_specs=[
            pl.BlockSpec((tm, d), lambda i: (i, 0)),
            pl.BlockSpec((tm, ATTN_WIDTH), lambda i: (i, 0)),
            _const_spec(gpre.shape), _const_spec(gpost.shape), _const_spec(wcg.shape),
            _const_spec(cw.shape), _const_spec(wb.shape), _const_spec(wo.shape),
        ] + r_in,
        out_specs=[pl.BlockSpec((tm, d), lambda i: (i, 0))] + r_out,
        out_shape=[jax.ShapeDtypeStruct((n, d), F32)] + r_shapes,
        scratch_shapes=[pltpu.VMEM((HALO, CONV_CH), F32)],
        compiler_params=pltpu.CompilerParams(
            dimension_semantics=("arbitrary",), vmem_limit_bytes=VMEM_LIMIT),
        name="mix",
    )(x2, yb2, gpre, gpost, wcg, cw, wb, wo, *[c[0] for c in casts])


def _ffn_kernel(x_ref, gpre_ref, gpost_ref, wgu_ref, wd_ref, o_ref, *, hidden, sub):
    parts = [slice(r * sub, (r + 1) * sub) for r in range(x_ref.shape[0] // sub)]
    xs = [x_ref[rows, :] for rows in parts]
    gus = []
    for x in xs:
        h = _rmsnorm(x, gpre_ref[...]).astype(BF16)
        gus.append(_dot(h, wgu_ref[...]))
    fs = []
    for gu in gus:
        g = gu[:, 0:hidden]
        u = gu[:, hidden:2 * hidden]
        a = (g * jax.nn.sigmoid(g) * u).astype(BF16)
        fs.append(_dot(a, wd_ref[...]))
    for rows, x, f in zip(parts, xs, fs):
        o_ref[rows, :] = x + _rmsnorm(f, gpost_ref[...])


def _ffn_call(x2, gpre, gpost, wgu, wd):
    n, d = x2.shape
    tm = TM_FFN
    hidden = wd.shape[0]
    kern = functools.partial(_ffn_kernel, hidden=hidden, sub=SUB_ROWS)
    return pl.pallas_call(
        kern,
        grid=(n // tm,),
        in_specs=[
            pl.BlockSpec((tm, d), lambda i: (i, 0)),
            _const_spec(gpre.shape), _const_spec(gpost.shape),
            _const_spec(wgu.shape), _const_spec(wd.shape),
        ],
        out_specs=pl.BlockSpec((tm, d), lambda i: (i, 0)),
        out_shape=jax.ShapeDtypeStruct((n, d), F32),
        compiler_params=pltpu.CompilerParams(
            dimension_semantics=("arbitrary",), vmem_limit_bytes=VMEM_LIMIT),
        name="ffn",
    )(x2, gpre, gpost, wgu, wd)


def _norm_indicator():
    ind = np.zeros((2 * ATTN_WIDTH, LANES), np.float32)
    for which in range(2):
        for col in range(ATTN_WIDTH):
            ind[which * ATTN_WIDTH + col, which * N_HEADS + col // HEAD_DIM] = 1.0
    return jnp.asarray(ind, BF16)


def kernel(x, norm_mix_pre, norm_mix_post, w_in, b_f, conv_w, w_branch, w_out,
           norm_ffn_pre, norm_ffn_post, w_gate_up, w_down):
    b, s, d = x.shape
    depth = w_in.shape[0]
    a = ATTN_WIDTH
    tri = jnp.asarray(np.tril(np.ones((SUB_ROWS, SUB_ROWS), np.float32)), BF16)
    ind = _norm_indicator()
    scale = LOG2E / math.sqrt(HEAD_DIM)

    for l in range(depth):
        bfp = jnp.pad(b_f[l], (0, LANES - N_HEADS)).reshape(1, LANES)
        wT = w_in[l].T
        rest0 = 3 * a + N_HEADS
        qT, kx, vT, first = _qkv_call(x, norm_mix_pre[l].reshape(1, d), wT, bfp, tri, ind, scale)
        yb, wb, wo, wcg = _attn_call(
            first[:, :, 0, 0:N_HEADS].reshape(-1), qT, kx, vT,
            [(w_branch[l].reshape(CONV_CH + a, d), 0, CONV_CH + a, False),
             (w_out[l], 0, d, False),
             (wT, rest0, wT.shape[0] - rest0, True)])
        x2, wgu, wd = _mix_call(
            x.reshape(b * s, d), yb.reshape(b * s, a),
            norm_mix_pre[l].reshape(1, d), norm_mix_post[l].reshape(1, d),
            wcg, conv_w[l], wb, wo, s,
            [(w_gate_up[l], 0, w_gate_up.shape[1], False),
             (w_down[l], 0, w_down.shape[1], False)])
        x2 = _ffn_call(x2, norm_ffn_pre[l].reshape(1, d), norm_ffn_post[l].reshape(1, d),
                       wgu, wd)
        x = x2.reshape(b, s, d)
    return x
```

```python
import functools
import math

import jax
import jax.numpy as jnp
import numpy as np
from jax import lax
from jax.experimental import pallas as pl
from jax.experimental.pallas import tpu as pltpu

F32 = jnp.float32
BF16 = jnp.bfloat16

RMS_EPS = 1e-6
N_HEADS = 8
HEAD_DIM = 64
ATTN_WIDTH = N_HEADS * HEAD_DIM
CONV_CH = 512
CONV_K = 3
LANES = 128
F32_ROWS = 8
BF16_ROWS = 16
HEADS_PER_STEP = 2
PAIR_W = 2 * LANES
N_SPLIT = 3
MASKED = -1e30
LOG2E = math.log2(math.e)
PRUNE_GAP = 106.0 * LOG2E
NORM_SLACK = 4.0 * 1.04
VMEM_LIMIT = 60 * 1024 * 1024

TM_QKV = 1024
PRUNE_TILE = 512
TQ = 1024
TKV = 512
CW = 256
TM_MIX = 1024
TM_FFN = 1024
SUB_ROWS = 256
HALO = F32_ROWS


def _rmsnorm(x, g):
    ms = jnp.mean(x * x, axis=-1, keepdims=True)
    return (x * lax.rsqrt(ms + RMS_EPS)) * g


def _split3(v):
    parts = []
    r = v
    for _ in range(N_SPLIT):
        p = r.astype(BF16)
        parts.append(p)
        r = r - p.astype(F32)
    return parts


def _dot(a, b):
    return jnp.dot(a, b, preferred_element_type=F32)


def _const_spec(shape):
    nd = len(shape)
    return pl.BlockSpec(shape, lambda *_: (0,) * nd, pipeline_mode=pl.Buffered(1))


def _qkv_kernel(x_ref, g_ref, w_ref, bf_ref, tri_ref, ind_ref, qT_ref, kx_ref, vT_ref,
                first_ref, w16_ref, carry_ref, kmax_ref, hist_ref, *, tm, sub, pt, tkv, scale):
    i = pl.program_id(1)

    a = ATTN_WIDTH

    @pl.when((pl.program_id(0) == 0) & (i == 0))
    def _():
        w16_ref[:, 0:a] = (w_ref[0:a, :] * scale).T.astype(BF16)
        w16_ref[:, a:] = w_ref[a:, :].T.astype(BF16)

    @pl.when(i == 0)
    def _():
        carry_ref[...] = jnp.zeros_like(carry_ref)
        kmax_ref[...] = jnp.zeros_like(kmax_ref)
        hist_ref[...] = jnp.zeros_like(hist_ref)

    parts = [slice(r * sub, (r + 1) * sub) for r in range(tm // sub)]
    head_lane = lax.broadcasted_iota(jnp.int32, (sub, LANES), 1) < N_HEADS

    def pack3(v):
        pieces = _split3(jnp.where(head_lane, v, 0.0))
        out = pieces[0].astype(F32)
        for j in range(1, N_SPLIT):
            out = out + pltpu.roll(pieces[j].astype(F32), j * N_HEADS, 1)
        return out.astype(BF16)

    hs, vfs = [], []
    for rows in parts:
        hs.append(_rmsnorm(x_ref[0, rows, :], g_ref[...]).astype(BF16))
        vfs.append(_dot(hs[-1], w16_ref[:, 2 * a:3 * a + LANES]))
    qs, css = [], []
    for rows, h, vf in zip(parts, hs, vfs):
        r0 = rows.start
        vT_ref[0, r0 // tkv, :, r0 % tkv:r0 % tkv + sub] = vf[:, 0:a].T.astype(BF16)
        f = vf[:, a:a + LANES] + bf_ref[...]
        logf = (jnp.minimum(f, 0.0) - jnp.log1p(jnp.exp(-jnp.abs(f)))) * LOG2E
        qf32 = _dot(h, w16_ref[:, 0:a])
        qs.append(qf32.astype(BF16))
        qT_ref[0, r0 // pt, :, r0 % pt:r0 % pt + sub] = qf32.T.astype(BF16)
        css.append(_dot(tri_ref[...], pack3(logf)))
    ks, norms = [], []
    for h, q in zip(hs, qs):
        ks.append(_dot(h, w16_ref[:, a:2 * a]).astype(BF16))
        qf = q.astype(F32)
        kf = ks[-1].astype(F32)
        sq = jnp.concatenate([qf * qf, kf * kf], axis=1).astype(BF16)
        norms.append(jnp.max(_dot(sq, ind_ref[...]), axis=0, keepdims=True))

    c_end = carry_ref[...]
    c_ends = []
    for rows, cs, k in zip(parts, css, ks):
        c = cs + c_end
        for j in range(1, N_SPLIT):
            c = c + pltpu.roll(cs, LANES - j * N_HEADS, 1)
        c_end = c[sub - 1:sub]
        c_ends.append(c_end)
        bias = pack3(-c)
        for p in range(N_HEADS // HEADS_PER_STEP):
            kx_ref[0, rows, p * PAIR_W:p * PAIR_W + LANES] = k[:, p * LANES:(p + 1) * LANES]
            kx_ref[0, rows, p * PAIR_W + LANES:(p + 1) * PAIR_W] = bias

    per = pt // sub
    c_prev = carry_ref[...]
    for t in range(tm // pt):
        gi = i * (tm // pt) + t
        norm2 = functools.reduce(jnp.maximum, norms[t * per:(t + 1) * per])
        kmax = jnp.maximum(kmax_ref[...], norm2)
        kmax_ref[...] = kmax
        qk2 = NORM_SLACK * norm2 * pltpu.roll(kmax, LANES - N_HEADS, 1)
        gap = hist_ref[...] - c_prev - PRUNE_GAP
        tile = lax.broadcasted_iota(jnp.int32, gap.shape, 0)
        skip = (tile < gi) & (gap > 0.0) & (qk2 < gap * gap)
        gi_f = gi.astype(F32)
        kept = jnp.where(skip, gi_f, jnp.minimum(tile.astype(F32), gi_f))
        first_ref[0, t] = jnp.min(kept, axis=0, keepdims=True).astype(jnp.int32)
        c_prev = c_ends[(t + 1) * per - 1]
        hist_ref[pl.ds(gi, 1), :] = c_prev
    carry_ref[...] = c_prev


def _riders(items, steps, step_of):
    in_specs, out_specs, out_shapes = [], [], []
    for arr, row0, rows, transpose in items:
        cols = arr.shape[1]
        align = LANES if transpose else BF16_ROWS
        n_blocks = max(n for n in range(1, steps + 1)
                       if rows % n == 0 and (rows // n) % align == 0)
        blk = rows // n_blocks
        assert row0 % F32_ROWS == 0 and not (transpose and blk == cols)

        def block_of(*g, last=n_blocks - 1):
            return jnp.minimum(step_of(*g), last)

        if row0 % blk == 0:
            in_specs.append(pl.BlockSpec(
                (blk, cols), lambda *g, o=row0 // blk, f=block_of: (f(*g) + o, 0)))
        else:
            in_specs.append(pl.BlockSpec(
                (pl.Element(blk), pl.Element(cols)),
                lambda *g, r=row0, n=blk, f=block_of: (
                    pl.multiple_of(r + f(*g) * n, F32_ROWS), 0)))
        if transpose:
            out_specs.append(pl.BlockSpec((cols, blk), lambda *g, f=block_of: (0, f(*g))))
            out_shapes.append(jax.ShapeDtypeStruct((cols, rows), BF16))
        else:
            out_specs.append(pl.BlockSpec((blk, cols), lambda *g, f=block_of: (f(*g), 0)))
            out_shapes.append(jax.ShapeDtypeStruct((rows, cols), BF16))
    return in_specs, out_specs, out_shapes


def _cast_riders(cast_in, cast_out):
    for src, dst in zip(cast_in, cast_out):
        v = src[...]
        dst[...] = (v.T if dst.shape != src.shape else v).astype(BF16)


def _qkv_call(x, g, wT, bfp, tri, ind, scale):
    b, s, d = x.shape
    tm, pt, tkv = TM_QKV, PRUNE_TILE, TKV
    n_pairs = N_HEADS // HEADS_PER_STEP
    n_tiles = s // tm
    own_rows = 3 * ATTN_WIDTH + LANES
    kern = functools.partial(_qkv_kernel, tm=tm, sub=SUB_ROWS, pt=pt, tkv=tkv, scale=scale)
    return pl.pallas_call(
        kern,
        grid=(b, n_tiles),
        in_specs=[
            pl.BlockSpec((1, tm, d), lambda bi, i: (bi, i, 0)),
            _const_spec(g.shape),
            pl.BlockSpec((own_rows, d), lambda bi, i: (0, 0), pipeline_mode=pl.Buffered(1)),
            _const_spec(bfp.shape),
            _const_spec(tri.shape), _const_spec(ind.shape),
        ],
        out_specs=[
            pl.BlockSpec((1, tm // pt, ATTN_WIDTH, pt), lambda bi, i: (bi, i, 0, 0)),
            pl.BlockSpec((1, tm, n_pairs * PAIR_W), lambda bi, i: (bi, i, 0)),
            pl.BlockSpec((1, tm // tkv, ATTN_WIDTH, tkv), lambda bi, i: (bi, i, 0, 0)),
            pl.BlockSpec((1, tm // pt, 1, LANES), lambda bi, i: (bi, i, 0, 0)),
        ],
        out_shape=[
            jax.ShapeDtypeStruct((b, s // pt, ATTN_WIDTH, pt), BF16),
            jax.ShapeDtypeStruct((b, s, n_pairs * PAIR_W), BF16),
            jax.ShapeDtypeStruct((b, s // tkv, ATTN_WIDTH, tkv), BF16),
            jax.ShapeDtypeStruct((b, s // pt, 1, LANES), jnp.int32),
        ],
        scratch_shapes=[
            pltpu.VMEM((d, own_rows), BF16),
            pltpu.VMEM((1, LANES), F32),
            pltpu.VMEM((1, LANES), F32),
            pltpu.VMEM((s // pt, LANES), F32),
        ],
        compiler_params=pltpu.CompilerParams(
            dimension_semantics=("arbitrary", "arbitrary"),
            vmem_limit_bytes=VMEM_LIMIT),
        name="qkv",
    )(x, g, wT, bfp, tri, ind)


def _attn_kernel(*refs, tq, tkv, cw, tile, n_tiles, n_q, n_cast):
    (first_ref, qT_ref, kx_ref, vT_ref), refs = refs[:4], refs[4:]
    cast_in, o_ref, refs = refs[:n_cast], refs[n_cast], refs[n_cast + 1:]
    cast_out, (qx_ref, s0_ref, s1_ref, mx_ref, m_ref, acc_ref) = refs[:n_cast], refs[n_cast:]
    bi, pair = pl.program_id(0), pl.program_id(1)
    heads = range(HEADS_PER_STEP)
    n_diag = tq // tkv
    sub_q = tq // tile
    s_bufs = (s0_ref, s1_ref)
    row = lax.broadcasted_iota(jnp.int32, (LANES, tile), 0)

    def build_qx(qi):
        for d in range(sub_q):
            qT = qT_ref[0, qi * sub_q + d].astype(F32)
            for h in heads:
                mine = (row >= h * HEAD_DIM) & (row < (h + 1) * HEAD_DIM)
                qx_ref[h, 0:LANES, d * tile:(d + 1) * tile] = (
                    jnp.where(mine, qT, 0.0).astype(BF16))

    def reset_state():
        m_ref[...] = jnp.full_like(m_ref, MASKED)
        acc_ref[...] = jnp.zeros_like(acc_ref)

    def first_block(qi):
        t0 = n_tiles
        for d in range(sub_q):
            for h in heads:
                t0 = jnp.minimum(t0, first_ref[(bi * n_tiles + qi * sub_q + d) * N_HEADS
                                               + pair * HEADS_PER_STEP + h])
        j0 = jnp.minimum(t0 * (tile // tkv), qi * n_diag - 2)
        return 2 * lax.shift_right_logical(j0, 1)

    def units_for(d):
        out = []
        for h in heads:
            for c in range(tq // cw):
                if d is None:
                    out.append((h, c, tkv, None))
                    continue
                if (c + 1) * cw <= d * tkv:
                    continue
                kr = min(tkv, (c + 1) * cw - d * tkv)
                masked = c * cw < d * tkv + kr - 1
                mask = None
                if masked:
                    mask = (lax.broadcasted_iota(jnp.int32, (kr, cw), 0) + (d * tkv - c * cw)
                            <= lax.broadcasted_iota(jnp.int32, (kr, cw), 1))
                out.append((h, c, kr, mask))
        return out

    def qk_unit(j, buf, unit):
        h, c, kr, mask = unit
        cols = slice(c * cw, (c + 1) * cw)
        start = pl.multiple_of(j * tkv, tkv)
        s = _dot(kx_ref[0, pl.ds(start, kr), :], qx_ref[h, :, cols])
        if mask is not None:
            s = jnp.where(mask, s, MASKED)
        s_bufs[buf][h, 0:kr, cols] = s
        mx_ref[buf, h, :, cols] = jnp.max(s, axis=0, keepdims=True)

    def softmax_unit(buf, unit):
        h, c, kr, _ = unit
        cols = slice(c * cw, (c + 1) * cw)
        m_old = m_ref[h, :, cols]
        m_new = jnp.maximum(m_old, mx_ref[buf, h, :, cols])
        m_ref[h, :, cols] = m_new
        p = jnp.exp2(s_bufs[buf][h, 0:kr, cols] - m_new).astype(BF16)
        return p, jnp.exp2(m_old - m_new)

    def pv_unit(j, unit, p, alpha):
        h, c, kr, _ = unit
        cols = slice(c * cw, (c + 1) * cw)
        vx = jnp.concatenate(
            [vT_ref[0, j, h * HEAD_DIM:(h + 1) * HEAD_DIM, :][:, 0:kr],
             jnp.ones((BF16_ROWS, kr), BF16)], axis=0)
        acc_ref[h, :, cols] = alpha * acc_ref[h, :, cols] + _dot(vx, p)

    def step(j, buf, cur_d=None, next_d=None, next_j=None, before_next=None):
        cur = units_for(cur_d)
        nxt = units_for(next_d)
        next_j = j + 1 if next_j is None else next_j
        if before_next is not None:
            before_next()
        pending = None
        for i in range(max(len(cur), len(nxt))):
            if i < len(nxt):
                qk_unit(next_j, 1 - buf, nxt[i])
            if pending is not None:
                pv_unit(j, *pending)
                pending = None
            if i < len(cur):
                pending = (cur[i],) + softmax_unit(buf, cur[i])
        if pending is not None:
            pv_unit(j, *pending)

    def diagonal_and_finish(qi):
        n_full = qi * n_diag
        for d in range(n_diag - 1):
            step(n_full + d, d % 2, cur_d=d, next_d=d + 1)
        nq = jnp.minimum(qi + 1, n_q - 1)
        step(n_full + n_diag - 1, (n_diag - 1) % 2, cur_d=n_diag - 1, next_d=None,
             next_j=first_block(nq), before_next=lambda: build_qx(nq))
        outs = []
        for h in heads:
            a = acc_ref[h]
            outs.append(a[0:HEAD_DIM] / a[HEAD_DIM:HEAD_DIM + 1])
        o = jnp.concatenate(outs, axis=0)
        start = qi * tq if isinstance(qi, int) else pl.multiple_of(qi * tq, tq)
        o_ref[0, pl.ds(start, tq), :] = o.T.astype(BF16)
        reset_state()

    def full_pair(i, carry):
        step(2 * i, 0)
        step(2 * i + 1, 1)
        return carry

    def later_tile(qi, carry):
        n_full = qi * n_diag
        lax.fori_loop(lax.shift_right_logical(first_block(qi), 1),
                      qi * (n_diag // 2) - 1, full_pair, 0)
        step(n_full - 2, 0)
        step(n_full - 1, 1, next_d=0)
        diagonal_and_finish(qi)
        return carry

    rowx = lax.broadcasted_iota(jnp.int32, (LANES, tq), 0)
    for h in heads:
        head = pair * HEADS_PER_STEP + h
        ones = rowx == head
        for j in range(1, N_SPLIT):
            ones = ones | (rowx == head + j * N_HEADS)
        qx_ref[h, LANES:PAIR_W, :] = jnp.where(ones, 1.0, 0.0).astype(BF16)
    reset_state()
    build_qx(0)
    for unit in units_for(0):
        qk_unit(0, 0, unit)
    _cast_riders(cast_in, cast_out)
    diagonal_and_finish(0)
    lax.fori_loop(1, n_q, later_tile, 0)


def _attn_call(first, qT, kx, vT, casts):
    b, s = kx.shape[0], kx.shape[1]
    tq, tkv = TQ, TKV
    n_pairs = N_HEADS // HEADS_PER_STEP
    n_tiles = s // PRUNE_TILE
    kern = functools.partial(_attn_kernel, tq=tq, tkv=tkv, cw=CW, tile=PRUNE_TILE,
                             n_tiles=n_tiles, n_q=s // tq, n_cast=len(casts))
    r_in, r_out, r_shapes = _riders(casts, b * n_pairs, lambda bi, p, first: bi * n_pairs + p)
    grid_spec = pltpu.PrefetchScalarGridSpec(
        num_scalar_prefetch=1,
        grid=(b, n_pairs),
        in_specs=[
            pl.BlockSpec((1, n_tiles, LANES, PRUNE_TILE), lambda bi, p, first: (bi, 0, p, 0)),
            pl.BlockSpec((1, s, PAIR_W), lambda bi, p, first: (bi, 0, p)),
            pl.BlockSpec((1, s // tkv, LANES, tkv), lambda bi, p, first: (bi, 0, p, 0)),
        ] + r_in,
        out_specs=[pl.BlockSpec((1, s, LANES), lambda bi, p, first: (bi, 0, p))] + r_out,
        scratch_shapes=[
            pltpu.VMEM((HEADS_PER_STEP, PAIR_W, tq), BF16),
            pltpu.VMEM((HEADS_PER_STEP, tkv, tq), F32),
            pltpu.VMEM((HEADS_PER_STEP, tkv, tq), F32),
            pltpu.VMEM((2, HEADS_PER_STEP, 1, tq), F32),
            pltpu.VMEM((HEADS_PER_STEP, 1, tq), F32),
            pltpu.VMEM((HEADS_PER_STEP, HEAD_DIM + BF16_ROWS, tq), F32),
        ],
    )
    return pl.pallas_call(
        kern,
        grid_spec=grid_spec,
        out_shape=[jax.ShapeDtypeStruct((b, s, ATTN_WIDTH), BF16)] + r_shapes,
        compiler_params=pltpu.CompilerParams(
            dimension_semantics=("arbitrary", "arbitrary"),
            vmem_limit_bytes=VMEM_LIMIT),
        name="attn",
    )(first, qT, kx, vT, *[c[0] for c in casts])


def _mix_kernel(*refs, tm, sub, tiles_per_seq, d_model, n_cast):
    (x_ref, yb_ref, gpre_ref, gpost_ref, wcg_ref, cw_ref, wb_ref, wo_ref), refs = (
        refs[:8], refs[8:])
    cast_in, o_ref, refs = refs[:n_cast], refs[n_cast], refs[n_cast + 1:]
    cast_out, (ztail_ref,) = refs[:n_cast], refs[n_cast:]
    parts = [slice(r * sub, (r + 1) * sub) for r in range(tm // sub)]
    xs = [x_ref[rows, :] for rows in parts]
    projs = []
    for x in xs:
        h = _rmsnorm(x, gpre_ref[...]).astype(BF16)
        projs.append(_dot(h, wcg_ref[...]))
    _cast_riders(cast_in, cast_out)

    first = pl.program_id(0) % tiles_per_seq == 0
    zh = jnp.where(first, 0.0, ztail_ref[...])
    branches = []
    for rows, proj in zip(parts, projs):
        gate_b = proj[:, 0:CONV_CH]
        z = proj[:, CONV_CH:2 * CONV_CH] * proj[:, 2 * CONV_CH:3 * CONV_CH]
        zz = jnp.concatenate([zh, z], axis=0)
        zh = z[sub - HALO:sub]
        conv = cw_ref[CONV_K - 1:CONV_K, :] * z
        for back in range(1, CONV_K):
            shifted = pltpu.roll(zz, back, 0)[HALO:]
            conv = conv + cw_ref[CONV_K - 1 - back:CONV_K - back, :] * shifted
        ya = (gate_b * conv).astype(BF16)
        branches.append((_dot(ya, wb_ref[0:CONV_CH, :]),
                         _dot(yb_ref[rows, :], wb_ref[CONV_CH:CONV_CH + ATTN_WIDTH, :])))
    ztail_ref[...] = zh

    g0 = 3 * CONV_CH
    ms = []
    for proj, (ya_d, yb_d) in zip(projs, branches):
        gc = jax.nn.sigmoid(proj[:, g0:g0 + d_model])
        ga = jax.nn.sigmoid(proj[:, g0 + d_model:g0 + 2 * d_model])
        merged = (gc * ya_d + ga * yb_d).astype(BF16)
        ms.append(_dot(merged, wo_ref[...]))
    for rows, x, m in zip(parts, xs, ms):
        o_ref[rows, :] = x + _rmsnorm(m, gpost_ref[...])


def _mix_call(x2, yb2, gpre, gpost, wcg, cw, wb, wo, seq_len, casts):
    n, d = x2.shape
    tm = TM_MIX
    kern = functools.partial(_mix_kernel, tm=tm, sub=SUB_ROWS, tiles_per_seq=seq_len // tm,
                             d_model=d, n_cast=len(casts))
    r_in, r_out, r_shapes = _riders(casts, n // tm, lambda i: i)
    return pl.pallas_call(
        kern,
        grid=(n // tm,),
        in_specs=[
            pl.BlockSpec((tm, d), lambda i: (i, 0)),
            pl.BlockSpec((tm, ATTN_WIDTH), lambda i: (i, 0)),
            _const_spec(gpre.shape), _const_spec(gpost.shape), _const_spec(wcg.shape),
            _const_spec(cw.shape), _const_spec(wb.shape), _const_spec(wo.shape),
        ] + r_in,
        out_specs=[pl.BlockSpec((tm, d), lambda i: (i, 0))] + r_out,
        out_shape=[jax.ShapeDtypeStruct((n, d), F32)] + r_shapes,
        scratch_shapes=[pltpu.VMEM((HALO, CONV_CH), F32)],
        compiler_params=pltpu.CompilerParams(
            dimension_semantics=("arbitrary",), vmem_limit_bytes=VMEM_LIMIT),
        name="mix",
    )(x2, yb2, gpre, gpost, wcg, cw, wb, wo, *[c[0] for c in casts])


def _ffn_kernel(x_ref, gpre_ref, gpost_ref, wgu_hbm, wd_hbm, o_ref, wgu_ref, wd_ref,
                sem_ref, *, hidden, sub):
    copies = [
        pltpu.make_async_copy(wgu_hbm.at[:, pl.ds(0, hidden)],
                              wgu_ref.at[:, pl.ds(0, hidden)], sem_ref.at[0]),
        pltpu.make_async_copy(wgu_hbm.at[:, pl.ds(hidden, hidden)],
                              wgu_ref.at[:, pl.ds(hidden, hidden)], sem_ref.at[1]),
        pltpu.make_async_copy(wd_hbm, wd_ref, sem_ref.at[2]),
    ]

    def body(fetch):
        if fetch:
            for cp in copies:
                cp.start()
        parts = [slice(r * sub, (r + 1) * sub) for r in range(x_ref.shape[0] // sub)]
        xs = [x_ref[rows, :] for rows in parts]
        gus = []
        for r, x in enumerate(xs):
            h = _rmsnorm(x, gpre_ref[...]).astype(BF16)
            if fetch and r == 0:
                copies[0].wait()
                g = _dot(h, wgu_ref[:, 0:hidden])
                copies[1].wait()
                u = _dot(h, wgu_ref[:, hidden:2 * hidden])
            else:
                gu = _dot(h, wgu_ref[...])
                g = gu[:, 0:hidden]
                u = gu[:, hidden:2 * hidden]
            gus.append((g, u))
        fs = []
        for r, (g, u) in enumerate(gus):
            a = (g * jax.nn.sigmoid(g) * u).astype(BF16)
            if fetch and r == 0:
                copies[2].wait()
            fs.append(_dot(a, wd_ref[...]))
        for rows, x, f in zip(parts, xs, fs):
            o_ref[rows, :] = x + _rmsnorm(f, gpost_ref[...])

    first = pl.program_id(0) == 0
    pl.when(first)(lambda: body(True))
    pl.when(jnp.logical_not(first))(lambda: body(False))


def _ffn_call(x2, gpre, gpost, wgu, wd):
    n, d = x2.shape
    tm = TM_FFN
    hidden = wd.shape[0]
    kern = functools.partial(_ffn_kernel, hidden=hidden, sub=SUB_ROWS)
    return pl.pallas_call(
        kern,
        grid=(n // tm,),
        in_specs=[
            pl.BlockSpec((tm, d), lambda i: (i, 0)),
            _const_spec(gpre.shape), _const_spec(gpost.shape),
            pl.BlockSpec(memory_space=pl.ANY), pl.BlockSpec(memory_space=pl.ANY),
        ],
        out_specs=pl.BlockSpec((tm, d), lambda i: (i, 0)),
        out_shape=jax.ShapeDtypeStruct((n, d), F32),
        scratch_shapes=[pltpu.VMEM(wgu.shape, BF16), pltpu.VMEM(wd.shape, BF16),
                        pltpu.SemaphoreType.DMA((3,))],
        compiler_params=pltpu.CompilerParams(
            dimension_semantics=("arbitrary",), vmem_limit_bytes=VMEM_LIMIT),
        name="ffn",
    )(x2, gpre, gpost, wgu, wd)


def _norm_indicator():
    ind = np.zeros((2 * ATTN_WIDTH, LANES), np.float32)
    for which in range(2):
        for col in range(ATTN_WIDTH):
            ind[which * ATTN_WIDTH + col, which * N_HEADS + col // HEAD_DIM] = 1.0
    return jnp.asarray(ind, BF16)


def kernel(x, norm_mix_pre, norm_mix_post, w_in, b_f, conv_w, w_branch, w_out,
           norm_ffn_pre, norm_ffn_post, w_gate_up, w_down):
    b, s, d = x.shape
    depth = w_in.shape[0]
    a = ATTN_WIDTH
    tri = jnp.asarray(np.tril(np.ones((SUB_ROWS, SUB_ROWS), np.float32)), BF16)
    ind = _norm_indicator()
    scale = LOG2E / math.sqrt(HEAD_DIM)

    for l in range(depth):
        bfp = jnp.pad(b_f[l], (0, LANES - N_HEADS)).reshape(1, LANES)
        wT = w_in[l].T
        rest0 = 3 * a + N_HEADS
        qT, kx, vT, first = _qkv_call(x, norm_mix_pre[l].reshape(1, d), wT, bfp, tri, ind, scale)
        yb, wb, wo, wcg = _attn_call(
            first[:, :, 0, 0:N_HEADS].reshape(-1), qT, kx, vT,
            [(w_branch[l].reshape(CONV_CH + a, d), 0, CONV_CH + a, False),
             (w_out[l], 0, d, False),
             (wT, rest0, wT.shape[0] - rest0, True)])
        x2, wgu, wd = _mix_call(
            x.reshape(b * s, d), yb.reshape(b * s, a),
            norm_mix_pre[l].reshape(1, d), norm_mix_post[l].reshape(1, d),
            wcg, conv_w[l], wb, wo, s,
            [(w_gate_up[l], 0, w_gate_up.shape[1], False),
             (w_down[l], 0, w_down.shape[1], False)])
        x2 = _ffn_call(x2, norm_ffn_pre[l].reshape(1, d), norm_ffn_post[l].reshape(1, d),
                       wgu, wd)
        x = x2.reshape(b, s, d)
    return x
```

```python
import functools
import math

import jax
import jax.numpy as jnp
import numpy as np
from jax import lax
from jax.experimental import pallas as pl
from jax.experimental.pallas import tpu as pltpu

F32 = jnp.float32
BF16 = jnp.bfloat16

RMS_EPS = 1e-6
N_HEADS = 8
HEAD_DIM = 64
ATTN_WIDTH = N_HEADS * HEAD_DIM
CONV_CH = 512
CONV_K = 3
LANES = 128
F32_ROWS = 8
BF16_ROWS = 16
HEADS_PER_STEP = 2
PAIR_W = 2 * LANES
N_SPLIT = 3
MASKED = -1e30
LOG2E = math.log2(math.e)
PRUNE_GAP = 106.0 * LOG2E
NORM_SLACK = 4.0 * 1.04
VMEM_LIMIT = 60 * 1024 * 1024

TM_QKV = 1024
PRUNE_TILE = 512
TQ = 1024
TKV = 512
CW = 256
TM_MIX = 1024
TM_FFN = 1024
SUB_ROWS = 256
HALO = F32_ROWS


def _rmsnorm(x, g):
    ms = jnp.mean(x * x, axis=-1, keepdims=True)
    return (x * lax.rsqrt(ms + RMS_EPS)) * g


def _split3(v):
    parts = []
    r = v
    for _ in range(N_SPLIT):
        p = r.astype(BF16)
        parts.append(p)
        r = r - p.astype(F32)
    return parts


def _dot(a, b):
    return jnp.dot(a, b, preferred_element_type=F32)


def _const_spec(shape):
    nd = len(shape)
    return pl.BlockSpec(shape, lambda *_: (0,) * nd, pipeline_mode=pl.Buffered(1))


def _qkv_kernel(x_ref, g_ref, w_ref, bf_ref, tri_ref, ind_ref, qT_ref, kx_ref, vT_ref,
                first_ref, w16_ref, carry_ref, kmax_ref, hist_ref, *, tm, sub, pt, tkv, scale):
    i = pl.program_id(1)

    a = ATTN_WIDTH

    @pl.when((pl.program_id(0) == 0) & (i == 0))
    def _():
        w16_ref[:, 0:a] = (w_ref[0:a, :] * scale).T.astype(BF16)
        w16_ref[:, a:] = w_ref[a:, :].T.astype(BF16)

    @pl.when(i == 0)
    def _():
        carry_ref[...] = jnp.zeros_like(carry_ref)
        kmax_ref[...] = jnp.zeros_like(kmax_ref)
        hist_ref[...] = jnp.zeros_like(hist_ref)

    parts = [slice(r * sub, (r + 1) * sub) for r in range(tm // sub)]
    head_lane = lax.broadcasted_iota(jnp.int32, (sub, LANES), 1) < N_HEADS

    def pack3(v):
        pieces = _split3(jnp.where(head_lane, v, 0.0))
        out = pieces[0].astype(F32)
        for j in range(1, N_SPLIT):
            out = out + pltpu.roll(pieces[j].astype(F32), j * N_HEADS, 1)
        return out.astype(BF16)

    hs, vfs = [], []
    for rows in parts:
        hs.append(_rmsnorm(x_ref[0, rows, :], g_ref[...]).astype(BF16))
        vfs.append(_dot(hs[-1], w16_ref[:, 2 * a:3 * a + LANES]))
    qs, css = [], []
    for rows, h, vf in zip(parts, hs, vfs):
        r0 = rows.start
        vT_ref[0, r0 // tkv, :, r0 % tkv:r0 % tkv + sub] = vf[:, 0:a].T.astype(BF16)
        f = vf[:, a:a + LANES] + bf_ref[...]
        logf = (jnp.minimum(f, 0.0) - jnp.log1p(jnp.exp(-jnp.abs(f)))) * LOG2E
        qf32 = _dot(h, w16_ref[:, 0:a])
        qs.append(qf32.astype(BF16))
        qT_ref[0, r0 // pt, :, r0 % pt:r0 % pt + sub] = qf32.T.astype(BF16)
        css.append(_dot(tri_ref[...], pack3(logf)))
    ks, norms = [], []
    for h, q in zip(hs, qs):
        ks.append(_dot(h, w16_ref[:, a:2 * a]).astype(BF16))
        qf = q.astype(F32)
        kf = ks[-1].astype(F32)
        sq = jnp.concatenate([qf * qf, kf * kf], axis=1).astype(BF16)
        norms.append(jnp.max(_dot(sq, ind_ref[...]), axis=0, keepdims=True))

    c_end = carry_ref[...]
    c_ends = []
    for rows, cs, k in zip(parts, css, ks):
        c = cs + c_end
        for j in range(1, N_SPLIT):
            c = c + pltpu.roll(cs, LANES - j * N_HEADS, 1)
        c_end = c[sub - 1:sub]
        c_ends.append(c_end)
        bias = pack3(-c)
        for p in range(N_HEADS // HEADS_PER_STEP):
            kx_ref[0, rows, p * PAIR_W:p * PAIR_W + LANES] = k[:, p * LANES:(p + 1) * LANES]
            kx_ref[0, rows, p * PAIR_W + LANES:(p + 1) * PAIR_W] = bias

    per = pt // sub
    c_prev = carry_ref[...]
    for t in range(tm // pt):
        gi = i * (tm // pt) + t
        norm2 = functools.reduce(jnp.maximum, norms[t * per:(t + 1) * per])
        kmax = jnp.maximum(kmax_ref[...], norm2)
        kmax_ref[...] = kmax
        qk2 = NORM_SLACK * norm2 * pltpu.roll(kmax, LANES - N_HEADS, 1)
        gap = hist_ref[...] - c_prev - PRUNE_GAP
        tile = lax.broadcasted_iota(jnp.int32, gap.shape, 0)
        skip = (tile < gi) & (gap > 0.0) & (qk2 < gap * gap)
        gi_f = gi.astype(F32)
        kept = jnp.where(skip, gi_f, jnp.minimum(tile.astype(F32), gi_f))
        first_ref[0, t] = jnp.min(kept, axis=0, keepdims=True).astype(jnp.int32)
        c_prev = c_ends[(t + 1) * per - 1]
        hist_ref[pl.ds(gi, 1), :] = c_prev
    carry_ref[...] = c_prev


def _riders(items, steps, step_of):
    in_specs, out_specs, out_shapes = [], [], []
    for arr, row0, rows, transpose in items:
        cols = arr.shape[1]
        align = LANES if transpose else BF16_ROWS
        n_blocks = max(n for n in range(1, steps + 1)
                       if rows % n == 0 and (rows // n) % align == 0)
        blk = rows // n_blocks
        assert row0 % F32_ROWS == 0 and not (transpose and blk == cols)

        def block_of(*g, last=n_blocks - 1):
            return jnp.minimum(step_of(*g), last)

        if row0 % blk == 0:
            in_specs.append(pl.BlockSpec(
                (blk, cols), lambda *g, o=row0 // blk, f=block_of: (f(*g) + o, 0)))
        else:
            in_specs.append(pl.BlockSpec(
                (pl.Element(blk), pl.Element(cols)),
                lambda *g, r=row0, n=blk, f=block_of: (
                    pl.multiple_of(r + f(*g) * n, F32_ROWS), 0)))
        if transpose:
            out_specs.append(pl.BlockSpec((cols, blk), lambda *g, f=block_of: (0, f(*g))))
            out_shapes.append(jax.ShapeDtypeStruct((cols, rows), BF16))
        else:
            out_specs.append(pl.BlockSpec((blk, cols), lambda *g, f=block_of: (f(*g), 0)))
            out_shapes.append(jax.ShapeDtypeStruct((rows, cols), BF16))
    return in_specs, out_specs, out_shapes


def _cast_riders(cast_in, cast_out):
    for src, dst in zip(cast_in, cast_out):
        v = src[...]
        dst[...] = (v.T if dst.shape != src.shape else v).astype(BF16)


def _qkv_call(x, g, wT, bfp, tri, ind, scale):
    b, s, d = x.shape
    tm, pt, tkv = TM_QKV, PRUNE_TILE, TKV
    n_pairs = N_HEADS // HEADS_PER_STEP
    n_tiles = s // tm
    own_rows = 3 * ATTN_WIDTH + LANES
    kern = functools.partial(_qkv_kernel, tm=tm, sub=SUB_ROWS, pt=pt, tkv=tkv, scale=scale)
    return pl.pallas_call(
        kern,
        grid=(b, n_tiles),
        in_specs=[
            pl.BlockSpec((1, tm, d), lambda bi, i: (bi, i, 0)),
            _const_spec(g.shape),
            pl.BlockSpec((own_rows, d), lambda bi, i: (0, 0), pipeline_mode=pl.Buffered(1)),
            _const_spec(bfp.shape),
            _const_spec(tri.shape), _const_spec(ind.shape),
        ],
        out_specs=[
            pl.BlockSpec((1, tm // pt, ATTN_WIDTH, pt), lambda bi, i: (bi, i, 0, 0)),
            pl.BlockSpec((1, tm, n_pairs * PAIR_W), lambda bi, i: (bi, i, 0)),
            pl.BlockSpec((1, tm // tkv, ATTN_WIDTH, tkv), lambda bi, i: (bi, i, 0, 0)),
            pl.BlockSpec((1, tm // pt, 1, LANES), lambda bi, i: (bi, i, 0, 0)),
        ],
        out_shape=[
            jax.ShapeDtypeStruct((b, s // pt, ATTN_WIDTH, pt), BF16),
            jax.ShapeDtypeStruct((b, s, n_pairs * PAIR_W), BF16),
            jax.ShapeDtypeStruct((b, s // tkv, ATTN_WIDTH, tkv), BF16),
            jax.ShapeDtypeStruct((b, s // pt, 1, LANES), jnp.int32),
        ],
        scratch_shapes=[
            pltpu.VMEM((d, own_rows), BF16),
            pltpu.VMEM((1, LANES), F32),
            pltpu.VMEM((1, LANES), F32),
            pltpu.VMEM((s // pt, LANES), F32),
        ],
        compiler_params=pltpu.CompilerParams(
            dimension_semantics=("arbitrary", "arbitrary"),
            vmem_limit_bytes=VMEM_LIMIT),
        name="qkv",
    )(x, g, wT, bfp, tri, ind)


def _attn_kernel(*refs, tq, tkv, cw, tile, n_tiles, n_q, n_cast):
    (first_ref, qT_ref, kx_ref, vT_ref), refs = refs[:4], refs[4:]
    cast_in, o_ref, refs = refs[:n_cast], refs[n_cast], refs[n_cast + 1:]
    cast_out, (qx_ref, s0_ref, s1_ref, mx_ref, m_ref, acc_ref) = refs[:n_cast], refs[n_cast:]
    bi, pair = pl.program_id(0), pl.program_id(1)
    heads = range(HEADS_PER_STEP)
    n_diag = tq // tkv
    sub_q = tq // tile
    s_bufs = (s0_ref, s1_ref)
    row = lax.broadcasted_iota(jnp.int32, (LANES, tile), 0)

    def build_qx(qi):
        for d in range(sub_q):
            qT = qT_ref[0, qi * sub_q + d].astype(F32)
            for h in heads:
                mine = (row >= h * HEAD_DIM) & (row < (h + 1) * HEAD_DIM)
                qx_ref[h, 0:LANES, d * tile:(d + 1) * tile] = (
                    jnp.where(mine, qT, 0.0).astype(BF16))

    def reset_state():
        m_ref[...] = jnp.full_like(m_ref, MASKED)
        acc_ref[...] = jnp.zeros_like(acc_ref)

    def first_block(qi):
        t0 = n_tiles
        for d in range(sub_q):
            for h in heads:
                t0 = jnp.minimum(t0, first_ref[(bi * n_tiles + qi * sub_q + d) * LANES
                                               + pair * HEADS_PER_STEP + h])
        j0 = jnp.minimum(t0 * (tile // tkv), qi * n_diag - 2)
        return 2 * lax.shift_right_logical(j0, 1)

    def units_for(d):
        out = []
        for h in heads:
            for c in range(tq // cw):
                if d is None:
                    out.append((h, c, tkv, None))
                    continue
                if (c + 1) * cw <= d * tkv:
                    continue
                kr = min(tkv, (c + 1) * cw - d * tkv)
                masked = c * cw < d * tkv + kr - 1
                mask = None
                if masked:
                    mask = (lax.broadcasted_iota(jnp.int32, (kr, cw), 0) + (d * tkv - c * cw)
                            <= lax.broadcasted_iota(jnp.int32, (kr, cw), 1))
                out.append((h, c, kr, mask))
        return out

    def qk_unit(j, buf, unit):
        h, c, kr, mask = unit
        cols = slice(c * cw, (c + 1) * cw)
        start = pl.multiple_of(j * tkv, tkv)
        s = _dot(kx_ref[0, pl.ds(start, kr), :], qx_ref[h, :, cols])
        if mask is not None:
            s = jnp.where(mask, s, MASKED)
        s_bufs[buf][h, 0:kr, cols] = s
        mx_ref[buf, h, :, cols] = jnp.max(s, axis=0, keepdims=True)

    def softmax_unit(buf, unit):
        h, c, kr, _ = unit
        cols = slice(c * cw, (c + 1) * cw)
        m_old = m_ref[h, :, cols]
        m_new = jnp.maximum(m_old, mx_ref[buf, h, :, cols])
        m_ref[h, :, cols] = m_new
        p = jnp.exp2(s_bufs[buf][h, 0:kr, cols] - m_new).astype(BF16)
        return p, jnp.exp2(m_old - m_new)

    def pv_unit(j, unit, p, alpha):
        h, c, kr, _ = unit
        cols = slice(c * cw, (c + 1) * cw)
        vx = jnp.concatenate(
            [vT_ref[0, j, h * HEAD_DIM:(h + 1) * HEAD_DIM, :][:, 0:kr],
             jnp.ones((BF16_ROWS, kr), BF16)], axis=0)
        acc_ref[h, :, cols] = alpha * acc_ref[h, :, cols] + _dot(vx, p)

    def step(j, buf, cur_d=None, next_d=None, next_j=None, before_next=None):
        cur = units_for(cur_d)
        nxt = units_for(next_d)
        next_j = j + 1 if next_j is None else next_j
        if before_next is not None:
            before_next()
        pending = None
        for i in range(max(len(cur), len(nxt))):
            if i < len(nxt):
                qk_unit(next_j, 1 - buf, nxt[i])
            if pending is not None:
                pv_unit(j, *pending)
                pending = None
            if i < len(cur):
                pending = (cur[i],) + softmax_unit(buf, cur[i])
        if pending is not None:
            pv_unit(j, *pending)

    def diagonal_and_finish(qi):
        n_full = qi * n_diag
        for d in range(n_diag - 1):
            step(n_full + d, d % 2, cur_d=d, next_d=d + 1)
        nq = jnp.minimum(qi + 1, n_q - 1)
        step(n_full + n_diag - 1, (n_diag - 1) % 2, cur_d=n_diag - 1, next_d=None,
             next_j=first_block(nq), before_next=lambda: build_qx(nq))
        outs = []
        for h in heads:
            a = acc_ref[h]
            outs.append(a[0:HEAD_DIM] / a[HEAD_DIM:HEAD_DIM + 1])
        o = jnp.concatenate(outs, axis=0)
        start = qi * tq if isinstance(qi, int) else pl.multiple_of(qi * tq, tq)
        o_ref[0, pl.ds(start, tq), :] = o.T.astype(BF16)
        reset_state()

    def full_pair(i, carry):
        step(2 * i, 0)
        step(2 * i + 1, 1)
        return carry

    def later_tile(qi, carry):
        n_full = qi * n_diag
        lax.fori_loop(lax.shift_right_logical(first_block(qi), 1),
                      qi * (n_diag // 2) - 1, full_pair, 0)
        step(n_full - 2, 0)
        step(n_full - 1, 1, next_d=0)
        diagonal_and_finish(qi)
        return carry

    rowx = lax.broadcasted_iota(jnp.int32, (LANES, tq), 0)
    for h in heads:
        head = pair * HEADS_PER_STEP + h
        ones = rowx == head
        for j in range(1, N_SPLIT):
            ones = ones | (rowx == head + j * N_HEADS)
        qx_ref[h, LANES:PAIR_W, :] = jnp.where(ones, 1.0, 0.0).astype(BF16)
    reset_state()
    build_qx(0)
    for unit in units_for(0):
        qk_unit(0, 0, unit)
    _cast_riders(cast_in, cast_out)
    diagonal_and_finish(0)
    lax.fori_loop(1, n_q, later_tile, 0)


def _attn_call(first, qT, kx, vT, casts):
    b, s = kx.shape[0], kx.shape[1]
    tq, tkv = TQ, TKV
    n_pairs = N_HEADS // HEADS_PER_STEP
    n_tiles = s // PRUNE_TILE
    kern = functools.partial(_attn_kernel, tq=tq, tkv=tkv, cw=CW, tile=PRUNE_TILE,
                             n_tiles=n_tiles, n_q=s // tq, n_cast=len(casts))
    r_in, r_out, r_shapes = _riders(casts, b * n_pairs, lambda bi, p, first: bi * n_pairs + p)
    grid_spec = pltpu.PrefetchScalarGridSpec(
        num_scalar_prefetch=1,
        grid=(b, n_pairs),
        in_specs=[
            pl.BlockSpec((1, n_tiles, LANES, PRUNE_TILE), lambda bi, p, first: (bi, 0, p, 0)),
            pl.BlockSpec((1, s, PAIR_W), lambda bi, p, first: (bi, 0, p)),
            pl.BlockSpec((1, s // tkv, LANES, tkv), lambda bi, p, first: (bi, 0, p, 0)),
        ] + r_in,
        out_specs=[pl.BlockSpec((1, s, LANES), lambda bi, p, first: (bi, 0, p))] + r_out,
        scratch_shapes=[
            pltpu.VMEM((HEADS_PER_STEP, PAIR_W, tq), BF16),
            pltpu.VMEM((HEADS_PER_STEP, tkv, tq), F32),
            pltpu.VMEM((HEADS_PER_STEP, tkv, tq), F32),
            pltpu.VMEM((2, HEADS_PER_STEP, 1, tq), F32),
            pltpu.VMEM((HEADS_PER_STEP, 1, tq), F32),
            pltpu.VMEM((HEADS_PER_STEP, HEAD_DIM + BF16_ROWS, tq), F32),
        ],
    )
    return pl.pallas_call(
        kern,
        grid_spec=grid_spec,
        out_shape=[jax.ShapeDtypeStruct((b, s, ATTN_WIDTH), BF16)] + r_shapes,
        compiler_params=pltpu.CompilerParams(
            dimension_semantics=("arbitrary", "arbitrary"),
            vmem_limit_bytes=VMEM_LIMIT),
        name="attn",
    )(first, qT, kx, vT, *[c[0] for c in casts])


def _mix_kernel(*refs, tm, sub, tiles_per_seq, d_model, n_cast):
    (x_ref, yb_ref, gpre_ref, gpost_ref, wcg_ref, cw_ref, wb_ref, wo_ref), refs = (
        refs[:8], refs[8:])
    cast_in, o_ref, refs = refs[:n_cast], refs[n_cast], refs[n_cast + 1:]
    cast_out, (ztail_ref,) = refs[:n_cast], refs[n_cast:]
    parts = [slice(r * sub, (r + 1) * sub) for r in range(tm // sub)]
    xs = [x_ref[rows, :] for rows in parts]
    projs = []
    for x in xs:
        h = _rmsnorm(x, gpre_ref[...]).astype(BF16)
        projs.append(_dot(h, wcg_ref[...]))
    _cast_riders(cast_in, cast_out)

    first = pl.program_id(0) % tiles_per_seq == 0
    zh = jnp.where(first, 0.0, ztail_ref[...])
    branches = []
    for rows, proj in zip(parts, projs):
        gate_b = proj[:, 0:CONV_CH]
        z = proj[:, CONV_CH:2 * CONV_CH] * proj[:, 2 * CONV_CH:3 * CONV_CH]
        zz = jnp.concatenate([zh, z], axis=0)
        zh = z[sub - HALO:sub]
        conv = cw_ref[CONV_K - 1:CONV_K, :] * z
        for back in range(1, CONV_K):
            shifted = pltpu.roll(zz, back, 0)[HALO:]
            conv = conv + cw_ref[CONV_K - 1 - back:CONV_K - back, :] * shifted
        ya = (gate_b * conv).astype(BF16)
        branches.append((_dot(ya, wb_ref[0:CONV_CH, :]),
                         _dot(yb_ref[rows, :], wb_ref[CONV_CH:CONV_CH + ATTN_WIDTH, :])))
    ztail_ref[...] = zh

    g0 = 3 * CONV_CH
    ms = []
    for proj, (ya_d, yb_d) in zip(projs, branches):
        gc = jax.nn.sigmoid(proj[:, g0:g0 + d_model])
        ga = jax.nn.sigmoid(proj[:, g0 + d_model:g0 + 2 * d_model])
        merged = (gc * ya_d + ga * yb_d).astype(BF16)
        ms.append(_dot(merged, wo_ref[...]))
    for rows, x, m in zip(parts, xs, ms):
        o_ref[rows, :] = x + _rmsnorm(m, gpost_ref[...])


def _mix_call(x2, yb2, gpre, gpost, wcg, cw, wb, wo, seq_len, casts):
    n, d = x2.shape
    tm = TM_MIX
    kern = functools.partial(_mix_kernel, tm=tm, sub=SUB_ROWS, tiles_per_seq=seq_len // tm,
                             d_model=d, n_cast=len(casts))
    r_in, r_out, r_shapes = _riders(casts, n // tm, lambda i: i)
    return pl.pallas_call(
        kern,
        grid=(n // tm,),
        in_specs=[
            pl.BlockSpec((tm, d), lambda i: (i, 0)),
            pl.BlockSpec((tm, ATTN_WIDTH), lambda i: (i, 0)),
            _const_spec(gpre.shape), _const_spec(gpost.shape), _const_spec(wcg.shape),
            _const_spec(cw.shape), _const_spec(wb.shape), _const_spec(wo.shape),
        ] + r_in,
        out_specs=[pl.BlockSpec((tm, d), lambda i: (i, 0))] + r_out,
        out_shape=[jax.ShapeDtypeStruct((n, d), F32)] + r_shapes,
        scratch_shapes=[pltpu.VMEM((HALO, CONV_CH), F32)],
        compiler_params=pltpu.CompilerParams(
            dimension_semantics=("arbitrary",), vmem_limit_bytes=VMEM_LIMIT),
        name="mix",
    )(x2, yb2, gpre, gpost, wcg, cw, wb, wo, *[c[0] for c in casts])


def _ffn_kernel(x_ref, gpre_ref, gpost_ref, wgu_ref, wd_ref, o_ref, *, hidden, sub):
    parts = [slice(r * sub, (r + 1) * sub) for r in range(x_ref.shape[0] // sub)]
    xs = [x_ref[rows, :] for rows in parts]
    gus = []
    for x in xs:
        h = _rmsnorm(x, gpre_ref[...]).astype(BF16)
        gus.append(_dot(h, wgu_ref[...]))
    fs = []
    for gu in gus:
        g = gu[:, 0:hidden]
        u = gu[:, hidden:2 * hidden]
        a = (g * jax.nn.sigmoid(g) * u).astype(BF16)
        fs.append(_dot(a, wd_ref[...]))
    for rows, x, f in zip(parts, xs, fs):
        o_ref[rows, :] = x + _rmsnorm(f, gpost_ref[...])


def _ffn_call(x2, gpre, gpost, wgu, wd):
    n, d = x2.shape
    tm = TM_FFN
    hidden = wd.shape[0]
    kern = functools.partial(_ffn_kernel, hidden=hidden, sub=SUB_ROWS)
    return pl.pallas_call(
        kern,
        grid=(n // tm,),
        in_specs=[
            pl.BlockSpec((tm, d), lambda i: (i, 0)),
            _const_spec(gpre.shape), _const_spec(gpost.shape),
            _const_spec(wgu.shape), _const_spec(wd.shape),
        ],
        out_specs=pl.BlockSpec((tm, d), lambda i: (i, 0)),
        out_shape=jax.ShapeDtypeStruct((n, d), F32),
        compiler_params=pltpu.CompilerParams(
            dimension_semantics=("arbitrary",), vmem_limit_bytes=VMEM_LIMIT),
        name="ffn",
    )(x2, gpre, gpost, wgu, wd)


def _norm_indicator():
    ind = np.zeros((2 * ATTN_WIDTH, LANES), np.float32)
    for which in range(2):
        for col in range(ATTN_WIDTH):
            ind[which * ATTN_WIDTH + col, which * N_HEADS + col // HEAD_DIM] = 1.0
    return jnp.asarray(ind, BF16)


def kernel(x, norm_mix_pre, norm_mix_post, w_in, b_f, conv_w, w_branch, w_out,
           norm_ffn_pre, norm_ffn_post, w_gate_up, w_down):
    b, s, d = x.shape
    depth = w_in.shape[0]
    a = ATTN_WIDTH
    tri = jnp.asarray(np.tril(np.ones((SUB_ROWS, SUB_ROWS), np.float32)), BF16)
    ind = _norm_indicator()
    scale = LOG2E / math.sqrt(HEAD_DIM)

    for l in range(depth):
        bfp = jnp.pad(b_f[l], (0, LANES - N_HEADS)).reshape(1, LANES)
        wT = w_in[l].T
        rest0 = 3 * a + N_HEADS
        qT, kx, vT, first = _qkv_call(x, norm_mix_pre[l].reshape(1, d), wT, bfp, tri, ind, scale)
        yb, wb, wo, wcg = _attn_call(
            first.reshape(-1), qT, kx, vT,
            [(w_branch[l].reshape(CONV_CH + a, d), 0, CONV_CH + a, False),
             (w_out[l], 0, d, False),
             (wT, rest0, wT.shape[0] - rest0, True)])
        x2, wgu, wd = _mix_call(
            x.reshape(b * s, d), yb.reshape(b * s, a),
            norm_mix_pre[l].reshape(1, d), norm_mix_post[l].reshape(1, d),
            wcg, conv_w[l], wb, wo, s,
            [(w_gate_up[l], 0, w_gate_up.shape[1], False),
             (w_down[l], 0, w_down.shape[1], False)])
        x2 = _ffn_call(x2, norm_ffn_pre[l].reshape(1, d), norm_ffn_post[l].reshape(1, d),
                       wgu, wd)
        x = x2.reshape(b, s, d)
    return x
```

```python
import functools
import math

import jax
import jax.numpy as jnp
import numpy as np
from jax import lax
from jax.experimental import pallas as pl
from jax.experimental.pallas import tpu as pltpu

F32 = jnp.float32
BF16 = jnp.bfloat16

RMS_EPS = 1e-6
N_HEADS = 8
HEAD_DIM = 64
ATTN_WIDTH = N_HEADS * HEAD_DIM
CONV_CH = 512
CONV_K = 3
LANES = 128
F32_ROWS = 8
BF16_ROWS = 16
HEADS_PER_STEP = 2
PAIR_W = 2 * LANES
N_SPLIT = 3
MASKED = -1e30
LOG2E = math.log2(math.e)
PRUNE_GAP = 106.0 * LOG2E
NORM_SLACK = 4.0 * 1.04
VMEM_LIMIT = 60 * 1024 * 1024

TM_QKV = 1024
PRUNE_TILE = 512
TQ = 1024
TKV = 512
CW = 256
TM_MIX = 1024
TM_FFN = 1024
SUB_ROWS = 256
HALO = F32_ROWS


def _rmsnorm(x, g):
    ms = jnp.mean(x * x, axis=-1, keepdims=True)
    return (x * lax.rsqrt(ms + RMS_EPS)) * g


def _split3(v):
    parts = []
    r = v
    for _ in range(N_SPLIT):
        p = r.astype(BF16)
        parts.append(p)
        r = r - p.astype(F32)
    return parts


def _dot(a, b):
    return jnp.dot(a, b, preferred_element_type=F32)


def _const_spec(shape):
    nd = len(shape)
    return pl.BlockSpec(shape, lambda *_: (0,) * nd, pipeline_mode=pl.Buffered(1))


def _qkv_kernel(x_ref, g_ref, w_ref, bf_ref, tri_ref, ind_ref, qT_ref, kx_ref, vT_ref,
                first_ref, w16_ref, carry_ref, kmax_ref, hist_ref, *, tm, sub, pt, tkv, scale):
    i = pl.program_id(1)

    a = ATTN_WIDTH

    @pl.when((pl.program_id(0) == 0) & (i == 0))
    def _():
        w16_ref[:, 0:a] = (w_ref[0:a, :] * scale).T.astype(BF16)
        w16_ref[:, a:] = w_ref[a:, :].T.astype(BF16)

    @pl.when(i == 0)
    def _():
        carry_ref[...] = jnp.zeros_like(carry_ref)
        kmax_ref[...] = jnp.zeros_like(kmax_ref)
        hist_ref[...] = jnp.zeros_like(hist_ref)

    parts = [slice(r * sub, (r + 1) * sub) for r in range(tm // sub)]
    head_lane = lax.broadcasted_iota(jnp.int32, (sub, LANES), 1) < N_HEADS
    lane = lax.broadcasted_iota(jnp.int32, (1, LANES), 1)
    bias_row = jnp.zeros((1, LANES), F32)
    for head in range(N_HEADS):
        bias_row = jnp.where(lane == head, bf_ref[head], bias_row)

    def pack3(v):
        pieces = _split3(jnp.where(head_lane, v, 0.0))
        out = pieces[0].astype(F32)
        for j in range(1, N_SPLIT):
            out = out + pltpu.roll(pieces[j].astype(F32), j * N_HEADS, 1)
        return out.astype(BF16)

    hs, vfs = [], []
    for rows in parts:
        hs.append(_rmsnorm(x_ref[0, rows, :], g_ref[...]).astype(BF16))
        vfs.append(_dot(hs[-1], w16_ref[:, 2 * a:3 * a + LANES]))
    qs, css = [], []
    for rows, h, vf in zip(parts, hs, vfs):
        r0 = rows.start
        vT_ref[0, r0 // tkv, :, r0 % tkv:r0 % tkv + sub] = vf[:, 0:a].T.astype(BF16)
        f = vf[:, a:a + LANES] + bias_row
        logf = (jnp.minimum(f, 0.0) - jnp.log1p(jnp.exp(-jnp.abs(f)))) * LOG2E
        qf32 = _dot(h, w16_ref[:, 0:a])
        qs.append(qf32.astype(BF16))
        qT_ref[0, r0 // pt, :, r0 % pt:r0 % pt + sub] = qf32.T.astype(BF16)
        css.append(_dot(tri_ref[...], pack3(logf)))
    ks, norms = [], []
    for h, q in zip(hs, qs):
        ks.append(_dot(h, w16_ref[:, a:2 * a]).astype(BF16))
        qf = q.astype(F32)
        kf = ks[-1].astype(F32)
        sq = jnp.concatenate([qf * qf, kf * kf], axis=1).astype(BF16)
        norms.append(jnp.max(_dot(sq, ind_ref[...]), axis=0, keepdims=True))

    c_end = carry_ref[...]
    c_ends = []
    for rows, cs, k in zip(parts, css, ks):
        c = cs + c_end
        for j in range(1, N_SPLIT):
            c = c + pltpu.roll(cs, LANES - j * N_HEADS, 1)
        c_end = c[sub - 1:sub]
        c_ends.append(c_end)
        bias = pack3(-c)
        for p in range(N_HEADS // HEADS_PER_STEP):
            kx_ref[0, rows, p * PAIR_W:p * PAIR_W + LANES] = k[:, p * LANES:(p + 1) * LANES]
            kx_ref[0, rows, p * PAIR_W + LANES:(p + 1) * PAIR_W] = bias

    per = pt // sub
    c_prev = carry_ref[...]
    for t in range(tm // pt):
        gi = i * (tm // pt) + t
        norm2 = functools.reduce(jnp.maximum, norms[t * per:(t + 1) * per])
        kmax = jnp.maximum(kmax_ref[...], norm2)
        kmax_ref[...] = kmax
        qk2 = NORM_SLACK * norm2 * pltpu.roll(kmax, LANES - N_HEADS, 1)
        gap = hist_ref[...] - c_prev - PRUNE_GAP
        tile = lax.broadcasted_iota(jnp.int32, gap.shape, 0)
        skip = (tile < gi) & (gap > 0.0) & (qk2 < gap * gap)
        gi_f = gi.astype(F32)
        kept = jnp.where(skip, gi_f, jnp.minimum(tile.astype(F32), gi_f))
        first_ref[0, t] = jnp.min(kept, axis=0, keepdims=True).astype(jnp.int32)
        c_prev = c_ends[(t + 1) * per - 1]
        hist_ref[pl.ds(gi, 1), :] = c_prev
    carry_ref[...] = c_prev


def _riders(items, steps, step_of):
    in_specs, out_specs, out_shapes = [], [], []
    for arr, row0, rows, transpose in items:
        cols = arr.shape[1]
        align = LANES if transpose else BF16_ROWS
        n_blocks = max(n for n in range(1, steps + 1)
                       if rows % n == 0 and (rows // n) % align == 0)
        blk = rows // n_blocks
        assert row0 % F32_ROWS == 0 and not (transpose and blk == cols)

        def block_of(*g, last=n_blocks - 1):
            return jnp.minimum(step_of(*g), last)

        if row0 % blk == 0:
            in_specs.append(pl.BlockSpec(
                (blk, cols), lambda *g, o=row0 // blk, f=block_of: (f(*g) + o, 0)))
        else:
            in_specs.append(pl.BlockSpec(
                (pl.Element(blk), pl.Element(cols)),
                lambda *g, r=row0, n=blk, f=block_of: (
                    pl.multiple_of(r + f(*g) * n, F32_ROWS), 0)))
        if transpose:
            out_specs.append(pl.BlockSpec((cols, blk), lambda *g, f=block_of: (0, f(*g))))
            out_shapes.append(jax.ShapeDtypeStruct((cols, rows), BF16))
        else:
            out_specs.append(pl.BlockSpec((blk, cols), lambda *g, f=block_of: (f(*g), 0)))
            out_shapes.append(jax.ShapeDtypeStruct((rows, cols), BF16))
    return in_specs, out_specs, out_shapes


def _cast_riders(cast_in, cast_out):
    for src, dst in zip(cast_in, cast_out):
        v = src[...]
        dst[...] = (v.T if dst.shape != src.shape else v).astype(BF16)


def _qkv_call(x, g, wT, bfp, tri, ind, scale):
    b, s, d = x.shape
    tm, pt, tkv = TM_QKV, PRUNE_TILE, TKV
    n_pairs = N_HEADS // HEADS_PER_STEP
    n_tiles = s // tm
    own_rows = 3 * ATTN_WIDTH + LANES
    kern = functools.partial(_qkv_kernel, tm=tm, sub=SUB_ROWS, pt=pt, tkv=tkv, scale=scale)
    return pl.pallas_call(
        kern,
        grid=(b, n_tiles),
        in_specs=[
            pl.BlockSpec((1, tm, d), lambda bi, i: (bi, i, 0)),
            _const_spec(g.shape),
            pl.BlockSpec((own_rows, d), lambda bi, i: (0, 0), pipeline_mode=pl.Buffered(1)),
            pl.BlockSpec(memory_space=pltpu.SMEM),
            _const_spec(tri.shape), _const_spec(ind.shape),
        ],
        out_specs=[
            pl.BlockSpec((1, tm // pt, ATTN_WIDTH, pt), lambda bi, i: (bi, i, 0, 0)),
            pl.BlockSpec((1, tm, n_pairs * PAIR_W), lambda bi, i: (bi, i, 0)),
            pl.BlockSpec((1, tm // tkv, ATTN_WIDTH, tkv), lambda bi, i: (bi, i, 0, 0)),
            pl.BlockSpec((1, tm // pt, 1, LANES), lambda bi, i: (bi, i, 0, 0)),
        ],
        out_shape=[
            jax.ShapeDtypeStruct((b, s // pt, ATTN_WIDTH, pt), BF16),
            jax.ShapeDtypeStruct((b, s, n_pairs * PAIR_W), BF16),
            jax.ShapeDtypeStruct((b, s // tkv, ATTN_WIDTH, tkv), BF16),
            jax.ShapeDtypeStruct((b, s // pt, 1, LANES), jnp.int32),
        ],
        scratch_shapes=[
            pltpu.VMEM((d, own_rows), BF16),
            pltpu.VMEM((1, LANES), F32),
            pltpu.VMEM((1, LANES), F32),
            pltpu.VMEM((s // pt, LANES), F32),
        ],
        compiler_params=pltpu.CompilerParams(
            dimension_semantics=("arbitrary", "arbitrary"),
            vmem_limit_bytes=VMEM_LIMIT),
        name="qkv",
    )(x, g, wT, bfp, tri, ind)


def _attn_kernel(*refs, tq, tkv, cw, tile, n_tiles, n_q, n_cast):
    (first_ref, qT_ref, kx_ref, vT_ref), refs = refs[:4], refs[4:]
    cast_in, o_ref, refs = refs[:n_cast], refs[n_cast], refs[n_cast + 1:]
    cast_out, (qx_ref, s0_ref, s1_ref, mx_ref, m_ref, acc_ref) = refs[:n_cast], refs[n_cast:]
    bi, pair = pl.program_id(0), pl.program_id(1)
    heads = range(HEADS_PER_STEP)
    n_diag = tq // tkv
    sub_q = tq // tile
    s_bufs = (s0_ref, s1_ref)
    row = lax.broadcasted_iota(jnp.int32, (LANES, tile), 0)

    def build_qx(qi):
        for d in range(sub_q):
            qT = qT_ref[0, qi * sub_q + d].astype(F32)
            for h in heads:
                mine = (row >= h * HEAD_DIM) & (row < (h + 1) * HEAD_DIM)
                qx_ref[h, 0:LANES, d * tile:(d + 1) * tile] = (
                    jnp.where(mine, qT, 0.0).astype(BF16))

    def reset_state():
        m_ref[...] = jnp.full_like(m_ref, MASKED)
        acc_ref[...] = jnp.zeros_like(acc_ref)

    def first_block(qi):
        t0 = n_tiles
        for d in range(sub_q):
            for h in heads:
                t0 = jnp.minimum(t0, first_ref[(bi * n_tiles + qi * sub_q + d) * N_HEADS
                                               + pair * HEADS_PER_STEP + h])
        j0 = jnp.minimum(t0 * (tile // tkv), qi * n_diag - 2)
        return 2 * lax.shift_right_logical(j0, 1)

    def units_for(d):
        out = []
        for h in heads:
            for c in range(tq // cw):
                if d is None:
                    out.append((h, c, tkv, None))
                    continue
                if (c + 1) * cw <= d * tkv:
                    continue
                kr = min(tkv, (c + 1) * cw - d * tkv)
                masked = c * cw < d * tkv + kr - 1
                mask = None
                if masked:
                    mask = (lax.broadcasted_iota(jnp.int32, (kr, cw), 0) + (d * tkv - c * cw)
                            <= lax.broadcasted_iota(jnp.int32, (kr, cw), 1))
                out.append((h, c, kr, mask))
        return out

    def qk_unit(j, buf, unit):
        h, c, kr, mask = unit
        cols = slice(c * cw, (c + 1) * cw)
        start = pl.multiple_of(j * tkv, tkv)
        s = _dot(kx_ref[0, pl.ds(start, kr), :], qx_ref[h, :, cols])
        if mask is not None:
            s = jnp.where(mask, s, MASKED)
        s_bufs[buf][h, 0:kr, cols] = s
        mx_ref[buf, h, :, cols] = jnp.max(s, axis=0, keepdims=True)

    def softmax_unit(buf, unit):
        h, c, kr, _ = unit
        cols = slice(c * cw, (c + 1) * cw)
        m_old = m_ref[h, :, cols]
        m_new = jnp.maximum(m_old, mx_ref[buf, h, :, cols])
        m_ref[h, :, cols] = m_new
        p = jnp.exp2(s_bufs[buf][h, 0:kr, cols] - m_new).astype(BF16)
        return p, jnp.exp2(m_old - m_new)

    def pv_unit(j, unit, p, alpha):
        h, c, kr, _ = unit
        cols = slice(c * cw, (c + 1) * cw)
        vx = jnp.concatenate(
            [vT_ref[0, j, h * HEAD_DIM:(h + 1) * HEAD_DIM, :][:, 0:kr],
             jnp.ones((BF16_ROWS, kr), BF16)], axis=0)
        acc_ref[h, :, cols] = alpha * acc_ref[h, :, cols] + _dot(vx, p)

    def step(j, buf, cur_d=None, next_d=None, next_j=None, before_next=None):
        cur = units_for(cur_d)
        nxt = units_for(next_d)
        next_j = j + 1 if next_j is None else next_j
        if before_next is not None:
            before_next()
        pending = None
        for i in range(max(len(cur), len(nxt))):
            if i < len(nxt):
                qk_unit(next_j, 1 - buf, nxt[i])
            if pending is not None:
                pv_unit(j, *pending)
                pending = None
            if i < len(cur):
                pending = (cur[i],) + softmax_unit(buf, cur[i])
        if pending is not None:
            pv_unit(j, *pending)

    def diagonal_and_finish(qi):
        n_full = qi * n_diag
        for d in range(n_diag - 1):
            step(n_full + d, d % 2, cur_d=d, next_d=d + 1)
        nq = jnp.minimum(qi + 1, n_q - 1)
        step(n_full + n_diag - 1, (n_diag - 1) % 2, cur_d=n_diag - 1, next_d=None,
             next_j=first_block(nq), before_next=lambda: build_qx(nq))
        outs = []
        for h in heads:
            a = acc_ref[h]
            outs.append(a[0:HEAD_DIM] / a[HEAD_DIM:HEAD_DIM + 1])
        o = jnp.concatenate(outs, axis=0)
        start = qi * tq if isinstance(qi, int) else pl.multiple_of(qi * tq, tq)
        o_ref[0, pl.ds(start, tq), :] = o.T.astype(BF16)
        reset_state()

    def full_pair(i, carry):
        step(2 * i, 0)
        step(2 * i + 1, 1)
        return carry

    def later_tile(qi, carry):
        n_full = qi * n_diag
        lax.fori_loop(lax.shift_right_logical(first_block(qi), 1),
                      qi * (n_diag // 2) - 1, full_pair, 0)
        step(n_full - 2, 0)
        step(n_full - 1, 1, next_d=0)
        diagonal_and_finish(qi)
        return carry

    rowx = lax.broadcasted_iota(jnp.int32, (LANES, tq), 0)
    for h in heads:
        head = pair * HEADS_PER_STEP + h
        ones = rowx == head
        for j in range(1, N_SPLIT):
            ones = ones | (rowx == head + j * N_HEADS)
        qx_ref[h, LANES:PAIR_W, :] = jnp.where(ones, 1.0, 0.0).astype(BF16)
    reset_state()
    build_qx(0)
    for unit in units_for(0):
        qk_unit(0, 0, unit)
    _cast_riders(cast_in, cast_out)
    diagonal_and_finish(0)
    lax.fori_loop(1, n_q, later_tile, 0)


def _attn_call(first, qT, kx, vT, casts):
    b, s = kx.shape[0], kx.shape[1]
    tq, tkv = TQ, TKV
    n_pairs = N_HEADS // HEADS_PER_STEP
    n_tiles = s // PRUNE_TILE
    kern = functools.partial(_attn_kernel, tq=tq, tkv=tkv, cw=CW, tile=PRUNE_TILE,
                             n_tiles=n_tiles, n_q=s // tq, n_cast=len(casts))
    r_in, r_out, r_shapes = _riders(casts, b * n_pairs, lambda bi, p, first: bi * n_pairs + p)
    grid_spec = pltpu.PrefetchScalarGridSpec(
        num_scalar_prefetch=1,
        grid=(b, n_pairs),
        in_specs=[
            pl.BlockSpec((1, n_tiles, LANES, PRUNE_TILE), lambda bi, p, first: (bi, 0, p, 0)),
            pl.BlockSpec((1, s, PAIR_W), lambda bi, p, first: (bi, 0, p)),
            pl.BlockSpec((1, s // tkv, LANES, tkv), lambda bi, p, first: (bi, 0, p, 0)),
        ] + r_in,
        out_specs=[pl.BlockSpec((1, s, LANES), lambda bi, p, first: (bi, 0, p))] + r_out,
        scratch_shapes=[
            pltpu.VMEM((HEADS_PER_STEP, PAIR_W, tq), BF16),
            pltpu.VMEM((HEADS_PER_STEP, tkv, tq), F32),
            pltpu.VMEM((HEADS_PER_STEP, tkv, tq), F32),
            pltpu.VMEM((2, HEADS_PER_STEP, 1, tq), F32),
            pltpu.VMEM((HEADS_PER_STEP, 1, tq), F32),
            pltpu.VMEM((HEADS_PER_STEP, HEAD_DIM + BF16_ROWS, tq), F32),
        ],
    )
    return pl.pallas_call(
        kern,
        grid_spec=grid_spec,
        out_shape=[jax.ShapeDtypeStruct((b, s, ATTN_WIDTH), BF16)] + r_shapes,
        compiler_params=pltpu.CompilerParams(
            dimension_semantics=("arbitrary", "arbitrary"),
            vmem_limit_bytes=VMEM_LIMIT),
        name="attn",
    )(first, qT, kx, vT, *[c[0] for c in casts])


def _mix_kernel(*refs, tm, sub, tiles_per_seq, d_model, n_cast):
    (x_ref, yb_ref, gpre_ref, gpost_ref, wcg_ref, cw_ref, wb_ref, wo_ref), refs = (
        refs[:8], refs[8:])
    cast_in, o_ref, refs = refs[:n_cast], refs[n_cast], refs[n_cast + 1:]
    cast_out, (ztail_ref,) = refs[:n_cast], refs[n_cast:]
    parts = [slice(r * sub, (r + 1) * sub) for r in range(tm // sub)]
    xs = [x_ref[rows, :] for rows in parts]
    projs = []
    for x in xs:
        h = _rmsnorm(x, gpre_ref[...]).astype(BF16)
        projs.append(_dot(h, wcg_ref[...]))
    _cast_riders(cast_in, cast_out)

    first = pl.program_id(0) % tiles_per_seq == 0
    zh = jnp.where(first, 0.0, ztail_ref[...])
    branches = []
    for rows, proj in zip(parts, projs):
        gate_b = proj[:, 0:CONV_CH]
        z = proj[:, CONV_CH:2 * CONV_CH] * proj[:, 2 * CONV_CH:3 * CONV_CH]
        zz = jnp.concatenate([zh, z], axis=0)
        zh = z[sub - HALO:sub]
        conv = cw_ref[CONV_K - 1:CONV_K, :] * z
        for back in range(1, CONV_K):
            shifted = pltpu.roll(zz, back, 0)[HALO:]
            conv = conv + cw_ref[CONV_K - 1 - back:CONV_K - back, :] * shifted
        ya = (gate_b * conv).astype(BF16)
        branches.append((_dot(ya, wb_ref[0:CONV_CH, :]),
                         _dot(yb_ref[rows, :], wb_ref[CONV_CH:CONV_CH + ATTN_WIDTH, :])))
    ztail_ref[...] = zh

    g0 = 3 * CONV_CH
    ms = []
    for proj, (ya_d, yb_d) in zip(projs, branches):
        gc = jax.nn.sigmoid(proj[:, g0:g0 + d_model])
        ga = jax.nn.sigmoid(proj[:, g0 + d_model:g0 + 2 * d_model])
        merged = (gc * ya_d + ga * yb_d).astype(BF16)
        ms.append(_dot(merged, wo_ref[...]))
    for rows, x, m in zip(parts, xs, ms):
        o_ref[rows, :] = x + _rmsnorm(m, gpost_ref[...])


def _mix_call(x2, yb2, gpre, gpost, wcg, cw, wb, wo, seq_len, casts):
    n, d = x2.shape
    tm = TM_MIX
    kern = functools.partial(_mix_kernel, tm=tm, sub=SUB_ROWS, tiles_per_seq=seq_len // tm,
                             d_model=d, n_cast=len(casts))
    r_in, r_out, r_shapes = _riders(casts, n // tm, lambda i: i)
    return pl.pallas_call(
        kern,
        grid=(n // tm,),
        in_specs=[
            pl.BlockSpec((tm, d), lambda i: (i, 0)),
            pl.BlockSpec((tm, ATTN_WIDTH), lambda i: (i, 0)),
            _const_spec(gpre.shape), _const_spec(gpost.shape), _const_spec(wcg.shape),
            _const_spec(cw.shape), _const_spec(wb.shape), _const_spec(wo.shape),
        ] + r_in,
        out_specs=[pl.BlockSpec((tm, d), lambda i: (i, 0))] + r_out,
        out_shape=[jax.ShapeDtypeStruct((n, d), F32)] + r_shapes,
        scratch_shapes=[pltpu.VMEM((HALO, CONV_CH), F32)],
        compiler_params=pltpu.CompilerParams(
            dimension_semantics=("arbitrary",), vmem_limit_bytes=VMEM_LIMIT),
        name="mix",
    )(x2, yb2, gpre, gpost, wcg, cw, wb, wo, *[c[0] for c in casts])


def _ffn_kernel(x_ref, gpre_ref, gpost_ref, wgu_ref, wd_ref, o_ref, *, hidden, sub):
    parts = [slice(r * sub, (r + 1) * sub) for r in range(x_ref.shape[0] // sub)]
    xs = [x_ref[rows, :] for rows in parts]
    gus = []
    for x in xs:
        h = _rmsnorm(x, gpre_ref[...]).astype(BF16)
        gus.append(_dot(h, wgu_ref[...]))
    fs = []
    for gu in gus:
        g = gu[:, 0:hidden]
        u = gu[:, hidden:2 * hidden]
        a = (g * jax.nn.sigmoid(g) * u).astype(BF16)
        fs.append(_dot(a, wd_ref[...]))
    for rows, x, f in zip(parts, xs, fs):
        o_ref[rows, :] = x + _rmsnorm(f, gpost_ref[...])


def _ffn_call(x2, gpre, gpost, wgu, wd):
    n, d = x2.shape
    tm = TM_FFN
    hidden = wd.shape[0]
    kern = functools.partial(_ffn_kernel, hidden=hidden, sub=SUB_ROWS)
    return pl.pallas_call(
        kern,
        grid=(n // tm,),
        in_specs=[
            pl.BlockSpec((tm, d), lambda i: (i, 0)),
            _const_spec(gpre.shape), _const_spec(gpost.shape),
            _const_spec(wgu.shape), _const_spec(wd.shape),
        ],
        out_specs=pl.BlockSpec((tm, d), lambda i: (i, 0)),
        out_shape=jax.ShapeDtypeStruct((n, d), F32),
        compiler_params=pltpu.CompilerParams(
            dimension_semantics=("arbitrary",), vmem_limit_bytes=VMEM_LIMIT),
        name="ffn",
    )(x2, gpre, gpost, wgu, wd)


def _norm_indicator():
    ind = np.zeros((2 * ATTN_WIDTH, LANES), np.float32)
    for which in range(2):
        for col in range(ATTN_WIDTH):
            ind[which * ATTN_WIDTH + col, which * N_HEADS + col // HEAD_DIM] = 1.0
    return jnp.asarray(ind, BF16)


def kernel(x, norm_mix_pre, norm_mix_post, w_in, b_f, conv_w, w_branch, w_out,
           norm_ffn_pre, norm_ffn_post, w_gate_up, w_down):
    b, s, d = x.shape
    depth = w_in.shape[0]
    a = ATTN_WIDTH
    tri = jnp.asarray(np.tril(np.ones((SUB_ROWS, SUB_ROWS), np.float32)), BF16)
    ind = _norm_indicator()
    scale = LOG2E / math.sqrt(HEAD_DIM)

    for l in range(depth):
        bfp = b_f[l]
        wT = w_in[l].T
        rest0 = 3 * a + N_HEADS
        qT, kx, vT, first = _qkv_call(x, norm_mix_pre[l].reshape(1, d), wT, bfp, tri, ind, scale)
        yb, wb, wo, wcg = _attn_call(
            first[:, :, 0, 0:N_HEADS].reshape(-1), qT, kx, vT,
            [(w_branch[l].reshape(CONV_CH + a, d), 0, CONV_CH + a, False),
             (w_out[l], 0, d, False),
             (wT, rest0, wT.shape[0] - rest0, True)])
        x2, wgu, wd = _mix_call(
            x.reshape(b * s, d), yb.reshape(b * s, a),
            norm_mix_pre[l].reshape(1, d), norm_mix_post[l].reshape(1, d),
            wcg, conv_w[l], wb, wo, s,
            [(w_gate_up[l], 0, w_gate_up.shape[1], False),
             (w_down[l], 0, w_down.shape[1], False)])
        x2 = _ffn_call(x2, norm_ffn_pre[l].reshape(1, d), norm_ffn_post[l].reshape(1, d),
                       wgu, wd)
        x = x2.reshape(b, s, d)
    return x
```

```python
import functools
import math

import jax
import jax.numpy as jnp
import numpy as np
from jax import lax
from jax.experimental import pallas as pl
from jax.experimental.pallas import tpu as pltpu

F32 = jnp.float32
BF16 = jnp.bfloat16

RMS_EPS = 1e-6
N_HEADS = 8
HEAD_DIM = 64
ATTN_WIDTH = N_HEADS * HEAD_DIM
CONV_CH = 512
CONV_K = 3
LANES = 128
F32_ROWS = 8
BF16_ROWS = 16
HEADS_PER_STEP = 2
PAIR_W = 2 * LANES
N_SPLIT = 3
MASKED = -1e30
LOG2E = math.log2(math.e)
PRUNE_GAP = 106.0 * LOG2E
NORM_SLACK = 4.0 * 1.04
VMEM_LIMIT = 60 * 1024 * 1024

TM_QKV = 1024
PRUNE_TILE = 512
TQ = 1024
TKV = 512
CW = 256
TM_MIX = 1024
TM_FFN = 1024
SUB_ROWS = 256
HALO = F32_ROWS


def _rmsnorm(x, g):
    ms = jnp.mean(x * x, axis=-1, keepdims=True)
    return (x * lax.rsqrt(ms + RMS_EPS)) * g


def _split3(v):
    parts = []
    r = v
    for _ in range(N_SPLIT):
        p = r.astype(BF16)
        parts.append(p)
        r = r - p.astype(F32)
    return parts


def _dot(a, b):
    return jnp.dot(a, b, preferred_element_type=F32)


def _const_spec(shape):
    nd = len(shape)
    return pl.BlockSpec(shape, lambda *_: (0,) * nd, pipeline_mode=pl.Buffered(1))


def _qkv_kernel(x_ref, g_ref, w_ref, bf_ref, tri_ref, ind_ref, qT_ref, kx_ref, vT_ref,
                first_ref, w16_ref, carry_ref, kmax_ref, hist_ref, *, tm, sub, pt, tkv, scale):
    i = pl.program_id(1)

    a = ATTN_WIDTH

    @pl.when((pl.program_id(0) == 0) & (i == 0))
    def _():
        w16_ref[:, 0:a] = (w_ref[0:a, :] * scale).T.astype(BF16)
        w16_ref[:, a:] = w_ref[a:, :].T.astype(BF16)

    @pl.when(i == 0)
    def _():
        carry_ref[...] = jnp.zeros_like(carry_ref)
        kmax_ref[...] = jnp.zeros_like(kmax_ref)
        hist_ref[...] = jnp.zeros_like(hist_ref)

    parts = [slice(r * sub, (r + 1) * sub) for r in range(tm // sub)]
    head_lane = lax.broadcasted_iota(jnp.int32, (sub, LANES), 1) < N_HEADS
    lane = lax.broadcasted_iota(jnp.int32, (1, LANES), 1)
    bias_row = jnp.zeros((1, LANES), F32)
    for head in range(N_HEADS):
        bias_row = jnp.where(lane == head, bf_ref[head], bias_row)

    def pack3(v):
        pieces = _split3(jnp.where(head_lane, v, 0.0))
        out = pieces[0].astype(F32)
        for j in range(1, N_SPLIT):
            out = out + pltpu.roll(pieces[j].astype(F32), j * N_HEADS, 1)
        return out.astype(BF16)

    hs, vfs = [], []
    for rows in parts:
        hs.append(_rmsnorm(x_ref[0, rows, :], g_ref[...]).astype(BF16))
        vfs.append(_dot(hs[-1], w16_ref[:, 2 * a:3 * a + LANES]))
    qs, css = [], []
    for rows, h, vf in zip(parts, hs, vfs):
        r0 = rows.start
        vT_ref[0, r0 // tkv, :, r0 % tkv:r0 % tkv + sub] = vf[:, 0:a].T.astype(BF16)
        f = vf[:, a:a + LANES] + bias_row
        logf = (jnp.minimum(f, 0.0) - jnp.log1p(jnp.exp(-jnp.abs(f)))) * LOG2E
        qf32 = _dot(h, w16_ref[:, 0:a])
        qs.append(qf32.astype(BF16))
        qT_ref[0, r0 // pt, :, r0 % pt:r0 % pt + sub] = qf32.T.astype(BF16)
        css.append(_dot(tri_ref[...], pack3(logf)))
    ks, norms = [], []
    for h, q in zip(hs, qs):
        ks.append(_dot(h, w16_ref[:, a:2 * a]).astype(BF16))
        qf = q.astype(F32)
        kf = ks[-1].astype(F32)
        sq = jnp.concatenate([qf * qf, kf * kf], axis=1).astype(BF16)
        norms.append(jnp.max(_dot(sq, ind_ref[...]), axis=0, keepdims=True))

    c_end = carry_ref[...]
    c_ends = []
    for rows, cs, k in zip(parts, css, ks):
        c = cs + c_end
        for j in range(1, N_SPLIT):
            c = c + pltpu.roll(cs, LANES - j * N_HEADS, 1)
        c_end = c[sub - 1:sub]
        c_ends.append(c_end)
        bias = pack3(-c)
        for p in range(N_HEADS // HEADS_PER_STEP):
            kx_ref[0, rows, p * PAIR_W:p * PAIR_W + LANES] = k[:, p * LANES:(p + 1) * LANES]
            kx_ref[0, rows, p * PAIR_W + LANES:(p + 1) * PAIR_W] = bias

    per = pt // sub
    c_prev = carry_ref[...]
    for t in range(tm // pt):
        gi = i * (tm // pt) + t
        norm2 = functools.reduce(jnp.maximum, norms[t * per:(t + 1) * per])
        kmax = jnp.maximum(kmax_ref[...], norm2)
        kmax_ref[...] = kmax
        qk2 = NORM_SLACK * norm2 * pltpu.roll(kmax, LANES - N_HEADS, 1)
        gap = hist_ref[...] - c_prev - PRUNE_GAP
        tile = lax.broadcasted_iota(jnp.int32, gap.shape, 0)
        skip = (tile < gi) & (gap > 0.0) & (qk2 < gap * gap)
        gi_f = gi.astype(F32)
        kept = jnp.where(skip, gi_f, jnp.minimum(tile.astype(F32), gi_f))
        first_ref[0, t] = jnp.min(kept, axis=0, keepdims=True).astype(jnp.int32)
        c_prev = c_ends[(t + 1) * per - 1]
        hist_ref[pl.ds(gi, 1), :] = c_prev
    carry_ref[...] = c_prev


def _riders(items, steps, step_of):
    in_specs, out_specs, out_shapes = [], [], []
    for arr, row0, rows, transpose in items:
        cols = arr.shape[1]
        align = LANES if transpose else BF16_ROWS
        n_blocks = max(n for n in range(1, steps + 1)
                       if rows % n == 0 and (rows // n) % align == 0)
        blk = rows // n_blocks
        assert row0 % F32_ROWS == 0 and not (transpose and blk == cols)

        def block_of(*g, last=n_blocks - 1):
            return jnp.minimum(step_of(*g), last)

        if row0 % blk == 0:
            in_specs.append(pl.BlockSpec(
                (blk, cols), lambda *g, o=row0 // blk, f=block_of: (f(*g) + o, 0)))
        else:
            in_specs.append(pl.BlockSpec(
                (pl.Element(blk), pl.Element(cols)),
                lambda *g, r=row0, n=blk, f=block_of: (
                    pl.multiple_of(r + f(*g) * n, F32_ROWS), 0)))
        if transpose:
            out_specs.append(pl.BlockSpec((cols, blk), lambda *g, f=block_of: (0, f(*g))))
            out_shapes.append(jax.ShapeDtypeStruct((cols, rows), BF16))
        else:
            out_specs.append(pl.BlockSpec((blk, cols), lambda *g, f=block_of: (f(*g), 0)))
            out_shapes.append(jax.ShapeDtypeStruct((rows, cols), BF16))
    return in_specs, out_specs, out_shapes


def _cast_riders(cast_in, cast_out):
    for src, dst in zip(cast_in, cast_out):
        v = src[...]
        dst[...] = (v.T if dst.shape != src.shape else v).astype(BF16)


def _qkv_call(x, g, wT, bfp, tri, ind, scale):
    b, s, d = x.shape
    tm, pt, tkv = TM_QKV, PRUNE_TILE, TKV
    n_pairs = N_HEADS // HEADS_PER_STEP
    n_tiles = s // tm
    own_rows = 3 * ATTN_WIDTH + LANES
    kern = functools.partial(_qkv_kernel, tm=tm, sub=SUB_ROWS, pt=pt, tkv=tkv, scale=scale)
    return pl.pallas_call(
        kern,
        grid=(b, n_tiles),
        in_specs=[
            pl.BlockSpec((1, tm, d), lambda bi, i: (bi, i, 0)),
            _const_spec(g.shape),
            pl.BlockSpec((own_rows, d), lambda bi, i: (0, 0), pipeline_mode=pl.Buffered(1)),
            pl.BlockSpec(memory_space=pltpu.SMEM),
            _const_spec(tri.shape), _const_spec(ind.shape),
        ],
        out_specs=[
            pl.BlockSpec((1, tm // pt, ATTN_WIDTH, pt), lambda bi, i: (bi, i, 0, 0)),
            pl.BlockSpec((1, tm, n_pairs * PAIR_W), lambda bi, i: (bi, i, 0)),
            pl.BlockSpec((1, tm // tkv, ATTN_WIDTH, tkv), lambda bi, i: (bi, i, 0, 0)),
            pl.BlockSpec((1, tm // pt, 1, LANES), lambda bi, i: (bi, i, 0, 0)),
        ],
        out_shape=[
            jax.ShapeDtypeStruct((b, s // pt, ATTN_WIDTH, pt), BF16),
            jax.ShapeDtypeStruct((b, s, n_pairs * PAIR_W), BF16),
            jax.ShapeDtypeStruct((b, s // tkv, ATTN_WIDTH, tkv), BF16),
            jax.ShapeDtypeStruct((b, s // pt, 1, LANES), jnp.int32),
        ],
        scratch_shapes=[
            pltpu.VMEM((d, own_rows), BF16),
            pltpu.VMEM((1, LANES), F32),
            pltpu.VMEM((1, LANES), F32),
            pltpu.VMEM((s // pt, LANES), F32),
        ],
        compiler_params=pltpu.CompilerParams(
            dimension_semantics=("arbitrary", "arbitrary"),
            vmem_limit_bytes=VMEM_LIMIT),
        name="qkv",
    )(x, g, wT, bfp, tri, ind)


def _attn_kernel(*refs, tq, tkv, cw, tile, n_tiles, n_q, n_cast):
    (first_ref, qT_ref, kx_ref, vT_ref), refs = refs[:4], refs[4:]
    cast_in, o_ref, refs = refs[:n_cast], refs[n_cast], refs[n_cast + 1:]
    cast_out, (qx_ref, s0_ref, s1_ref, mx_ref, m_ref, acc_ref) = refs[:n_cast], refs[n_cast:]
    bi, pair = pl.program_id(0), pl.program_id(1)
    heads = range(HEADS_PER_STEP)
    n_diag = tq // tkv
    sub_q = tq // tile
    s_bufs = (s0_ref, s1_ref)
    row = lax.broadcasted_iota(jnp.int32, (LANES, tile), 0)

    def build_qx(qi):
        for d in range(sub_q):
            qT = qT_ref[0, qi * sub_q + d].astype(F32)
            for h in heads:
                mine = (row >= h * HEAD_DIM) & (row < (h + 1) * HEAD_DIM)
                qx_ref[h, 0:LANES, d * tile:(d + 1) * tile] = (
                    jnp.where(mine, qT, 0.0).astype(BF16))

    def reset_state():
        m_ref[...] = jnp.full_like(m_ref, MASKED)
        acc_ref[...] = jnp.zeros_like(acc_ref)

    def first_block(qi):
        t0 = n_tiles
        for d in range(sub_q):
            for h in heads:
                t0 = jnp.minimum(t0, first_ref[(bi * n_tiles + qi * sub_q + d) * LANES
                                               + pair * HEADS_PER_STEP + h])
        j0 = jnp.minimum(t0 * (tile // tkv), qi * n_diag - 2)
        return 2 * lax.shift_right_logical(j0, 1)

    def units_for(d):
        out = []
        for h in heads:
            for c in range(tq // cw):
                if d is None:
                    out.append((h, c, tkv, None))
                    continue
                if (c + 1) * cw <= d * tkv:
                    continue
                kr = min(tkv, (c + 1) * cw - d * tkv)
                masked = c * cw < d * tkv + kr - 1
                mask = None
                if masked:
                    mask = (lax.broadcasted_iota(jnp.int32, (kr, cw), 0) + (d * tkv - c * cw)
                            <= lax.broadcasted_iota(jnp.int32, (kr, cw), 1))
                out.append((h, c, kr, mask))
        return out

    def qk_unit(j, buf, unit):
        h, c, kr, mask = unit
        cols = slice(c * cw, (c + 1) * cw)
        start = pl.multiple_of(j * tkv, tkv)
        s = _dot(kx_ref[0, pl.ds(start, kr), :], qx_ref[h, :, cols])
        if mask is not None:
            s = jnp.where(mask, s, MASKED)
        s_bufs[buf][h, 0:kr, cols] = s
        mx_ref[buf, h, :, cols] = jnp.max(s, axis=0, keepdims=True)

    def softmax_unit(buf, unit):
        h, c, kr, _ = unit
        cols = slice(c * cw, (c + 1) * cw)
        m_old = m_ref[h, :, cols]
        m_new = jnp.maximum(m_old, mx_ref[buf, h, :, cols])
        m_ref[h, :, cols] = m_new
        p = jnp.exp2(s_bufs[buf][h, 0:kr, cols] - m_new).astype(BF16)
        return p, jnp.exp2(m_old - m_new)

    def pv_unit(j, unit, p, alpha):
        h, c, kr, _ = unit
        cols = slice(c * cw, (c + 1) * cw)
        vx = jnp.concatenate(
            [vT_ref[0, j, h * HEAD_DIM:(h + 1) * HEAD_DIM, :][:, 0:kr],
             jnp.ones((BF16_ROWS, kr), BF16)], axis=0)
        acc_ref[h, :, cols] = alpha * acc_ref[h, :, cols] + _dot(vx, p)

    def step(j, buf, cur_d=None, next_d=None, next_j=None, before_next=None):
        cur = units_for(cur_d)
        nxt = units_for(next_d)
        next_j = j + 1 if next_j is None else next_j
        if before_next is not None:
            before_next()
        pending = None
        for i in range(max(len(cur), len(nxt))):
            if i < len(nxt):
                qk_unit(next_j, 1 - buf, nxt[i])
            if pending is not None:
                pv_unit(j, *pending)
                pending = None
            if i < len(cur):
                pending = (cur[i],) + softmax_unit(buf, cur[i])
        if pending is not None:
            pv_unit(j, *pending)

    def diagonal_and_finish(qi):
        n_full = qi * n_diag
        for d in range(n_diag - 1):
            step(n_full + d, d % 2, cur_d=d, next_d=d + 1)
        nq = jnp.minimum(qi + 1, n_q - 1)
        step(n_full + n_diag - 1, (n_diag - 1) % 2, cur_d=n_diag - 1, next_d=None,
             next_j=first_block(nq), before_next=lambda: build_qx(nq))
        outs = []
        for h in heads:
            a = acc_ref[h]
            outs.append(a[0:HEAD_DIM] / a[HEAD_DIM:HEAD_DIM + 1])
        o = jnp.concatenate(outs, axis=0)
        start = qi * tq if isinstance(qi, int) else pl.multiple_of(qi * tq, tq)
        o_ref[0, pl.ds(start, tq), :] = o.T.astype(BF16)
        reset_state()

    def full_pair(i, carry):
        step(2 * i, 0)
        step(2 * i + 1, 1)
        return carry

    def later_tile(qi, carry):
        n_full = qi * n_diag
        lax.fori_loop(lax.shift_right_logical(first_block(qi), 1),
                      qi * (n_diag // 2) - 1, full_pair, 0)
        step(n_full - 2, 0)
        step(n_full - 1, 1, next_d=0)
        diagonal_and_finish(qi)
        return carry

    rowx = lax.broadcasted_iota(jnp.int32, (LANES, tq), 0)
    for h in heads:
        head = pair * HEADS_PER_STEP + h
        ones = rowx == head
        for j in range(1, N_SPLIT):
            ones = ones | (rowx == head + j * N_HEADS)
        qx_ref[h, LANES:PAIR_W, :] = jnp.where(ones, 1.0, 0.0).astype(BF16)
    reset_state()
    build_qx(0)
    for unit in units_for(0):
        qk_unit(0, 0, unit)
    _cast_riders(cast_in, cast_out)
    diagonal_and_finish(0)
    lax.fori_loop(1, n_q, later_tile, 0)


def _attn_call(first, qT, kx, vT, casts):
    b, s = kx.shape[0], kx.shape[1]
    tq, tkv = TQ, TKV
    n_pairs = N_HEADS // HEADS_PER_STEP
    n_tiles = s // PRUNE_TILE
    kern = functools.partial(_attn_kernel, tq=tq, tkv=tkv, cw=CW, tile=PRUNE_TILE,
                             n_tiles=n_tiles, n_q=s // tq, n_cast=len(casts))
    r_in, r_out, r_shapes = _riders(casts, b * n_pairs, lambda bi, p, first: bi * n_pairs + p)
    grid_spec = pltpu.PrefetchScalarGridSpec(
        num_scalar_prefetch=1,
        grid=(b, n_pairs),
        in_specs=[
            pl.BlockSpec((1, n_tiles, LANES, PRUNE_TILE), lambda bi, p, first: (bi, 0, p, 0)),
            pl.BlockSpec((1, s, PAIR_W), lambda bi, p, first: (bi, 0, p)),
            pl.BlockSpec((1, s // tkv, LANES, tkv), lambda bi, p, first: (bi, 0, p, 0)),
        ] + r_in,
        out_specs=[pl.BlockSpec((1, s, LANES), lambda bi, p, first: (bi, 0, p))] + r_out,
        scratch_shapes=[
            pltpu.VMEM((HEADS_PER_STEP, PAIR_W, tq), BF16),
            pltpu.VMEM((HEADS_PER_STEP, tkv, tq), F32),
            pltpu.VMEM((HEADS_PER_STEP, tkv, tq), F32),
            pltpu.VMEM((2, HEADS_PER_STEP, 1, tq), F32),
            pltpu.VMEM((HEADS_PER_STEP, 1, tq), F32),
            pltpu.VMEM((HEADS_PER_STEP, HEAD_DIM + BF16_ROWS, tq), F32),
        ],
    )
    return pl.pallas_call(
        kern,
        grid_spec=grid_spec,
        out_shape=[jax.ShapeDtypeStruct((b, s, ATTN_WIDTH), BF16)] + r_shapes,
        compiler_params=pltpu.CompilerParams(
            dimension_semantics=("arbitrary", "arbitrary"),
            vmem_limit_bytes=VMEM_LIMIT),
        name="attn",
    )(first, qT, kx, vT, *[c[0] for c in casts])


def _mix_kernel(*refs, tm, sub, tiles_per_seq, d_model, n_cast):
    (x_ref, yb_ref, gpre_ref, gpost_ref, wcg_ref, cw_ref, wb_ref, wo_ref), refs = (
        refs[:8], refs[8:])
    cast_in, o_ref, refs = refs[:n_cast], refs[n_cast], refs[n_cast + 1:]
    cast_out, (ztail_ref,) = refs[:n_cast], refs[n_cast:]
    parts = [slice(r * sub, (r + 1) * sub) for r in range(tm // sub)]
    xs = [x_ref[rows, :] for rows in parts]
    projs = []
    for x in xs:
        h = _rmsnorm(x, gpre_ref[...]).astype(BF16)
        projs.append(_dot(h, wcg_ref[...]))
    _cast_riders(cast_in, cast_out)

    first = pl.program_id(0) % tiles_per_seq == 0
    zh = jnp.where(first, 0.0, ztail_ref[...])
    branches = []
    for rows, proj in zip(parts, projs):
        gate_b = proj[:, 0:CONV_CH]
        z = proj[:, CONV_CH:2 * CONV_CH] * proj[:, 2 * CONV_CH:3 * CONV_CH]
        zz = jnp.concatenate([zh, z], axis=0)
        zh = z[sub - HALO:sub]
        conv = cw_ref[CONV_K - 1:CONV_K, :] * z
        for back in range(1, CONV_K):
            shifted = pltpu.roll(zz, back, 0)[HALO:]
            conv = conv + cw_ref[CONV_K - 1 - back:CONV_K - back, :] * shifted
        ya = (gate_b * conv).astype(BF16)
        branches.append((_dot(ya, wb_ref[0:CONV_CH, :]),
                         _dot(yb_ref[rows, :], wb_ref[CONV_CH:CONV_CH + ATTN_WIDTH, :])))
    ztail_ref[...] = zh

    g0 = 3 * CONV_CH
    ms = []
    for proj, (ya_d, yb_d) in zip(projs, branches):
        gc = jax.nn.sigmoid(proj[:, g0:g0 + d_model])
        ga = jax.nn.sigmoid(proj[:, g0 + d_model:g0 + 2 * d_model])
        merged = (gc * ya_d + ga * yb_d).astype(BF16)
        ms.append(_dot(merged, wo_ref[...]))
    for rows, x, m in zip(parts, xs, ms):
        o_ref[rows, :] = x + _rmsnorm(m, gpost_ref[...])


def _mix_call(x2, yb2, gpre, gpost, wcg, cw, wb, wo, seq_len, casts):
    n, d = x2.shape
    tm = TM_MIX
    kern = functools.partial(_mix_kernel, tm=tm, sub=SUB_ROWS, tiles_per_seq=seq_len // tm,
                             d_model=d, n_cast=len(casts))
    r_in, r_out, r_shapes = _riders(casts, n // tm, lambda i: i)
    return pl.pallas_call(
        kern,
        grid=(n // tm,),
        in_specs=[
            pl.BlockSpec((tm, d), lambda i: (i, 0)),
            pl.BlockSpec((tm, ATTN_WIDTH), lambda i: (i, 0)),
            _const_spec(gpre.shape), _const_spec(gpost.shape), _const_spec(wcg.shape),
            _const_spec(cw.shape), _const_spec(wb.shape), _const_spec(wo.shape),
        ] + r_in,
        out_specs=[pl.BlockSpec((tm, d), lambda i: (i, 0))] + r_out,
        out_shape=[jax.ShapeDtypeStruct((n, d), F32)] + r_shapes,
        scratch_shapes=[pltpu.VMEM((HALO, CONV_CH), F32)],
        compiler_params=pltpu.CompilerParams(
            dimension_semantics=("arbitrary",), vmem_limit_bytes=VMEM_LIMIT),
        name="mix",
    )(x2, yb2, gpre, gpost, wcg, cw, wb, wo, *[c[0] for c in casts])


def _ffn_kernel(x_ref, gpre_ref, gpost_ref, wgu_ref, wd_ref, o_ref, *, hidden, sub):
    parts = [slice(r * sub, (r + 1) * sub) for r in range(x_ref.shape[0] // sub)]
    xs = [x_ref[rows, :] for rows in parts]
    gus = []
    for x in xs:
        h = _rmsnorm(x, gpre_ref[...]).astype(BF16)
        gus.append(_dot(h, wgu_ref[...]))
    fs = []
    for gu in gus:
        g = gu[:, 0:hidden]
        u = gu[:, hidden:2 * hidden]
        a = (g * jax.nn.sigmoid(g) * u).astype(BF16)
        fs.append(_dot(a, wd_ref[...]))
    for rows, x, f in zip(parts, xs, fs):
        o_ref[rows, :] = x + _rmsnorm(f, gpost_ref[...])


def _ffn_call(x2, gpre, gpost, wgu, wd):
    n, d = x2.shape
    tm = TM_FFN
    hidden = wd.shape[0]
    kern = functools.partial(_ffn_kernel, hidden=hidden, sub=SUB_ROWS)
    return pl.pallas_call(
        kern,
        grid=(n // tm,),
        in_specs=[
            pl.BlockSpec((tm, d), lambda i: (i, 0)),
            _const_spec(gpre.shape), _const_spec(gpost.shape),
            _const_spec(wgu.shape), _const_spec(wd.shape),
        ],
        out_specs=pl.BlockSpec((tm, d), lambda i: (i, 0)),
        out_shape=jax.ShapeDtypeStruct((n, d), F32),
        compiler_params=pltpu.CompilerParams(
            dimension_semantics=("arbitrary",), vmem_limit_bytes=VMEM_LIMIT),
        name="ffn",
    )(x2, gpre, gpost, wgu, wd)


def _norm_indicator():
    ind = np.zeros((2 * ATTN_WIDTH, LANES), np.float32)
    for which in range(2):
        for col in range(ATTN_WIDTH):
            ind[which * ATTN_WIDTH + col, which * N_HEADS + col // HEAD_DIM] = 1.0
    return jnp.asarray(ind, BF16)


def kernel(x, norm_mix_pre, norm_mix_post, w_in, b_f, conv_w, w_branch, w_out,
           norm_ffn_pre, norm_ffn_post, w_gate_up, w_down):
    b, s, d = x.shape
    depth = w_in.shape[0]
    a = ATTN_WIDTH
    tri = jnp.asarray(np.tril(np.ones((SUB_ROWS, SUB_ROWS), np.float32)), BF16)
    ind = _norm_indicator()
    scale = LOG2E / math.sqrt(HEAD_DIM)

    for l in range(depth):
        bfp = b_f[l]
        wT = w_in[l].T
        rest0 = 3 * a + N_HEADS
        qT, kx, vT, first = _qkv_call(x, norm_mix_pre[l].reshape(1, d), wT, bfp, tri, ind, scale)
        yb, wb, wo, wcg = _attn_call(
            first.reshape(-1), qT, kx, vT,
            [(w_branch[l].reshape(CONV_CH + a, d), 0, CONV_CH + a, False),
             (w_out[l], 0, d, False),
             (wT, rest0, wT.shape[0] - rest0, True)])
        x2, wgu, wd = _mix_call(
            x.reshape(b * s, d), yb.reshape(b * s, a),
            norm_mix_pre[l].reshape(1, d), norm_mix_post[l].reshape(1, d),
            wcg, conv_w[l], wb, wo, s,
            [(w_gate_up[l], 0, w_gate_up.shape[1], False),
             (w_down[l], 0, w_down.shape[1], False)])
        x2 = _ffn_call(x2, norm_ffn_pre[l].reshape(1, d), norm_ffn_post[l].reshape(1, d),
                       wgu, wd)
        x = x2.reshape(b, s, d)
    return x
```
